```python
import jax
import jax.numpy as jnp
from jax import lax
import numpy as np

D_MODEL = 1024
BATCH = 2
SEQ = 8192
DEPTH = 4
DEC_BATCH = 32
DEC_SEQ = 8
PAST_LEN = 8192
PAGE_SIZE = 128

D_CONV = D_MODEL // 2
CONV_WIDTH = 31
N_HEADS = 8
HEAD_DIM = 64
N_KV = 2
Q_PER_KV = N_HEADS // N_KV
CMP_BLOCK = 64
SEL_BLOCK = CMP_BLOCK
N_SELECT = 16
WINDOW = 512
CMP_HIDDEN = 256
Q_BLOCK = 128
ROPE_THETA = 10000.0
ATTN_SCALE = HEAD_DIM ** -0.5
FORCED_SCORE = 1e4
N_EXPERTS = 16
N_GROUPS = 4
EXPERTS_PER_GROUP = N_EXPERTS // N_GROUPS
TOP_GROUPS = 1
TOP_K = 2
D_EXPERT = 256
N_KV_SLOTS = 4
EPS = 1e-6
IN_SPLITS = (2 * D_CONV, N_HEADS * HEAD_DIM, 6 * N_KV * HEAD_DIM, 3 * N_HEADS, 2 * D_MODEL)
D_IN = 2 * D_CONV + N_HEADS * HEAD_DIM + 6 * N_KV * HEAD_DIM + 3 * N_HEADS + 2 * D_MODEL

kernel_name = "hybrid_conformer_nsa_moe_step"


def rms_norm(x, g):
    xf = x.astype(jnp.float32)
    y = xf * lax.rsqrt(jnp.mean(xf * xf, axis=-1, keepdims=True) + EPS)
    return (y * g).astype(x.dtype)


def layer_norm(x, g, b):
    xf = x.astype(jnp.float32)
    mu = jnp.mean(xf, axis=-1, keepdims=True)
    var = jnp.mean(jnp.square(xf - mu), axis=-1, keepdims=True)
    return ((xf - mu) * lax.rsqrt(var + EPS) * g + b).astype(x.dtype)


def rope(x, pos):
    half = HEAD_DIM // 2
    inv = ROPE_THETA ** (-jnp.arange(half, dtype=jnp.float32) / half)
    ang = pos.astype(jnp.float32)[:, None] * inv[None, :]
    cos = jnp.cos(ang)[:, None, :]
    sin = jnp.sin(ang)[:, None, :]
    xf = x.astype(jnp.float32)
    x1, x2 = xf[..., :half], xf[..., half:]
    return jnp.concatenate([x1 * cos - x2 * sin, x2 * cos + x1 * sin], axis=-1).astype(x.dtype)


def masked_softmax(s, mask):
    s = jnp.where(mask, s.astype(jnp.float32), -1e30)
    m = jnp.max(s, axis=-1, keepdims=True)
    p = jnp.exp(s - m) * mask
    return p / jnp.maximum(jnp.sum(p, axis=-1, keepdims=True), 1e-30)


def map_query_blocks(fn, q_len):
    qb = Q_BLOCK if q_len % Q_BLOCK == 0 else q_len
    nb = q_len // qb
    if nb == 1:
        return fn(0, qb)
    out = lax.map(lambda i: fn(i * qb, qb), jnp.arange(nb, dtype=jnp.int32))
    out = jnp.moveaxis(out, 0, 1)
    return out.reshape(out.shape[0], q_len, *out.shape[3:])


def compress_blocks(rows, pe, w1, b1, w2, b2, n_blk):
    B = rows.shape[0]
    blk = rows[:, :n_blk * CMP_BLOCK].reshape(B, n_blk, CMP_BLOCK, N_KV, HEAD_DIM) + pe[None, None, :, None, :]
    flat = blk.transpose(0, 1, 3, 2, 4).reshape(B, n_blk, N_KV, CMP_BLOCK * HEAD_DIM)
    return jax.nn.gelu(flat @ w1 + b1) @ w2 + b2


def band_attend(q, qpos, k, v, kpos):
    B, T = q.shape[:2]
    qg = q.reshape(B, T, N_KV, Q_PER_KV, HEAD_DIM)
    s = jnp.einsum("bqgrh,bkgh->bgrqk", qg, k) * ATTN_SCALE
    dist = qpos[:, None] - kpos[None, :]
    mask = (dist >= 0) & (dist < WINDOW) & (kpos[None, :] >= 0)
    p = masked_softmax(s, mask)
    o = jnp.einsum("bgrqk,bkgh->bqgrh", p.astype(v.dtype), v)
    return o.reshape(B, T, N_HEADS, HEAD_DIM)


def compressed_and_selected(q, q_r, qpos, rows, prm):
    B, L = rows.shape[:2]
    T = q.shape[1]
    n_cmp = L // CMP_BLOCK
    k_c = rms_norm(compress_blocks(rows[:, :, 0], prm["cmp_pe"][0], prm["cmp_w1"][0], prm["cmp_b1"][0],
                                   prm["cmp_w2"][0], prm["cmp_b2"][0], n_cmp), prm["g_kc"])
    v_c = compress_blocks(rows[:, :, 1], prm["cmp_pe"][1], prm["cmp_w1"][1], prm["cmp_b1"][1],
                          prm["cmp_w2"][1], prm["cmp_b2"][1], n_cmp)
    qg = q.reshape(B, T, N_KV, Q_PER_KV, HEAD_DIM)
    s = jnp.einsum("bqgrh,bngh->bgrqn", qg, k_c) * ATTN_SCALE
    cmask = jnp.arange(n_cmp)[None, :] < ((qpos + 1) // CMP_BLOCK)[:, None]
    p_cmp = masked_softmax(s, cmask)
    o_cmp = jnp.einsum("bgrqn,bngh->bqgrh", p_cmp.astype(v_c.dtype), v_c).reshape(B, T, N_HEADS, HEAD_DIM)

    n_blk = -(-L // SEL_BLOCK)
    imp = jnp.pad(jnp.sum(p_cmp, axis=2), ((0, 0), (0, 0), (0, 0), (0, n_blk - n_cmp)))
    blk = jnp.arange(n_blk)[None, :]
    cur = (qpos // SEL_BLOCK)[:, None]
    imp = jnp.where((blk == cur) | (blk == 0), FORCED_SCORE, jnp.where(blk > cur, -1.0, imp))
    n_sel = min(N_SELECT, n_blk)
    _, idx = lax.top_k(imp, n_sel)

    pad = n_blk * SEL_BLOCK - L
    sel = jnp.pad(rows[:, :, 2:4], ((0, 0), (0, pad), (0, 0), (0, 0), (0, 0)))
    sel = sel.reshape(B, n_blk, SEL_BLOCK, 2, N_KV, HEAD_DIM).transpose(3, 0, 4, 1, 2, 5)
    k_blk, v_blk = sel[0], sel[1]
    b_ix = jnp.arange(B)[:, None, None, None]
    g_ix = jnp.arange(N_KV)[None, :, None, None]

    def sel_block(start, size):
        qb = lax.dynamic_slice_in_dim(q_r, start, size, 1).reshape(B, size, N_KV, Q_PER_KV, HEAD_DIM)
        ib = lax.dynamic_slice_in_dim(idx, start, size, 2)
        qp = lax.dynamic_slice_in_dim(qpos, start, size, 0)
        kg = k_blk[b_ix, g_ix, ib]
        vg = v_blk[b_ix, g_ix, ib]
        kpos = ib[..., None] * SEL_BLOCK + jnp.arange(SEL_BLOCK)
        sc = jnp.einsum("bqgrh,bgqnkh->bgrqnk", qb, kg) * ATTN_SCALE
        sc = sc.reshape(B, N_KV, Q_PER_KV, size, n_sel * SEL_BLOCK)
        m = (kpos <= qp[None, None, :, None, None]).reshape(B, N_KV, 1, size, n_sel * SEL_BLOCK)
        pr = masked_softmax(sc, m)
        o = jnp.einsum("bgrqk,bgqkh->bqgrh", pr.astype(vg.dtype),
                       vg.reshape(B, N_KV, size, n_sel * SEL_BLOCK, HEAD_DIM))
        return o.reshape(B, size, N_HEADS, HEAD_DIM)

    o_sel = map_query_blocks(sel_block, T)
    return o_cmp, o_sel


def token_mixer(h, pos0, conv_prefix, win_prefix, past_kv, win_buf, prm):
    B, T, _ = h.shape
    cuts = [int(c) for c in np.cumsum(IN_SPLITS)[:-1]]
    u, q, kv, nsa_g, merge_g = jnp.split(h @ prm["w_in"], cuts, axis=-1)

    u = u[..., :D_CONV] * jax.nn.sigmoid(u[..., D_CONV:])
    u_ext = jnp.concatenate([conv_prefix.astype(u.dtype), u], axis=1)
    y = lax.conv_general_dilated(u_ext, prm["w_dw"][:, None, :].astype(u.dtype), (1,), "VALID",
                                 dimension_numbers=("NWC", "WIO", "NWC"),
                                 feature_group_count=D_CONV) + prm["b_dw"]
    y = jax.nn.silu(layer_norm(y, prm["ln_conv_g"], prm["ln_conv_b"]))
    conv_out = y @ prm["w_conv_out"]
    conv_state = u_ext[:, u_ext.shape[1] - (CONV_WIDTH - 1):]

    pos = pos0 + jnp.arange(T, dtype=jnp.int32)
    q = rms_norm(q.reshape(B, T, N_HEADS, HEAD_DIM), prm["g_q"])
    q_r = rope(q, pos)
    kv = kv.reshape(B, T, 6, N_KV, HEAD_DIM)
    k_cmp, v_cmp, k_sel, v_sel, k_win, v_win = (kv[:, :, i] for i in range(6))
    k_sel = rope(rms_norm(k_sel, prm["g_ks"]), pos)
    k_win = rope(rms_norm(k_win, prm["g_kw"]), pos)
    new_rows = jnp.stack([k_cmp, v_cmp, k_sel, v_sel], axis=2)
    rows = new_rows if past_kv is None else jnp.concatenate([past_kv.astype(new_rows.dtype), new_rows], axis=1)
    o_cmp, o_sel = compressed_and_selected(q, q_r, pos, rows, prm)

    p_w = win_prefix.shape[1]
    win_all = jnp.concatenate(
        [jnp.pad(win_prefix.astype(h.dtype), ((0, 0), (WINDOW - p_w, 0), (0, 0), (0, 0), (0, 0))),
         jnp.stack([k_win, v_win], axis=2)], axis=1)
    kpos_all = pos0 - WINDOW + jnp.arange(WINDOW + T, dtype=jnp.int32)

    def win_block(start, size):
        qb = lax.dynamic_slice_in_dim(q_r, start, size, 1)
        wb = lax.dynamic_slice_in_dim(win_all, start, WINDOW + size, 1)
        kp = lax.dynamic_slice_in_dim(kpos_all, start, WINDOW + size, 0)
        return band_attend(qb, pos0 + start + jnp.arange(size, dtype=jnp.int32), wb[:, :, 0], wb[:, :, 1], kp)

    o_win = map_query_blocks(win_block, T)
    win_state = win_all[:, WINDOW + T - win_buf:]

    g3 = jax.nn.sigmoid(nsa_g).reshape(B, T, 3, N_HEADS, 1)
    o = g3[:, :, 0] * o_cmp + g3[:, :, 1] * o_sel + g3[:, :, 2] * o_win
    attn_out = o.reshape(B, T, N_HEADS * HEAD_DIM) @ prm["w_attn_out"]

    g_conv, g_attn = jnp.split(jax.nn.sigmoid(merge_g), 2, axis=-1)
    out = (g_conv * conv_out + g_attn * attn_out) @ prm["w_out"]
    return out, conv_state, win_state, new_rows


def moe(h, w_router, b_router, w_gate, w_up, w_down):
    aff = jax.nn.sigmoid((h @ w_router).astype(jnp.float32))
    score = aff + b_router.astype(jnp.float32)
    grp = score.reshape(*score.shape[:-1], N_GROUPS, EXPERTS_PER_GROUP)
    grp_score = jnp.sum(lax.top_k(grp, TOP_K)[0], axis=-1)
    top_grp = lax.top_k(grp_score, TOP_GROUPS)[1]
    grp_mask = jnp.any(top_grp[..., :, None] == jnp.arange(N_GROUPS), axis=-2)
    exp_mask = jnp.repeat(grp_mask, EXPERTS_PER_GROUP, axis=-1)
    top_idx = lax.top_k(jnp.where(exp_mask, score, -1e30), TOP_K)[1]
    top_aff = jnp.take_along_axis(aff, top_idx, axis=-1)
    wts = top_aff / jnp.sum(top_aff, axis=-1, keepdims=True)
    dense_w = jnp.sum(jax.nn.one_hot(top_idx, N_EXPERTS, dtype=jnp.float32) * wts[..., None], axis=-2)
    hid = jax.nn.silu(jnp.einsum("btd,edf->btef", h, w_gate)) * jnp.einsum("btd,edf->btef", h, w_up)
    hid = hid * dense_w[..., None].astype(hid.dtype)
    return jnp.einsum("btef,efd->btd", hid, w_down)


def decoder_layer(x, c, pos0, conv_prefix, win_prefix, past_kv, win_buf, prm, w_router, b_router):
    mod = (jax.nn.silu(c) @ prm["w_ada"] + prm["b_ada"])[:, None, :]
    sh1, sc1, gt1, sh2, sc2, gt2 = jnp.split(mod, 6, axis=-1)
    h = rms_norm(x, prm["g_norm_mix"]) * (1 + sc1) + sh1
    mix, conv_state, win_state, new_rows = token_mixer(h, pos0, conv_prefix, win_prefix, past_kv, win_buf, prm)
    x = x + gt1 * mix
    h = rms_norm(x, prm["g_norm_ffn"]) * (1 + sc2) + sh2
    x = x + gt2 * moe(h, w_router, b_router, prm["w_exp_gate"], prm["w_exp_up"], prm["w_exp_down"])
    return x, conv_state, win_state, new_rows


def _normal(key, shape, scale):
    return scale * jax.random.normal(key, shape, dtype=jnp.float32)


def setup_inputs(seed: int = 0) -> dict:
    key = jax.random.key(seed)
    ks = list(jax.random.split(key, 40))
    n_pages = PAST_LEN // PAGE_SIZE
    n_used = DEC_BATCH * n_pages
    n_phys = n_used + n_used // 4
    win_buf = min(WINDOW, PAST_LEN)
    page_table = jax.random.permutation(ks[0], n_phys)[:n_used].reshape(DEC_BATCH, n_pages).astype(jnp.int32)
    d = {}
    d["x_prompt"] = _normal(ks[1], (BATCH, SEQ, D_MODEL), 1.0)
    d["x_sample"] = _normal(ks[2], (DEC_BATCH, DEC_SEQ, D_MODEL), 1.0)
    d["c_prompt"] = _normal(ks[3], (BATCH, D_MODEL), 1.0)
    d["c_sample"] = _normal(ks[4], (DEC_BATCH, D_MODEL), 1.0)
    d["cache_kv"] = _normal(ks[5], (DEPTH, n_phys, PAGE_SIZE, N_KV_SLOTS, N_KV, HEAD_DIM), 1.0)
    d["state_win_kv"] = _normal(ks[6], (DEPTH, DEC_BATCH, win_buf, 2, N_KV, HEAD_DIM), 1.0)
    d["state_conv"] = _normal(ks[7], (DEPTH, DEC_BATCH, CONV_WIDTH - 1, D_CONV), 1.0)
    d["page_table"] = page_table
    d["w_ada"] = _normal(ks[8], (DEPTH, D_MODEL, 6 * D_MODEL), 0.5 * D_MODEL ** -0.5)
    d["b_ada"] = _normal(ks[9], (DEPTH, 6 * D_MODEL), 0.02)
    d["g_norm_mix"] = 1.0 + _normal(ks[10], (DEPTH, D_MODEL), 0.05)
    d["g_norm_ffn"] = 1.0 + _normal(ks[11], (DEPTH, D_MODEL), 0.05)
    d["w_in"] = _normal(ks[12], (DEPTH, D_MODEL, D_IN), D_MODEL ** -0.5)
    d["w_dw"] = _normal(ks[13], (DEPTH, CONV_WIDTH, D_CONV), CONV_WIDTH ** -0.5)
    d["b_dw"] = _normal(ks[14], (DEPTH, D_CONV), 0.02)
    d["ln_conv_g"] = 1.0 + _normal(ks[15], (DEPTH, D_CONV), 0.05)
    d["ln_conv_b"] = _normal(ks[16], (DEPTH, D_CONV), 0.02)
    d["w_conv_out"] = _normal(ks[17], (DEPTH, D_CONV, D_MODEL), D_CONV ** -0.5)
    d["g_q"] = 1.0 + _normal(ks[18], (DEPTH, HEAD_DIM), 0.05)
    d["g_kc"] = 1.0 + _normal(ks[19], (DEPTH, HEAD_DIM), 0.05)
    d["g_ks"] = 1.0 + _normal(ks[20], (DEPTH, HEAD_DIM), 0.05)
    d["g_kw"] = 1.0 + _normal(ks[21], (DEPTH, HEAD_DIM), 0.05)
    d["cmp_pe"] = _normal(ks[22], (DEPTH, 2, CMP_BLOCK, HEAD_DIM), 0.1)
    d["cmp_w1"] = _normal(ks[23], (DEPTH, 2, CMP_BLOCK * HEAD_DIM, CMP_HIDDEN), (CMP_BLOCK * HEAD_DIM) ** -0.5)
    d["cmp_b1"] = _normal(ks[24], (DEPTH, 2, CMP_HIDDEN), 0.02)
    d["cmp_w2"] = _normal(ks[25], (DEPTH, 2, CMP_HIDDEN, HEAD_DIM), CMP_HIDDEN ** -0.5)
    d["cmp_b2"] = _normal(ks[26], (DEPTH, 2, HEAD_DIM), 0.02)
    d["w_attn_out"] = _normal(ks[27], (DEPTH, N_HEADS * HEAD_DIM, D_MODEL), (N_HEADS * HEAD_DIM) ** -0.5)
    d["w_out"] = _normal(ks[28], (DEPTH, D_MODEL, D_MODEL), D_MODEL ** -0.5)
    d["w_router"] = _normal(ks[29], (D_MODEL, N_EXPERTS), D_MODEL ** -0.5)
    d["b_router"] = _normal(ks[30], (N_EXPERTS,), 0.01)
    d["w_exp_gate"] = _normal(ks[31], (DEPTH, N_EXPERTS, D_MODEL, D_EXPERT), D_MODEL ** -0.5)
    d["w_exp_up"] = _normal(ks[32], (DEPTH, N_EXPERTS, D_MODEL, D_EXPERT), D_MODEL ** -0.5)
    d["w_exp_down"] = _normal(ks[33], (DEPTH, N_EXPERTS, D_EXPERT, D_MODEL), D_EXPERT ** -0.5)
    return d


def reference(x_prompt, x_sample, c_prompt, c_sample, cache_kv, state_win_kv, state_conv, page_table,
              w_ada, b_ada, g_norm_mix, g_norm_ffn, w_in, w_dw, b_dw, ln_conv_g, ln_conv_b, w_conv_out,
              g_q, g_kc, g_ks, g_kw, cmp_pe, cmp_w1, cmp_b1, cmp_w2, cmp_b2, w_attn_out, w_out,
              w_router, b_router, w_exp_gate, w_exp_up, w_exp_down):
    B = x_prompt.shape[0]
    DB = x_sample.shape[0]
    n_pages = page_table.shape[1]
    past_len = n_pages * PAGE_SIZE
    win_buf = state_win_kv.shape[2]
    xp, xs = x_prompt, x_sample
    kv_p, kv_s, win_p, win_s, conv_p, conv_s = [], [], [], [], [], []
    conv_zero = jnp.zeros((B, CONV_WIDTH - 1, D_CONV), x_prompt.dtype)
    win_empty = jnp.zeros((B, 0, 2, N_KV, HEAD_DIM), x_prompt.dtype)
    for l in range(DEPTH):
        prm = {
            "w_ada": w_ada[l], "b_ada": b_ada[l], "g_norm_mix": g_norm_mix[l], "g_norm_ffn": g_norm_ffn[l],
            "w_in": w_in[l], "w_dw": w_dw[l], "b_dw": b_dw[l], "ln_conv_g": ln_conv_g[l],
            "ln_conv_b": ln_conv_b[l], "w_conv_out": w_conv_out[l], "g_q": g_q[l], "g_kc": g_kc[l],
            "g_ks": g_ks[l], "g_kw": g_kw[l], "cmp_pe": cmp_pe[l], "cmp_w1": cmp_w1[l], "cmp_b1": cmp_b1[l],
            "cmp_w2": cmp_w2[l], "cmp_b2": cmp_b2[l], "w_attn_out": w_attn_out[l], "w_out": w_out[l],
            "w_exp_gate": w_exp_gate[l], "w_exp_up": w_exp_up[l], "w_exp_down": w_exp_down[l],
        }
        xp, cs, ws, rows = decoder_layer(xp, c_prompt, 0, conv_zero, win_empty, None, win_buf, prm,
                                         w_router, b_router)
        kv_p.append(rows)
        win_p.append(ws)
        conv_p.append(cs)
        past = cache_kv[l][page_table].reshape(DB, past_len, N_KV_SLOTS, N_KV, HEAD_DIM)
        xs, cs, ws, rows = decoder_layer(xs, c_sample, past_len, state_conv[l], state_win_kv[l], past, win_buf,
                                         prm, w_router, b_router)
        kv_s.append(rows)
        win_s.append(ws)
        conv_s.append(cs)
    return (xp, xs, jnp.stack(kv_p), jnp.stack(kv_s), jnp.stack(win_p), jnp.stack(win_s),
            jnp.stack(conv_p), jnp.stack(conv_s))
```

```python
import functools

import jax
import jax.numpy as jnp
from jax import lax
from jax.experimental import pallas as pl
from jax.experimental.pallas import tpu as pltpu

F32 = jnp.float32
BF16 = jnp.bfloat16

D_MODEL = 1024
D_CONV = 512
CONV_WIDTH = 31
CONV_HALO = 32
N_HEADS = 8
HEAD_DIM = 64
N_KV = 2
Q_PER_KV = 4
CMP_BLOCK = 64
N_SELECT = 16
WINDOW = 512
N_EXPERTS = 16
N_GROUPS = 4
EXPERTS_PER_GROUP = 4
D_EXPERT = 256
PAGE_SIZE = 128
ROPE_THETA = 10000.0
ATTN_SCALE = HEAD_DIM ** -0.5
FORCED_SCORE = 1e4
EPS = 1e-6
NEG = -1e30
LANES = 128
KEY_CHUNK = 128
PAGES_PER_STEP = 8
VMEM_LIMIT = 56 * 1024 * 1024


def _params(*sem):
    return pltpu.CompilerParams(dimension_semantics=sem, vmem_limit_bytes=VMEM_LIMIT)


def _dot(a, b):
    return jnp.dot(a, b, preferred_element_type=F32)


def _dot_t0(a, b):
    return lax.dot_general(a, b, (((0,), (0,)), ((), ())), preferred_element_type=F32)


def _shift(d):
    assert d & (d - 1) == 0
    return d.bit_length() - 1


def _div(x, d):
    return lax.shift_right_logical(x, jnp.int32(_shift(d)))


def _mod(x, d):
    return x & jnp.int32(d - 1)


def _sigmoid(x):
    return 1.0 / (1.0 + jnp.exp(-x))


def _silu(x):
    return x * _sigmoid(x)


def _rms(x):
    return x * lax.rsqrt(jnp.mean(x * x, axis=-1, keepdims=True) + EPS)


def _split_dot(x, w):
    hi = x.astype(BF16)
    lo = (x - hi.astype(F32)).astype(BF16)
    return _dot(hi, w) + _dot(lo, w)


def _head_rms(x, seg):
    ss = _split_dot(x * x, seg)
    return x * lax.rsqrt(ss * (1.0 / HEAD_DIM) + EPS)


def _rope(x, cos, sin_signed):
    w = x.shape[-1]
    lane = lax.broadcasted_iota(jnp.int32, x.shape, 1)
    first_half = _mod(lane, HEAD_DIM) < (HEAD_DIM // 2)
    partner = jnp.where(first_half, pltpu.roll(x, w - HEAD_DIM // 2, 1), pltpu.roll(x, HEAD_DIM // 2, 1))
    return x * cos + partner * sin_signed


def _topk_mask(imp, k):
    nb = imp.shape[0]
    blk = lax.broadcasted_iota(jnp.int32, imp.shape, 0).astype(F32)
    sel = jnp.zeros(imp.shape, F32)
    for _ in range(k):
        mx = jnp.max(imp, axis=0, keepdims=True)
        first = jnp.min(jnp.where(imp == mx, blk, float(nb)), axis=0, keepdims=True)
        hit = blk == first
        sel = jnp.where(hit, 1.0, sel)
        imp = jnp.where(hit, -jnp.inf, imp)
    return sel


def _ada_kernel(c_ref, w_ref, b_ref, o_ref):
    o_ref[...] = _dot(_silu(c_ref[...]).astype(BF16), w_ref[...]) + b_ref[...]


def _ada_all_layers(c_all, w_ada, b_ada):
    depth, rows = w_ada.shape[0], c_all.shape[0]
    tn = 1536
    return pl.pallas_call(
        _ada_kernel,
        grid=(depth, 6 * D_MODEL // tn),
        in_specs=[pl.BlockSpec((rows, D_MODEL), lambda l, j: (0, 0)),
                  pl.BlockSpec((None, D_MODEL, tn), lambda l, j: (l, 0, j)),
                  pl.BlockSpec((None, 1, tn), lambda l, j: (l, 0, j))],
        out_specs=pl.BlockSpec((None, rows, tn), lambda l, j: (l, 0, j)),
        out_shape=jax.ShapeDtypeStruct((depth, rows, 6 * D_MODEL), F32),
        compiler_params=_params("parallel", "parallel"),
        name="ada",
    )(c_all, w_ada, b_ada)


def _mod_operand(mod, rows_per_batch, tm):
    if rows_per_batch % tm == 0:
        per = rows_per_batch // tm
        return mod[:, None, :], pl.BlockSpec((None, 1, D_MODEL), lambda i: (i // per, 0, 0))
    return jnp.repeat(mod, rows_per_batch, axis=0), pl.BlockSpec((tm, D_MODEL), lambda i: (i, 0))


def _const_spec(shape):
    nd = len(shape)
    return pl.BlockSpec(shape, lambda i: (0,) * nd)


def _inproj_kernel(x_ref, gn_ref, sc_ref, sh_ref, wu_ref, wq_ref, wkv_ref, wg_ref, wm_ref,
                   gq_ref, gks_ref, gkw_ref, cos_ref, sin_ref, seg_ref, pe_ref,
                   ug_ref, qn_ref, qr_ref, rows_ref, win_ref, g3_ref, gm_ref, *cmp_ref):
    h = _rms(x_ref[...]) * gn_ref[...]
    h = h * (1.0 + sc_ref[...]) + sh_ref[...]
    hb = h.astype(BF16)

    u = _dot(hb, wu_ref[...])
    ug_ref[...] = u[:, :D_CONV] * _sigmoid(u[:, D_CONV:])

    cos1, sin1 = cos_ref[...], sin_ref[...]
    cos4 = jnp.concatenate([cos1] * 4, axis=1)
    sin4 = jnp.concatenate([sin1] * 4, axis=1)
    qn = _head_rms(_dot(hb, wq_ref[...]), seg_ref[...]) * gq_ref[...]
    qn_ref[...] = (qn * ATTN_SCALE).astype(BF16)
    qr_ref[...] = (_rope(qn, cos4, sin4) * ATTN_SCALE).astype(BF16)

    kv = _dot(hb, wkv_ref[...])
    seg1 = seg_ref[:LANES, :LANES]
    ks = _rope(_head_rms(kv[:, 2 * LANES:3 * LANES], seg1) * gks_ref[...], cos1, sin1)
    kw = _rope(_head_rms(kv[:, 4 * LANES:5 * LANES], seg1) * gkw_ref[...], cos1, sin1)
    rows_ref[:, :2 * LANES] = kv[:, :2 * LANES]
    rows_ref[:, 2 * LANES:3 * LANES] = ks
    rows_ref[:, 3 * LANES:] = kv[:, 3 * LANES:4 * LANES]
    win_ref[:, :LANES] = kw
    win_ref[:, LANES:] = kv[:, 5 * LANES:]
    if cmp_ref:
        cmp_ref[0][...] = (kv[:, :2 * LANES] + pe_ref[...]).astype(BF16)

    g3_ref[...] = _sigmoid(_dot(hb, wg_ref[...]))
    gm_ref[...] = _sigmoid(_dot(hb, wm_ref[...])).astype(BF16)


def _inproj(x, mods, lw, tabs, rows_per_batch, tm, with_cmp):
    n = x.shape[0]
    sc, sc_spec = _mod_operand(mods["sc1"], rows_per_batch, tm)
    sh, sh_spec = _mod_operand(mods["sh1"], rows_per_batch, tm)
    cos, sin, tab_tiles = tabs["cos"], tabs["sin"], tabs["cos"].shape[0] // tm
    tab_spec = pl.BlockSpec((tm, LANES), lambda i: (i % tab_tiles, 0))
    row = lambda w: pl.BlockSpec((tm, w), lambda i: (i, 0))
    out_shape = [jax.ShapeDtypeStruct((n, D_CONV), F32), jax.ShapeDtypeStruct((n, 512), BF16),
                 jax.ShapeDtypeStruct((n, 512), BF16), jax.ShapeDtypeStruct((n, 512), F32),
                 jax.ShapeDtypeStruct((n, 256), F32), jax.ShapeDtypeStruct((n, LANES), F32),
                 jax.ShapeDtypeStruct((n, 2 * D_MODEL), BF16)]
    out_specs = [row(D_CONV), row(512), row(512), row(512), row(256), row(LANES), row(2 * D_MODEL)]
    if with_cmp:
        out_shape.append(jax.ShapeDtypeStruct((n, 256), BF16))
        out_specs.append(row(256))
    return pl.pallas_call(
        _inproj_kernel,
        grid=(n // tm,),
        in_specs=[row(D_MODEL), _const_spec((1, D_MODEL)), sc_spec, sh_spec,
                  _const_spec(lw["w_u"].shape), _const_spec(lw["w_q"].shape), _const_spec(lw["w_kv"].shape),
                  _const_spec(lw["w_g"].shape), _const_spec(lw["w_m"].shape),
                  _const_spec((1, 512)), _const_spec((1, LANES)), _const_spec((1, LANES)),
                  tab_spec, tab_spec, _const_spec((512, 512)), _const_spec((tm, 256))],
        out_specs=out_specs,
        out_shape=out_shape,
        compiler_params=_params("parallel"),
        name="inproj",
    )(x, lw["g_norm_mix"], sc, sh, lw["w_u"], lw["w_q"], lw["w_kv"], lw["w_g"], lw["w_m"],
      lw["g_q"], lw["g_ks"], lw["g_kw"], cos, sin, tabs["seg"], lw["pe_tile"][:tm])


def _conv_kernel(*refs, tc, n_tiles):
    if n_tiles > 1:
        cur_ref, halo_ref, pre_ref, w_ref, b_ref, g_ref, beta_ref, y_ref, win_ref = refs
    else:
        cur_ref, pre_ref, w_ref, b_ref, g_ref, beta_ref, y_ref, win_ref = refs
    if n_tiles > 1:
        first = pl.program_id(1) == 0

        @pl.when(first)
        def _():
            win_ref[:CONV_HALO, :] = pre_ref[...]

        @pl.when(jnp.logical_not(first))
        def _():
            win_ref[:CONV_HALO, :] = halo_ref[...]
    else:
        win_ref[:CONV_HALO, :] = pre_ref[...]
    win_ref[CONV_HALO:, :] = cur_ref[...]

    rc = min(tc, 64)
    off = CONV_HALO - (CONV_WIDTH - 1)
    for r0 in range(0, tc, rc):
        acc = jnp.zeros((rc, D_CONV), F32)
        for j in range(CONV_WIDTH):
            acc = acc + w_ref[j:j + 1, :] * win_ref[r0 + off + j:r0 + off + j + rc, :]
        y = acc + b_ref[...]
        mu = jnp.mean(y, axis=-1, keepdims=True)
        yc = y - mu
        var = jnp.mean(yc * yc, axis=-1, keepdims=True)
        y = yc * lax.rsqrt(var + EPS) * g_ref[...] + beta_ref[...]
        y_ref[r0:r0 + rc, :] = _silu(y).astype(BF16)


def _conv_branch(ug, prefix, lw, tc):
    b, t, _ = ug.shape
    n_tiles = t // tc
    cur_spec = pl.BlockSpec((None, tc, D_CONV), lambda bi, i: (bi, i, 0))
    pre_spec = pl.BlockSpec((None, CONV_HALO, D_CONV), lambda bi, i: (bi, 0, 0))
    vec = lambda r: pl.BlockSpec((r, D_CONV), lambda bi, i: (0, 0))
    ops, specs = [ug], [cur_spec]
    if n_tiles > 1:
        per = tc // CONV_HALO
        ops.append(ug)
        specs.append(pl.BlockSpec((None, CONV_HALO, D_CONV), lambda bi, i: (bi, jnp.maximum(i * per - 1, 0), 0)))
    ops += [prefix, lw["w_dw"], lw["b_dw"], lw["ln_conv_g"], lw["ln_conv_b"]]
    specs += [pre_spec, vec(CONV_HALO), vec(1), vec(1), vec(1)]
    return pl.pallas_call(
        functools.partial(_conv_kernel, tc=tc, n_tiles=n_tiles),
        grid=(b, n_tiles),
        in_specs=specs,
        out_specs=pl.BlockSpec((None, tc, D_CONV), lambda bi, i: (bi, i, 0)),
        out_shape=jax.ShapeDtypeStruct((b, t, D_CONV), BF16),
        scratch_shapes=[pltpu.VMEM((CONV_HALO + tc, D_CONV), F32)],
        compiler_params=_params("parallel", "parallel"),
        name="conv",
    )(*ops)


def _compress_kernel(x_ref, w1_ref, b1_ref, w2_ref, b2_ref, gkc_ref, o_ref):
    hid = jax.nn.gelu(_dot(x_ref[...], w1_ref[...]) + b1_ref[...])
    o = _dot(hid.astype(BF16), w2_ref[...]) + b2_ref[...]
    is_key = pl.program_id(0) == 0
    o_ref[...] = jnp.where(is_key, _rms(o) * gkc_ref[...], o)


def _compress(flat, lw):
    _, m, kdim = flat.shape
    tm = min(m, 512)
    hidden = lw["cmp_w1"].shape[-1]
    return pl.pallas_call(
        _compress_kernel,
        grid=(2, m // tm),
        in_specs=[pl.BlockSpec((None, tm, kdim), lambda s, i: (s, i, 0)),
                  pl.BlockSpec((None, kdim, hidden), lambda s, i: (s, 0, 0)),
                  pl.BlockSpec((None, 1, hidden), lambda s, i: (s, 0, 0)),
                  pl.BlockSpec((None, hidden, HEAD_DIM), lambda s, i: (s, 0, 0)),
                  pl.BlockSpec((None, 1, HEAD_DIM), lambda s, i: (s, 0, 0)),
                  pl.BlockSpec((1, HEAD_DIM), lambda s, i: (0, 0))],
        out_specs=pl.BlockSpec((None, tm, HEAD_DIM), lambda s, i: (s, i, 0)),
        out_shape=jax.ShapeDtypeStruct((2, m, HEAD_DIM), F32),
        compiler_params=_params("parallel", "parallel"),
        name="compress",
    )(flat, lw["cmp_w1"], lw["cmp_b1"], lw["cmp_w2"], lw["cmp_b2"], lw["g_kc"])


def _flatten_blocks(cmp_in, b, n_blk):
    x = cmp_in.reshape(b, n_blk, CMP_BLOCK, 2, N_KV, HEAD_DIM)
    x = x.transpose(3, 0, 4, 1, 2, 5)
    return x.reshape(2, b * N_KV * n_blk, CMP_BLOCK * HEAD_DIM)


def _softmax_step(s, ok, vt, state):
    m, l, acc = state
    s = jnp.where(ok, s, NEG)
    m_new = jnp.maximum(m, jnp.max(s, axis=0, keepdims=True))
    alpha = jnp.exp(m - m_new)
    p = jnp.exp(s - m_new)
    l = alpha * l + jnp.sum(p, axis=0, keepdims=True)
    acc = alpha * acc + _dot(vt, p.astype(BF16))
    return m_new, l, acc


def _attn_prompt_kernel(qn_ref, qr_ref, kc_ref, vct_ref, ks_ref, vst_ref, kw_ref, vwt_ref, g3_ref,
                        o_ref, sel_ref, *, tq):
    i = pl.program_id(2)
    t0 = i * tq
    lq = Q_PER_KV * tq
    qn = qn_ref[...]
    qr = qr_ref[...]
    lane = lax.broadcasted_iota(jnp.int32, (1, lq), 1)
    qpos = t0 + _mod(lane, tq)

    nb = kc_ref.shape[0]
    blk = lax.broadcasted_iota(jnp.int32, (nb, 1), 0)
    cm = blk < _div(qpos + 1, CMP_BLOCK)
    s = jnp.where(cm, _dot(kc_ref[...], qn), NEG)
    p = jnp.exp(s - jnp.max(s, axis=0, keepdims=True)) * cm.astype(F32)
    p = p / jnp.maximum(jnp.sum(p, axis=0, keepdims=True), 1e-30)
    o_cmp = _dot(vct_ref[...], p.astype(BF16))

    imp = p[:, :tq]
    for r in range(1, Q_PER_KV):
        imp = imp + p[:, r * tq:(r + 1) * tq]
    cur = _div(qpos[:, :tq], CMP_BLOCK)
    imp = jnp.where((blk == cur) | (blk == 0), FORCED_SCORE, jnp.where(blk > cur, -1.0, imp))
    sel_ref[...] = _topk_mask(imp, N_SELECT)

    kidx = lax.broadcasted_iota(jnp.int32, (KEY_CHUNK, 1), 0)
    row8 = lax.broadcasted_iota(jnp.int32, (8, 1), 0)
    blocks_per_chunk = KEY_CHUNK // CMP_BLOCK
    zero = (jnp.full((1, lq), NEG, F32), jnp.zeros((1, lq), F32), jnp.zeros((HEAD_DIM, lq), F32))

    def sel_body(c, state):
        k0 = pl.multiple_of(c * KEY_CHUNK, KEY_CHUNK)
        rows8 = sel_ref[pl.ds(pl.multiple_of((c * blocks_per_chunk // 8) * 8, 8), 8), :]
        r0 = (c * blocks_per_chunk) % 8
        pieces = []
        for j in range(blocks_per_chunk):
            rowj = jnp.sum(jnp.where(row8 == r0 + j, rows8, 0.0), axis=0, keepdims=True)
            pieces.append(jnp.broadcast_to(rowj, (CMP_BLOCK, tq)))
        selk = jnp.concatenate(pieces, axis=0)
        selk = jnp.concatenate([selk] * Q_PER_KV, axis=1)
        ok = (selk > 0.5) & (k0 + kidx <= qpos)
        s = _dot(ks_ref[pl.ds(k0, KEY_CHUNK), :], qr)
        return _softmax_step(s, ok, vst_ref[:, pl.ds(k0, KEY_CHUNK)], state)

    n_sel_chunks = (t0 + tq) // KEY_CHUNK
    _, l_s, acc_s = lax.fori_loop(0, n_sel_chunks, sel_body, zero)

    def win_body(c, state):
        k0 = pl.multiple_of(t0 - WINDOW + c * KEY_CHUNK, KEY_CHUNK)
        dist = qpos - (k0 + kidx)
        ok = (dist >= 0) & (dist < WINDOW)
        s = _dot(kw_ref[pl.ds(k0, KEY_CHUNK), :], qr)
        return _softmax_step(s, ok, vwt_ref[:, pl.ds(k0, KEY_CHUNK)], state)

    n_win_chunks = (WINDOW + tq) // KEY_CHUNK
    first_chunk = jnp.maximum(0, (WINDOW - t0) // KEY_CHUNK)
    _, l_w, acc_w = lax.fori_loop(first_chunk, n_win_chunks, win_body, zero)

    g3 = g3_ref[...]
    o = g3[0:1] * o_cmp + g3[1:2] * (acc_s / l_s) + g3[2:3] * (acc_w / l_w)
    o_ref[...] = o.astype(BF16)


def _attn_prompt(qn_t, qr_t, kc, vct, ks, vst, kw, vwt, g3_t, tq):
    b, g, nq = qn_t.shape[:3]
    t = ks.shape[2]
    lq = Q_PER_KV * tq
    n_cmp = kc.shape[2]
    qspec = pl.BlockSpec((None, None, None, HEAD_DIM, lq), lambda bi, gi, i: (bi, gi, i, 0, 0))
    per_bg = lambda s0, s1: pl.BlockSpec((None, None, s0, s1), lambda bi, gi, i: (bi, gi, 0, 0))
    return pl.pallas_call(
        functools.partial(_attn_prompt_kernel, tq=tq),
        grid=(b, g, nq),
        in_specs=[qspec, qspec, per_bg(n_cmp, HEAD_DIM), per_bg(HEAD_DIM, n_cmp),
                  per_bg(t, HEAD_DIM), per_bg(HEAD_DIM, t), per_bg(t, HEAD_DIM), per_bg(HEAD_DIM, t),
                  pl.BlockSpec((None, None, None, 3, lq), lambda bi, gi, i: (bi, gi, i, 0, 0))],
        out_specs=qspec,
        out_shape=jax.ShapeDtypeStruct((b, g, nq, HEAD_DIM, lq), BF16),
        scratch_shapes=[pltpu.VMEM((n_cmp, tq), F32)],
        compiler_params=_params("parallel", "parallel", "arbitrary"),
        name="attn_prompt",
    )(qn_t, qr_t, kc, vct, ks, vst, kw, vwt, g3_t)


def _to_group_major(q, b, t, tq):
    x = q.reshape(b, t // tq, tq, N_KV, Q_PER_KV, HEAD_DIM).transpose(0, 3, 1, 5, 4, 2)
    return x.reshape(b, N_KV, t // tq, HEAD_DIM, Q_PER_KV * tq)


def _from_group_major(o, b, t, tq):
    x = o.reshape(b, N_KV, t // tq, HEAD_DIM, Q_PER_KV, tq).transpose(0, 2, 5, 1, 4, 3)
    return x.reshape(b * t, N_HEADS * HEAD_DIM)


def _gates_group_major(g3, b, t, tq):
    x = g3[:, :3 * N_HEADS].reshape(b, t // tq, tq, 3, N_KV, Q_PER_KV).transpose(0, 4, 1, 3, 5, 2)
    return x.reshape(b, N_KV, t // tq, 3, Q_PER_KV * tq)


def _attn_sample_kernel(pt_ref, *refs, n_pages, n_blk, pos0, tq):
    pages = refs[:PAGES_PER_STEP]
    (qn_ref, qr_ref, kc_ref, vct_ref, rown_ref, kwa_ref, vwa_ref, g3_ref, rsum_ref,
     o_ref, sel_ref, m_ref, l_ref, acc_ref, ocmp_ref) = refs[PAGES_PER_STEP:]
    del pt_ref
    j = pl.program_id(1)
    lq = Q_PER_KV * tq
    lane = lax.broadcasted_iota(jnp.int32, (1, lq), 1)
    qpos = pos0 + _mod(lane, tq)
    nbp = sel_ref.shape[1]
    n_cmp = kc_ref.shape[1]

    @pl.when(j == 0)
    def _():
        blk = lax.broadcasted_iota(jnp.int32, (nbp, 1), 0)
        for g in range(N_KV):
            cm = blk[:n_cmp] < _div(qpos + 1, CMP_BLOCK)
            s = jnp.where(cm, _dot(kc_ref[g], qn_ref[g, g * HEAD_DIM:(g + 1) * HEAD_DIM, :]), NEG)
            p = jnp.exp(s - jnp.max(s, axis=0, keepdims=True)) * cm.astype(F32)
            p = p / jnp.maximum(jnp.sum(p, axis=0, keepdims=True), 1e-30)
            ocmp_ref[g] = _dot(vct_ref[g], p.astype(BF16))
            imp = _split_dot(p, rsum_ref[...])
            imp = jnp.concatenate([imp, jnp.zeros((nbp - n_cmp, lq), F32)], axis=0)
            cur = _div(qpos, CMP_BLOCK)
            imp = jnp.where((blk == cur) | (blk == 0), FORCED_SCORE, jnp.where(blk > cur, -1.0, imp))
            imp = jnp.where(blk >= n_blk, -jnp.inf, imp)
            sel_ref[g] = _topk_mask(imp, N_SELECT)
        m_ref[...] = jnp.full(m_ref.shape, NEG, F32)
        l_ref[...] = jnp.zeros(l_ref.shape, F32)
        acc_ref[...] = jnp.zeros(acc_ref.shape, F32)

    blocks_per_page = PAGE_SIZE // CMP_BLOCK
    blocks_per_step = PAGES_PER_STEP * blocks_per_page
    kb = [pg[:, :LANES].astype(BF16) for pg in pages]
    vb = [pg[:, LANES:].astype(BF16) for pg in pages]
    for g in range(N_KV):
        selrows = sel_ref[g, pl.ds(pl.multiple_of(j * blocks_per_step, blocks_per_step), blocks_per_step), :]
        qr = qr_ref[g]
        ss = []
        for k in range(PAGES_PER_STEP):
            okk = jnp.concatenate(
                [jnp.broadcast_to(selrows[k * blocks_per_page + u:k * blocks_per_page + u + 1, :], (CMP_BLOCK, lq))
                 for u in range(blocks_per_page)], axis=0) > 0.5
            ss.append(jnp.where(okk, _dot(kb[k], qr), NEG))
        m = m_ref[g]
        m_new = m
        for s in ss:
            m_new = jnp.maximum(m_new, jnp.max(s, axis=0, keepdims=True))
        alpha = jnp.exp(m - m_new)
        l = alpha * l_ref[g]
        acc = alpha * acc_ref[g]
        for k, s in enumerate(ss):
            p = jnp.exp(s - m_new)
            l = l + jnp.sum(p, axis=0, keepdims=True)
            acc = acc + _dot_t0(vb[k], p.astype(BF16))[g * HEAD_DIM:(g + 1) * HEAD_DIM]
        m_ref[g] = m_new
        l_ref[g] = l
        acc_ref[g] = acc

    @pl.when(j == n_pages // PAGES_PER_STEP - 1)
    def _():
        tn = rown_ref.shape[0]
        kn = rown_ref[:, :LANES].astype(BF16)
        vn = rown_ref[:, LANES:].astype(BF16)
        kidx = lax.broadcasted_iota(jnp.int32, (tn, 1), 0)
        widx = lax.broadcasted_iota(jnp.int32, (kwa_ref.shape[0], 1), 0)
        cur_blk = pos0 // CMP_BLOCK
        for g in range(N_KV):
            qr = qr_ref[g]
            cur_sel = sel_ref[g, cur_blk:cur_blk + 1, :] > 0.5
            ok = cur_sel & (pos0 + kidx <= qpos)
            s = jnp.where(ok, _dot(kn, qr), NEG)
            m = m_ref[g]
            m_new = jnp.maximum(m, jnp.max(s, axis=0, keepdims=True))
            alpha = jnp.exp(m - m_new)
            p = jnp.exp(s - m_new)
            l = alpha * l_ref[g] + jnp.sum(p, axis=0, keepdims=True)
            acc = alpha * acc_ref[g] + _dot_t0(vn, p.astype(BF16))[g * HEAD_DIM:(g + 1) * HEAD_DIM]
            o_sel = acc / l

            dist = qpos - (pos0 - WINDOW + widx)
            ok = (dist >= 0) & (dist < WINDOW) & (pos0 - WINDOW + widx >= 0)
            s = jnp.where(ok, _dot(kwa_ref[...], qr), NEG)
            p = jnp.exp(s - jnp.max(s, axis=0, keepdims=True)) * ok.astype(F32)
            p = p / jnp.maximum(jnp.sum(p, axis=0, keepdims=True), 1e-30)
            o_win = _dot_t0(vwa_ref[...], p.astype(BF16))[g * HEAD_DIM:(g + 1) * HEAD_DIM]

            g3 = g3_ref[g]
            o_ref[g] = g3[0:1] * ocmp_ref[g] + g3[1:2] * o_sel + g3[2:3] * o_win


def _attn_sample(page_table, cache_l, qn_p, qr_p, kc, vct, rows_new, kw_all, vw_all, g3_t, rsum, pos0, tq):
    db, n_pages = page_table.shape
    lq = Q_PER_KV * tq
    n_cmp = kc.shape[2]
    n_blk = -(-(pos0 + tq) // CMP_BLOCK)
    nbp = -(-n_blk // 8) * 8
    steps = n_pages // PAGES_PER_STEP

    def page_spec(k):
        return pl.BlockSpec((None, PAGE_SIZE, 2 * LANES), lambda b, j, pt: (pt[b, j * PAGES_PER_STEP + k], 0, 1))

    per_b = lambda *s: pl.BlockSpec((None,) + s, lambda b, j, pt: (b,) + (0,) * len(s))
    grid_spec = pltpu.PrefetchScalarGridSpec(
        num_scalar_prefetch=1,
        grid=(db, steps),
        in_specs=[page_spec(k) for k in range(PAGES_PER_STEP)] + [
            per_b(N_KV, LANES, lq), per_b(N_KV, LANES, lq), per_b(N_KV, n_cmp, HEAD_DIM), per_b(N_KV, HEAD_DIM, n_cmp),
            per_b(tq, 2 * LANES), per_b(WINDOW + tq, LANES), per_b(WINDOW + tq, LANES), per_b(N_KV, 3, lq),
            pl.BlockSpec((lq, lq), lambda b, j, pt: (0, 0))],
        out_specs=per_b(N_KV, HEAD_DIM, lq),
        scratch_shapes=[pltpu.VMEM((N_KV, nbp, lq), F32), pltpu.VMEM((N_KV, 1, lq), F32),
                        pltpu.VMEM((N_KV, 1, lq), F32), pltpu.VMEM((N_KV, HEAD_DIM, lq), F32),
                        pltpu.VMEM((N_KV, HEAD_DIM, lq), F32)])
    return pl.pallas_call(
        functools.partial(_attn_sample_kernel, n_pages=n_pages, n_blk=n_blk, pos0=pos0, tq=tq),
        grid_spec=grid_spec,
        out_shape=jax.ShapeDtypeStruct((db, N_KV, HEAD_DIM, lq), F32),
        compiler_params=_params("parallel", "arbitrary"),
        name="attn_sample",
    )(page_table, *([cache_l] * PAGES_PER_STEP), qn_p, qr_p, kc, vct, rows_new, kw_all, vw_all, g3_t, rsum)


def _gather_cmp_kernel(pt_ref, *refs):
    del pt_ref
    pages, pe_ref, o_ref = refs[:PAGES_PER_STEP], refs[PAGES_PER_STEP], refs[PAGES_PER_STEP + 1]
    for k, pg in enumerate(pages):
        o_ref[k * PAGE_SIZE:(k + 1) * PAGE_SIZE, :] = (pg[...] + pe_ref[...]).astype(BF16)


def _gather_cmp(page_table, cache, pe_tiles):
    depth = cache.shape[0]
    db, n_pages = page_table.shape
    steps = n_pages // PAGES_PER_STEP

    def page_spec(k):
        return pl.BlockSpec((None, None, PAGE_SIZE, 2 * LANES),
                            lambda l, b, j, pt: (l, pt[b, j * PAGES_PER_STEP + k], 0, 0))

    grid_spec = pltpu.PrefetchScalarGridSpec(
        num_scalar_prefetch=1,
        grid=(depth, db, steps),
        in_specs=[page_spec(k) for k in range(PAGES_PER_STEP)] + [
            pl.BlockSpec((None, PAGE_SIZE, 2 * LANES), lambda l, b, j, pt: (l, 0, 0))],
        out_specs=pl.BlockSpec((None, None, PAGES_PER_STEP * PAGE_SIZE, 2 * LANES), lambda l, b, j, pt: (l, b, j, 0)))
    return pl.pallas_call(
        _gather_cmp_kernel,
        grid_spec=grid_spec,
        out_shape=jax.ShapeDtypeStruct((depth, db, n_pages * PAGE_SIZE, 2 * LANES), BF16),
        compiler_params=_params("parallel", "parallel", "parallel"),
        name="gather_cmp",
    )(page_table, *([cache] * PAGES_PER_STEP), pe_tiles)


def _merge_kernel(x_ref, y_ref, o_ref, gm_ref, gt_ref, sc_ref, sh_ref, gn_ref, wc_ref, wa_ref, wo_ref,
                  wr_ref, br_ref, x1_ref, h2_ref, dw_ref):
    gm = gm_ref[...]
    merged = (gm[:, :D_MODEL].astype(F32) * _dot(y_ref[...], wc_ref[...])
              + gm[:, D_MODEL:].astype(F32) * _dot(o_ref[...], wa_ref[...]))
    x1 = x_ref[...] + gt_ref[...] * _dot(merged.astype(BF16), wo_ref[...])
    x1_ref[...] = x1
    h2 = _rms(x1) * gn_ref[...]
    h2 = h2 * (1.0 + sc_ref[...]) + sh_ref[...]
    h2_ref[...] = h2.astype(BF16)

    w_hi = wr_ref[0]
    w_lo = wr_ref[1]
    h_hi = h2.astype(BF16)
    h_lo = (h2 - h_hi.astype(F32)).astype(BF16)
    aff = _sigmoid(_dot(h_hi, w_hi) + _dot(h_lo, w_hi) + _dot(h_hi, w_lo))
    score = aff + br_ref[...]
    e = lax.broadcasted_iota(jnp.int32, score.shape, 1)
    grp = _div(e, EXPERTS_PER_GROUP)
    e = e.astype(F32)
    big = float(LANES)

    def top2(j):
        vals = jnp.where(grp == j, score, -jnp.inf)
        m1 = jnp.max(vals, axis=-1, keepdims=True)
        i1 = jnp.min(jnp.where(vals == m1, e, big), axis=-1, keepdims=True)
        vals = jnp.where(e == i1, -jnp.inf, vals)
        m2 = jnp.max(vals, axis=-1, keepdims=True)
        i2 = jnp.min(jnp.where(vals == m2, e, big), axis=-1, keepdims=True)
        return m1 + m2, i1, i2

    best, ia, ib = top2(0)
    for j in range(1, N_GROUPS):
        gs, i1, i2 = top2(j)
        better = gs > best
        best = jnp.where(better, gs, best)
        ia = jnp.where(better, i1, ia)
        ib = jnp.where(better, i2, ib)
    hit_a = e == ia
    hit_b = e == ib
    aff_a = jnp.sum(jnp.where(hit_a, aff, 0.0), axis=-1, keepdims=True)
    aff_b = jnp.sum(jnp.where(hit_b, aff, 0.0), axis=-1, keepdims=True)
    tot = aff_a + aff_b
    dw_ref[...] = jnp.where(hit_a, aff_a / tot, 0.0) + jnp.where(hit_b, aff_b / tot, 0.0)


def _merge(x, y, o, gm, mods, lw, shared, rows_per_batch, tm):
    n = x.shape[0]
    gt, gt_spec = _mod_operand(mods["gt1"], rows_per_batch, tm)
    sc, sc_spec = _mod_operand(mods["sc2"], rows_per_batch, tm)
    sh, sh_spec = _mod_operand(mods["sh2"], rows_per_batch, tm)
    row = lambda w: pl.BlockSpec((tm, w), lambda i: (i, 0))
    return pl.pallas_call(
        _merge_kernel,
        grid=(n // tm,),
        in_specs=[row(D_MODEL), row(D_CONV), row(512), row(2 * D_MODEL), gt_spec, sc_spec, sh_spec,
                  _const_spec((1, D_MODEL)), _const_spec((D_CONV, D_MODEL)), _const_spec((512, D_MODEL)),
                  _const_spec((D_MODEL, D_MODEL)), _const_spec((2, D_MODEL, LANES)), _const_spec((1, LANES))],
        out_specs=[row(D_MODEL), row(D_MODEL), row(LANES)],
        out_shape=[jax.ShapeDtypeStruct((n, D_MODEL), F32), jax.ShapeDtypeStruct((n, D_MODEL), BF16),
                   jax.ShapeDtypeStruct((n, LANES), F32)],
        compiler_params=_params("parallel"),
        name="merge",
    )(x, y, o, gm, gt, sc, sh, lw["g_norm_ffn"], lw["w_conv_out"], lw["w_attn_out"], lw["w_out"],
      shared["w_router"], shared["b_router"])


def _moe_kernel(h_ref, dw_ref, x_ref, gt_ref, wg_ref, wu_ref, wd_ref, o_ref, acc_ref):
    ex = pl.program_id(1)

    @pl.when(ex == 0)
    def _():
        acc_ref[...] = jnp.zeros(acc_ref.shape, F32)

    h = h_ref[...]
    lane = lax.broadcasted_iota(jnp.int32, dw_ref.shape, 1)
    w_col = jnp.sum(jnp.where(lane == ex, dw_ref[...], 0.0), axis=-1, keepdims=True)
    hid = _silu(_dot(h, wg_ref[...])) * _dot(h, wu_ref[...]) * w_col
    acc_ref[...] += _dot(hid.astype(BF16), wd_ref[...])

    @pl.when(ex == N_EXPERTS - 1)
    def _():
        o_ref[...] = x_ref[...] + gt_ref[...] * acc_ref[...]


def _moe(h2, dw, x1, mods, lw, rows_per_batch, tm):
    n = h2.shape[0]
    gt, gt_spec = _mod_operand(mods["gt2"], rows_per_batch, tm)
    gt_spec = pl.BlockSpec(gt_spec.block_shape, lambda i, ex, f=gt_spec.index_map: f(i))
    row = lambda w: pl.BlockSpec((tm, w), lambda i, ex: (i, 0))
    return pl.pallas_call(
        _moe_kernel,
        grid=(n // tm, N_EXPERTS),
        in_specs=[row(D_MODEL), row(LANES), row(D_MODEL), gt_spec,
                  pl.BlockSpec((None, D_MODEL, D_EXPERT), lambda i, ex: (ex, 0, 0)),
                  pl.BlockSpec((None, D_MODEL, D_EXPERT), lambda i, ex: (ex, 0, 0)),
                  pl.BlockSpec((None, D_EXPERT, D_MODEL), lambda i, ex: (ex, 0, 0))],
        out_specs=row(D_MODEL),
        out_shape=jax.ShapeDtypeStruct((n, D_MODEL), F32),
        scratch_shapes=[pltpu.VMEM((tm, D_MODEL), F32)],
        compiler_params=_params("parallel", "arbitrary"),
        name="moe",
    )(h2, dw, x1, gt, lw["w_exp_gate"], lw["w_exp_up"], lw["w_exp_down"])


def _rope_tables(pos):
    half = HEAD_DIM // 2
    inv = ROPE_THETA ** (-jnp.arange(half, dtype=F32) / half)
    ang = pos.astype(F32)[:, None] * inv[None, :]
    cos, sin = jnp.cos(ang), jnp.sin(ang)
    cos = jnp.concatenate([cos, cos], axis=1)
    sin = jnp.concatenate([-sin, sin], axis=1)
    return jnp.concatenate([cos, cos], axis=1), jnp.concatenate([sin, sin], axis=1)


def _split_mod(mod):
    names = ("sh1", "sc1", "gt1", "sh2", "sc2", "gt2")
    return dict(zip(names, jnp.split(mod, 6, axis=-1)))


def kernel(x_prompt, x_sample, c_prompt, c_sample, cache_kv, state_win_kv, state_conv, page_table, w_ada, b_ada, g_norm_mix, g_norm_ffn, w_in, w_dw, b_dw, ln_conv_g, ln_conv_b, w_conv_out, g_q, g_kc, g_ks, g_kw, cmp_pe, cmp_w1, cmp_b1, cmp_w2, cmp_b2, w_attn_out, w_out, w_router, b_router, w_exp_gate, w_exp_up, w_exp_down):
    depth = w_in.shape[0]
    b, t, _ = x_prompt.shape
    db, dt, _ = x_sample.shape
    n_pages = page_table.shape[1]
    past_len = n_pages * PAGE_SIZE
    win_buf = state_win_kv.shape[2]
    n_phys = cache_kv.shape[1]
    tq = 128
    tm_p, tm_s = 512, db * dt
    assert t % tm_p == 0 and t % tq == 0 and win_buf == WINDOW and past_len % CMP_BLOCK == 0

    c0, c1, c2, c3 = 2 * D_CONV, 2 * D_CONV + 512, 2 * D_CONV + 512 + 768, 2 * D_CONV + 512 + 768 + 3 * N_HEADS
    w_in_b = w_in.astype(BF16)
    w_dw_p = jnp.pad(w_dw, ((0, 0), (0, CONV_HALO - CONV_WIDTH), (0, 0)))
    pe256 = cmp_pe[:, :, :, None, :].repeat(N_KV, axis=3).transpose(0, 2, 1, 3, 4).reshape(depth, CMP_BLOCK, 256)
    pe_tile = jnp.tile(pe256, (1, tm_p // CMP_BLOCK, 1))
    layers = []
    for l in range(depth):
        layers.append(dict(
            g_norm_mix=g_norm_mix[l][None], g_norm_ffn=g_norm_ffn[l][None],
            w_u=w_in_b[l][:, :c0], w_q=w_in_b[l][:, c0:c1], w_kv=w_in_b[l][:, c1:c2],
            w_g=jnp.pad(w_in_b[l][:, c2:c3], ((0, 0), (0, LANES - 3 * N_HEADS))), w_m=w_in_b[l][:, c3:],
            g_q=jnp.tile(g_q[l], N_HEADS)[None], g_ks=jnp.tile(g_ks[l], N_KV)[None], g_kw=jnp.tile(g_kw[l], N_KV)[None],
            g_kc=g_kc[l][None], pe_tile=pe_tile[l],
            w_dw=w_dw_p[l], b_dw=b_dw[l][None], ln_conv_g=ln_conv_g[l][None], ln_conv_b=ln_conv_b[l][None],
            w_conv_out=w_conv_out[l].astype(BF16), w_attn_out=w_attn_out[l].astype(BF16), w_out=w_out[l].astype(BF16),
            cmp_w1=cmp_w1[l].astype(BF16), cmp_b1=cmp_b1[l][:, None, :], cmp_w2=cmp_w2[l].astype(BF16),
            cmp_b2=cmp_b2[l][:, None, :],
            w_exp_gate=w_exp_gate[l].astype(BF16), w_exp_up=w_exp_up[l].astype(BF16),
            w_exp_down=w_exp_down[l].astype(BF16)))
    wr = jnp.pad(w_router, ((0, 0), (0, LANES - N_EXPERTS)))
    wr_hi = wr.astype(BF16)
    wr_lo = (wr - wr_hi.astype(F32)).astype(BF16)
    shared = dict(w_router=jnp.stack([wr_hi, wr_lo]),
                  b_router=jnp.pad(b_router, (0, LANES - N_EXPERTS))[None])
    seg = (jnp.arange(512)[:, None] // HEAD_DIM == jnp.arange(512)[None, :] // HEAD_DIM).astype(BF16)
    cos_p, sin_p = _rope_tables(jnp.arange(t, dtype=jnp.int32))
    cos_s, sin_s = _rope_tables(past_len + jnp.arange(dt, dtype=jnp.int32))
    tabs_p = dict(cos=cos_p, sin=sin_p, seg=seg)
    tabs_s = dict(cos=jnp.tile(cos_s, (db, 1)), sin=jnp.tile(sin_s, (db, 1)), seg=seg)
    lq_s = Q_PER_KV * dt
    rsum = (jnp.arange(lq_s)[:, None] % dt == jnp.arange(lq_s)[None, :] % dt).astype(BF16)

    rows = b + db
    rows_p = -(-rows // 8) * 8
    c_all = jnp.pad(jnp.concatenate([c_prompt, c_sample], axis=0), ((0, rows_p - rows), (0, 0)))
    mod_all = _ada_all_layers(c_all, w_ada.astype(BF16), b_ada[:, None, :])

    cache_rows = cache_kv.reshape(depth, n_phys, PAGE_SIZE, 4 * N_KV * HEAD_DIM)
    pe_page = jnp.tile(pe256, (1, PAGE_SIZE // CMP_BLOCK, 1))
    past_cmp = _gather_cmp(page_table, cache_rows, pe_page)
    n_cmp_s = (past_len + dt) // CMP_BLOCK
    assert n_cmp_s * CMP_BLOCK <= past_len

    xp = x_prompt.reshape(b * t, D_MODEL)
    xs = x_sample.reshape(db * dt, D_MODEL)
    outs = {k: [] for k in ("kv_p", "kv_s", "win_p", "win_s", "conv_p", "conv_s")}
    zero_prefix = jnp.zeros((b, CONV_HALO, D_CONV), F32)

    for l in range(depth):
        lw = layers[l]
        mods_p = _split_mod(mod_all[l, :b])
        mods_s = _split_mod(mod_all[l, b:b + db])

        ug, qn, qr, rows_new, win_new, g3, gm, cmp_in = _inproj(xp, mods_p, lw, tabs_p, t, tm_p, True)
        y = _conv_branch(ug.reshape(b, t, D_CONV), zero_prefix, lw, 512)
        kvc = _compress(_flatten_blocks(cmp_in, b, t // CMP_BLOCK), lw)
        n_blk = t // CMP_BLOCK
        kc = kvc[0].reshape(b, N_KV, n_blk, HEAD_DIM).astype(BF16)
        vct = kvc[1].reshape(b, N_KV, n_blk, HEAD_DIM).transpose(0, 1, 3, 2).astype(BF16)
        r5 = rows_new.reshape(b, t, 4, N_KV, HEAD_DIM)
        w5 = win_new.reshape(b, t, 2, N_KV, HEAD_DIM)
        ks = r5[:, :, 2].transpose(0, 2, 1, 3).astype(BF16)
        vst = r5[:, :, 3].transpose(0, 2, 3, 1).astype(BF16)
        kw = w5[:, :, 0].transpose(0, 2, 1, 3).astype(BF16)
        vwt = w5[:, :, 1].transpose(0, 2, 3, 1).astype(BF16)
        o_t = _attn_prompt(_to_group_major(qn, b, t, tq), _to_group_major(qr, b, t, tq), kc, vct, ks, vst, kw, vwt,
                           _gates_group_major(g3, b, t, tq), tq)
        o = _from_group_major(o_t, b, t, tq)
        x1, h2, dw = _merge(xp, y.reshape(b * t, D_CONV), o, gm, mods_p, lw, shared, t, tm_p)
        xp = _moe(h2, dw, x1, mods_p, lw, t, tm_p)
        outs["kv_p"].append(r5)
        outs["win_p"].append(w5[:, t - win_buf:])
        outs["conv_p"].append(ug.reshape(b, t, D_CONV)[:, t - (CONV_WIDTH - 1):])

        ug, qn, qr, rows_new, win_new, g3, gm = _inproj(xs, mods_s, lw, tabs_s, dt, tm_s, False)
        ug3 = ug.reshape(db, dt, D_CONV)
        conv_ext = jnp.concatenate([state_conv[l], ug3], axis=1)
        prefix = jnp.pad(state_conv[l], ((0, 0), (CONV_HALO - (CONV_WIDTH - 1), 0), (0, 0)))
        y = _conv_branch(ug3, prefix, lw, dt)
        kvc = _compress(_flatten_blocks(past_cmp[l][:, :n_cmp_s * CMP_BLOCK].reshape(-1, 256), db, n_cmp_s), lw)
        kc = kvc[0].reshape(db, N_KV, n_cmp_s, HEAD_DIM).astype(BF16)
        vct = kvc[1].reshape(db, N_KV, n_cmp_s, HEAD_DIM).transpose(0, 1, 3, 2).astype(BF16)
        w5 = win_new.reshape(db, dt, 2, N_KV, HEAD_DIM)
        win_all = jnp.concatenate([state_win_kv[l], w5], axis=1)
        kw_all = win_all[:, :, 0].reshape(db, WINDOW + dt, LANES).astype(BF16)
        vw_all = win_all[:, :, 1].reshape(db, WINDOW + dt, LANES).astype(BF16)
        qpad = []
        for q in (qn, qr):
            qt = q.reshape(db, dt, N_KV, Q_PER_KV, HEAD_DIM).transpose(0, 2, 4, 3, 1).reshape(db, N_KV, HEAD_DIM, lq_s)
            eye = jnp.eye(N_KV, dtype=BF16)
            qpad.append((qt[:, :, None] * eye[None, :, :, None, None]).reshape(db, N_KV, LANES, lq_s))
        g3_t = g3[:, :3 * N_HEADS].reshape(db, dt, 3, N_KV, Q_PER_KV).transpose(0, 3, 2, 4, 1).reshape(db, N_KV, 3, lq_s)
        o_t = _attn_sample(page_table, cache_rows[l], qpad[0], qpad[1], kc, vct,
                           rows_new.reshape(db, dt, 512)[:, :, 2 * LANES:], kw_all, vw_all, g3_t, rsum, past_len, dt)
        o = o_t.reshape(db, N_KV, HEAD_DIM, Q_PER_KV, dt).transpose(0, 4, 1, 3, 2).reshape(db * dt, 512).astype(BF16)
        x1, h2, dw = _merge(xs, y.reshape(db * dt, D_CONV), o, gm, mods_s, lw, shared, dt, tm_s)
        xs = _moe(h2, dw, x1, mods_s, lw, dt, tm_s)
        outs["kv_s"].append(rows_new.reshape(db, dt, 4, N_KV, HEAD_DIM))
        outs["win_s"].append(win_all[:, dt:])
        outs["conv_s"].append(conv_ext[:, dt:])

    return (xp.reshape(b, t, D_MODEL), xs.reshape(db, dt, D_MODEL),
            jnp.stack(outs["kv_p"]), jnp.stack(outs["kv_s"]), jnp.stack(outs["win_p"]), jnp.stack(outs["win_s"]),
            jnp.stack(outs["conv_p"]), jnp.stack(outs["conv_s"]))
```

```python
import functools

import jax
import jax.numpy as jnp
from jax import lax
from jax.experimental import pallas as pl
from jax.experimental.pallas import tpu as pltpu

F32 = jnp.float32
BF16 = jnp.bfloat16

D_MODEL = 1024
D_CONV = 512
CONV_WIDTH = 31
CONV_HALO = 32
N_HEADS = 8
HEAD_DIM = 64
N_KV = 2
Q_PER_KV = 4
CMP_BLOCK = 64
N_SELECT = 16
WINDOW = 512
N_EXPERTS = 16
N_GROUPS = 4
EXPERTS_PER_GROUP = 4
D_EXPERT = 256
PAGE_SIZE = 128
ROPE_THETA = 10000.0
ATTN_SCALE = HEAD_DIM ** -0.5
FORCED_SCORE = 1e4
EPS = 1e-6
NEG = -1e30
LANES = 128
KEY_CHUNK = 256
GATHER_PAGES = 8
ATTN_PAGES = 16
VMEM_LIMIT = 56 * 1024 * 1024


def _params(*sem):
    return pltpu.CompilerParams(dimension_semantics=sem, vmem_limit_bytes=VMEM_LIMIT)


def _dot(a, b):
    return jnp.dot(a, b, preferred_element_type=F32)


def _dot_t0(a, b):
    return lax.dot_general(a, b, (((0,), (0,)), ((), ())), preferred_element_type=F32)


def _dot_t1(a, b):
    return lax.dot_general(a, b, (((1,), (1,)), ((), ())), preferred_element_type=F32)


def _shift(d):
    assert d & (d - 1) == 0
    return d.bit_length() - 1


def _div(x, d):
    return lax.shift_right_logical(x, jnp.int32(_shift(d)))


def _mod(x, d):
    return x & jnp.int32(d - 1)


def _sigmoid(x):
    return 1.0 / (1.0 + jnp.exp(-x))


def _silu(x):
    return x * _sigmoid(x)


def _rms(x):
    return x * lax.rsqrt(jnp.mean(x * x, axis=-1, keepdims=True) + EPS)


def _split_dot(x, w):
    hi = x.astype(BF16)
    lo = (x - hi.astype(F32)).astype(BF16)
    return _dot(hi, w) + _dot(lo, w)


def _head_rms(x, seg):
    ss = _split_dot(x * x, seg)
    return x * lax.rsqrt(ss * (1.0 / HEAD_DIM) + EPS)


def _rope(x, cos, sin_signed):
    w = x.shape[-1]
    lane = lax.broadcasted_iota(jnp.int32, x.shape, 1)
    first_half = _mod(lane, HEAD_DIM) < (HEAD_DIM // 2)
    partner = jnp.where(first_half, pltpu.roll(x, w - HEAD_DIM // 2, 1), pltpu.roll(x, HEAD_DIM // 2, 1))
    return x * cos + partner * sin_signed


def _topk_mask(imp, k):
    nb = imp.shape[0]
    blk = lax.broadcasted_iota(jnp.int32, imp.shape, 0).astype(F32)
    sel = jnp.zeros(imp.shape, F32)
    for _ in range(k):
        mx = jnp.max(imp, axis=0, keepdims=True)
        first = jnp.min(jnp.where(imp == mx, blk, float(nb)), axis=0, keepdims=True)
        hit = blk == first
        sel = jnp.where(hit, 1.0, sel)
        imp = jnp.where(hit, -jnp.inf, imp)
    return sel


def _block_importance(imp, blk, cur):
    return jnp.where((blk == cur) | (blk == 0), FORCED_SCORE, jnp.where(blk > cur, -1.0, imp))


def _ada_kernel(c_ref, w_ref, b_ref, o_ref):
    o_ref[...] = _dot(_silu(c_ref[...]).astype(BF16), w_ref[...]) + b_ref[...]


def _ada_all_layers(c_all, w_ada, b_ada):
    depth, rows = w_ada.shape[0], c_all.shape[0]
    tn = 1536
    return pl.pallas_call(
        _ada_kernel,
        grid=(depth, 6 * D_MODEL // tn),
        in_specs=[pl.BlockSpec((rows, D_MODEL), lambda l, j: (0, 0)),
                  pl.BlockSpec((None, D_MODEL, tn), lambda l, j: (l, 0, j)),
                  pl.BlockSpec((None, 1, tn), lambda l, j: (l, 0, j))],
        out_specs=pl.BlockSpec((None, rows, tn), lambda l, j: (l, 0, j)),
        out_shape=jax.ShapeDtypeStruct((depth, rows, 6 * D_MODEL), F32),
        compiler_params=_params("parallel", "parallel"),
        name="ada",
    )(c_all, w_ada, b_ada)


def _mod_operand(mod, rows_per_batch, tm):
    if rows_per_batch % tm == 0:
        per = rows_per_batch // tm
        return mod[:, None, :], pl.BlockSpec((None, 1, D_MODEL), lambda i: (i // per, 0, 0))
    return jnp.repeat(mod, rows_per_batch, axis=0), pl.BlockSpec((tm, D_MODEL), lambda i: (i, 0))


def _const_spec(shape):
    nd = len(shape)
    return pl.BlockSpec(shape, lambda i: (0,) * nd)


def _inproj_kernel(x_ref, gn_ref, sc_ref, sh_ref, wu_ref, wq_ref, wkv_ref, wg_ref, wm_ref,
                   gq_ref, gks_ref, gkw_ref, cos_ref, sin_ref, seg_ref, pe_ref, ug_ref, gm_ref, *out, transposed):
    h = _rms(x_ref[...]) * gn_ref[...]
    h = h * (1.0 + sc_ref[...]) + sh_ref[...]
    hb = h.astype(BF16)

    u = _dot(hb, wu_ref[...])
    ug_ref[...] = u[:, :D_CONV] * _sigmoid(u[:, D_CONV:])
    gm_ref[...] = _sigmoid(_dot(hb, wm_ref[...])).astype(BF16)
    g3 = _sigmoid(_dot(hb, wg_ref[...]))

    cos1, sin1 = cos_ref[...], sin_ref[...]
    cos4 = jnp.concatenate([cos1] * 4, axis=1)
    sin4 = jnp.concatenate([sin1] * 4, axis=1)
    qn = _head_rms(_dot(hb, wq_ref[...]), seg_ref[...]) * gq_ref[...]
    qr = _rope(qn, cos4, sin4) * ATTN_SCALE
    qn = qn * ATTN_SCALE

    kv = _dot(hb, wkv_ref[...])
    seg1 = seg_ref[:LANES, :LANES]
    ks = _rope(_head_rms(kv[:, 2 * LANES:3 * LANES], seg1) * gks_ref[...], cos1, sin1)
    kw = _rope(_head_rms(kv[:, 4 * LANES:5 * LANES], seg1) * gkw_ref[...], cos1, sin1)
    vs = kv[:, 3 * LANES:4 * LANES]
    vw = kv[:, 5 * LANES:]

    if not transposed:
        qn_ref, qr_ref, rows_ref, win_ref, g3_ref = out
        qn_ref[...] = qn.astype(BF16)
        qr_ref[...] = qr.astype(BF16)
        rows_ref[:, :2 * LANES] = kv[:, :2 * LANES]
        rows_ref[:, 2 * LANES:3 * LANES] = ks
        rows_ref[:, 3 * LANES:] = vs
        win_ref[:, :LANES] = kw
        win_ref[:, LANES:] = vw
        g3_ref[...] = g3
        return

    qnt_ref, qrt_ref, rowst_ref, wint_ref, g3t_ref, cmp_ref, ks_ref, kw_ref, vst_ref, vwt_ref = out
    qnt_ref[...] = qn.T.astype(BF16)
    qrt_ref[...] = qr.T.astype(BF16)
    rowst_ref[:2 * LANES, :] = kv[:, :2 * LANES].T
    rowst_ref[2 * LANES:3 * LANES, :] = ks.T
    vst = vs.T
    rowst_ref[3 * LANES:, :] = vst
    wint_ref[:LANES, :] = kw.T
    vwt = vw.T
    wint_ref[LANES:, :] = vwt
    g3t_ref[...] = g3.T[:3 * N_HEADS, :]
    cmp_ref[...] = (kv[:, :2 * LANES] + pe_ref[...]).astype(BF16)
    ks_ref[...] = ks.astype(BF16)
    kw_ref[...] = kw.astype(BF16)
    vst_ref[...] = vst.astype(BF16)
    vwt_ref[...] = vwt.astype(BF16)


def _inproj(x, mods, lw, tabs, rows_per_batch, tm, transposed):
    n = x.shape[0]
    sc, sc_spec = _mod_operand(mods["sc1"], rows_per_batch, tm)
    sh, sh_spec = _mod_operand(mods["sh1"], rows_per_batch, tm)
    cos, sin, tab_tiles = tabs["cos"], tabs["sin"], tabs["cos"].shape[0] // tm
    tab_spec = pl.BlockSpec((tm, LANES), lambda i: (i % tab_tiles, 0))
    row = lambda w: pl.BlockSpec((tm, w), lambda i: (i, 0))
    col = lambda h: pl.BlockSpec((h, tm), lambda i: (0, i))
    sds = jax.ShapeDtypeStruct
    out_shape = [sds((n, D_CONV), F32), sds((n, 2 * D_MODEL), BF16)]
    out_specs = [row(D_CONV), row(2 * D_MODEL)]
    if transposed:
        per = rows_per_batch // tm
        nb = n // rows_per_batch
        bcol = lambda h: pl.BlockSpec((None, h, tm), lambda i: (i // per, 0, i % per))
        out_shape += [sds((512, n), BF16), sds((512, n), BF16), sds((nb, 512, rows_per_batch), F32),
                      sds((nb, 256, rows_per_batch), F32), sds((3 * N_HEADS, n), F32), sds((n, 256), BF16),
                      sds((n, LANES), BF16), sds((n, LANES), BF16), sds((LANES, n), BF16), sds((LANES, n), BF16)]
        out_specs += [col(512), col(512), bcol(512), bcol(256), col(3 * N_HEADS), row(256),
                      row(LANES), row(LANES), col(LANES), col(LANES)]
    else:
        out_shape += [sds((n, 512), BF16), sds((n, 512), BF16), sds((n, 512), F32), sds((n, 256), F32),
                      sds((n, LANES), F32)]
        out_specs += [row(512), row(512), row(512), row(256), row(LANES)]
    return pl.pallas_call(
        functools.partial(_inproj_kernel, transposed=transposed),
        grid=(n // tm,),
        in_specs=[row(D_MODEL), _const_spec((1, D_MODEL)), sc_spec, sh_spec,
                  _const_spec(lw["w_u"].shape), _const_spec(lw["w_q"].shape), _const_spec(lw["w_kv"].shape),
                  _const_spec(lw["w_g"].shape), _const_spec(lw["w_m"].shape),
                  _const_spec((1, 512)), _const_spec((1, LANES)), _const_spec((1, LANES)),
                  tab_spec, tab_spec, _const_spec((512, 512)), _const_spec((tm, 256))],
        out_specs=out_specs,
        out_shape=out_shape,
        compiler_params=_params("parallel"),
        name="inproj",
    )(x, lw["g_norm_mix"], sc, sh, lw["w_u"], lw["w_q"], lw["w_kv"], lw["w_g"], lw["w_m"],
      lw["g_q"], lw["g_ks"], lw["g_kw"], cos, sin, tabs["seg"], lw["pe_tile"][:tm])


def _conv_kernel(*refs, tc, n_tiles):
    if n_tiles > 1:
        cur_ref, halo_ref, pre_ref, w_ref, b_ref, g_ref, beta_ref, y_ref, win_ref = refs
    else:
        cur_ref, pre_ref, w_ref, b_ref, g_ref, beta_ref, y_ref, win_ref = refs
    if n_tiles > 1:
        first = pl.program_id(1) == 0

        @pl.when(first)
        def _():
            win_ref[:CONV_HALO, :] = pre_ref[...]

        @pl.when(jnp.logical_not(first))
        def _():
            win_ref[:CONV_HALO, :] = halo_ref[...]
    else:
        win_ref[:CONV_HALO, :] = pre_ref[...]
    win_ref[CONV_HALO:, :] = cur_ref[...]

    rc = min(tc, 64)
    off = CONV_HALO - (CONV_WIDTH - 1)
    for r0 in range(0, tc, rc):
        acc = jnp.zeros((rc, D_CONV), F32)
        for j in range(CONV_WIDTH):
            acc = acc + w_ref[j:j + 1, :] * win_ref[r0 + off + j:r0 + off + j + rc, :]
        y = acc + b_ref[...]
        mu = jnp.mean(y, axis=-1, keepdims=True)
        yc = y - mu
        var = jnp.mean(yc * yc, axis=-1, keepdims=True)
        y = yc * lax.rsqrt(var + EPS) * g_ref[...] + beta_ref[...]
        y_ref[r0:r0 + rc, :] = _silu(y).astype(BF16)


def _conv_branch(ug, prefix, lw, tc):
    b, t, _ = ug.shape
    n_tiles = t // tc
    cur_spec = pl.BlockSpec((None, tc, D_CONV), lambda bi, i: (bi, i, 0))
    pre_spec = pl.BlockSpec((None, CONV_HALO, D_CONV), lambda bi, i: (bi, 0, 0))
    vec = lambda r: pl.BlockSpec((r, D_CONV), lambda bi, i: (0, 0))
    ops, specs = [ug], [cur_spec]
    if n_tiles > 1:
        per = tc // CONV_HALO
        ops.append(ug)
        specs.append(pl.BlockSpec((None, CONV_HALO, D_CONV), lambda bi, i: (bi, jnp.maximum(i * per - 1, 0), 0)))
    ops += [prefix, lw["w_dw"], lw["b_dw"], lw["ln_conv_g"], lw["ln_conv_b"]]
    specs += [pre_spec, vec(CONV_HALO), vec(1), vec(1), vec(1)]
    return pl.pallas_call(
        functools.partial(_conv_kernel, tc=tc, n_tiles=n_tiles),
        grid=(b, n_tiles),
        in_specs=specs,
        out_specs=pl.BlockSpec((None, tc, D_CONV), lambda bi, i: (bi, i, 0)),
        out_shape=jax.ShapeDtypeStruct((b, t, D_CONV), BF16),
        scratch_shapes=[pltpu.VMEM((CONV_HALO + tc, D_CONV), F32)],
        compiler_params=_params("parallel", "parallel"),
        name="conv",
    )(*ops)


def _compress_kernel(x_ref, w1_ref, b1_ref, w2_ref, b2_ref, gkc_ref, o_ref):
    hid = jax.nn.gelu(_dot(x_ref[...], w1_ref[...]) + b1_ref[...])
    o = _dot(hid.astype(BF16), w2_ref[...]) + b2_ref[...]
    is_key = pl.program_id(0) == 0
    o_ref[...] = jnp.where(is_key, _rms(o) * gkc_ref[...], o)


def _compress(flat, lw):
    _, m, kdim = flat.shape
    tm = min(m, 512)
    hidden = lw["cmp_w1"].shape[-1]
    return pl.pallas_call(
        _compress_kernel,
        grid=(2, m // tm),
        in_specs=[pl.BlockSpec((None, tm, kdim), lambda s, i: (s, i, 0)),
                  pl.BlockSpec((None, kdim, hidden), lambda s, i: (s, 0, 0)),
                  pl.BlockSpec((None, 1, hidden), lambda s, i: (s, 0, 0)),
                  pl.BlockSpec((None, hidden, HEAD_DIM), lambda s, i: (s, 0, 0)),
                  pl.BlockSpec((None, 1, HEAD_DIM), lambda s, i: (s, 0, 0)),
                  pl.BlockSpec((1, HEAD_DIM), lambda s, i: (0, 0))],
        out_specs=pl.BlockSpec((None, tm, HEAD_DIM), lambda s, i: (s, i, 0)),
        out_shape=jax.ShapeDtypeStruct((2, m, HEAD_DIM), F32),
        compiler_params=_params("parallel", "parallel"),
        name="compress",
    )(flat, lw["cmp_w1"], lw["cmp_b1"], lw["cmp_w2"], lw["cmp_b2"], lw["g_kc"])


def _flatten_blocks(cmp_in, b, n_blk):
    x = cmp_in.reshape(b, n_blk, CMP_BLOCK, 2, N_KV, HEAD_DIM)
    x = x.transpose(3, 0, 4, 1, 2, 5)
    return x.reshape(2, b * N_KV * n_blk, CMP_BLOCK * HEAD_DIM)


def _gather_pages_kernel(pt_ref, *refs):
    del pt_ref
    pages, o_ref = refs[:GATHER_PAGES], refs[GATHER_PAGES]
    rows = N_KV * HEAD_DIM
    for k, pg in enumerate(pages):
        for s in range(2):
            o_ref[s, k * rows:(k + 1) * rows, :] = pg[s].reshape(rows, PAGE_SIZE)


def _gather_pages(page_table, cache_t):
    depth = cache_t.shape[0]
    db, n_pages = page_table.shape
    steps = n_pages // GATHER_PAGES
    rows = N_KV * HEAD_DIM

    def page_spec(k):
        return pl.BlockSpec((None, None, 2, N_KV, HEAD_DIM, PAGE_SIZE),
                            lambda l, b, j, pt: (l, pt[b, j * GATHER_PAGES + k], 0, 0, 0, 0))

    grid_spec = pltpu.PrefetchScalarGridSpec(
        num_scalar_prefetch=1,
        grid=(depth, db, steps),
        in_specs=[page_spec(k) for k in range(GATHER_PAGES)],
        out_specs=pl.BlockSpec((None, None, 2, GATHER_PAGES * rows, PAGE_SIZE), lambda l, b, j, pt: (l, b, 0, j, 0)))
    return pl.pallas_call(
        _gather_pages_kernel,
        grid_spec=grid_spec,
        out_shape=jax.ShapeDtypeStruct((depth, db, 2, n_pages * rows, PAGE_SIZE), F32),
        compiler_params=_params("parallel", "parallel", "parallel"),
        name="gather_pages",
    )(page_table, *([cache_t] * GATHER_PAGES))


def _compress_paged_kernel(x_ref, w_ref, pe_ref, b1_ref, w2_ref, b2_ref, gkc_ref, seg_ref, o_ref, *, n_rows):
    acc = jnp.zeros((n_rows, w_ref.shape[-1]), F32)
    for d in range(HEAD_DIM):
        a = x_ref[pl.ds(d, n_rows, stride=HEAD_DIM), :] + pe_ref[d:d + 1, :]
        acc = acc + _dot(a.astype(BF16), w_ref[d])
    hid = jax.nn.gelu(acc + b1_ref[...])
    o = _dot(hid.astype(BF16), w2_ref[...]) + b2_ref[...]
    is_key = pl.program_id(1) == 0
    o_ref[...] = jnp.where(is_key, _head_rms(o, seg_ref[...]) * gkc_ref[...], o)


def _compress_paged(x, pw):
    depth, db, _, rows, _ = x.shape
    n_rows = rows // HEAD_DIM
    hid2 = pw["w1"].shape[-1]
    sel = lambda *s: pl.BlockSpec((None, None) + s, lambda l, sl, b: (l, sl) + (0,) * len(s))
    return pl.pallas_call(
        functools.partial(_compress_paged_kernel, n_rows=n_rows),
        grid=(depth, 2, db),
        in_specs=[pl.BlockSpec((None, None, None, rows, PAGE_SIZE), lambda l, sl, b: (l, b, sl, 0, 0)),
                  sel(HEAD_DIM, PAGE_SIZE, hid2), sel(HEAD_DIM, PAGE_SIZE), sel(1, hid2), sel(hid2, LANES),
                  sel(1, LANES), pl.BlockSpec((None, 1, LANES), lambda l, sl, b: (l, 0, 0)),
                  pl.BlockSpec((LANES, LANES), lambda l, sl, b: (0, 0))],
        out_specs=pl.BlockSpec((None, None, None, n_rows, LANES), lambda l, sl, b: (l, sl, b, 0, 0)),
        out_shape=jax.ShapeDtypeStruct((depth, 2, db, n_rows, LANES), F32),
        compiler_params=_params("parallel", "parallel", "parallel"),
        name="compress_paged",
    )(x, pw["w1"], pw["pe"], pw["b1"], pw["w2"], pw["b2"], pw["g_kc"], pw["seg"])


def _attn_prompt_kernel(qn_ref, qr_ref, g3_ref, kc_ref, vc_ref, ks_ref, vst_ref, kw_ref, vwt_ref,
                        o_ref, bias_ref, ocmp_ref, ms_ref, ls_ref, accs_ref, mw_ref, lw_ref, accw_ref, *, tq):
    i = pl.program_id(1)
    t0 = i * tq
    lq = Q_PER_KV * tq
    lane = lax.broadcasted_iota(jnp.int32, (1, lq), 1)
    qpos = t0 + _mod(lane, tq)
    nb = kc_ref.shape[1]
    blk = lax.broadcasted_iota(jnp.int32, (nb, 1), 0)
    zpad = jnp.zeros((HEAD_DIM, lq), BF16)

    def heads(ref, g):
        return jnp.concatenate([ref[(Q_PER_KV * g + r) * HEAD_DIM:(Q_PER_KV * g + r + 1) * HEAD_DIM, :]
                                for r in range(Q_PER_KV)], axis=1)

    qpad = []
    for g in range(N_KV):
        q4 = heads(qr_ref, g)
        qpad.append(jnp.concatenate([q4, zpad] if g == 0 else [zpad, q4], axis=0))

        cm = blk < _div(qpos + 1, CMP_BLOCK)
        s = jnp.where(cm, _dot(kc_ref[g].astype(BF16), heads(qn_ref, g)), NEG)
        p = jnp.exp(s - jnp.max(s, axis=0, keepdims=True)) * cm.astype(F32)
        p = p / jnp.maximum(jnp.sum(p, axis=0, keepdims=True), 1e-30)
        ocmp_ref[g] = _dot_t0(vc_ref[g].astype(BF16), p.astype(BF16))

        imp = p[:, :tq]
        for r in range(1, Q_PER_KV):
            imp = imp + p[:, r * tq:(r + 1) * tq]
        imp = _block_importance(imp, blk, _div(qpos[:, :tq], CMP_BLOCK))
        bias_ref[g] = (_topk_mask(imp, N_SELECT) - 1.0) * (-NEG)

    for m_ref, l_ref, acc_ref in ((ms_ref, ls_ref, accs_ref), (mw_ref, lw_ref, accw_ref)):
        m_ref[...] = jnp.full(m_ref.shape, NEG, F32)
        l_ref[...] = jnp.zeros(l_ref.shape, F32)
        acc_ref[...] = jnp.zeros(acc_ref.shape, F32)

    kidx = lax.broadcasted_iota(jnp.int32, (KEY_CHUNK, 1), 0)
    row8 = lax.broadcasted_iota(jnp.int32, (8, 1), 0)
    blocks_per_chunk = KEY_CHUNK // CMP_BLOCK

    def update(s, vt, g, m_ref, l_ref, acc_ref):
        m = m_ref[g]
        m_new = jnp.maximum(m, jnp.max(s, axis=0, keepdims=True))
        alpha = jnp.exp(m - m_new)
        p = jnp.exp(s - m_new)
        l_ref[g] = alpha * l_ref[g] + jnp.sum(p, axis=0, keepdims=True)
        acc_ref[g] = alpha * acc_ref[g] + _dot(vt, p.astype(BF16))
        m_ref[g] = m_new

    def sel_bias(c, g):
        first = c * blocks_per_chunk
        rows8 = bias_ref[g, pl.ds(pl.multiple_of((first // 8) * 8, 8), 8), :]
        r0 = first % 8
        pieces = [jnp.broadcast_to(jnp.sum(jnp.where(row8 == r0 + j, rows8, 0.0), axis=0, keepdims=True),
                                   (CMP_BLOCK, tq)) for j in range(blocks_per_chunk)]
        bias = jnp.concatenate(pieces, axis=0)
        return jnp.concatenate([bias] * Q_PER_KV, axis=1)

    def sel_chunk(c, causal):
        k0 = pl.multiple_of(c * KEY_CHUNK, KEY_CHUNK)
        kblk = ks_ref[pl.ds(k0, KEY_CHUNK), :]
        for g in range(N_KV):
            s = _dot(kblk, qpad[g]) + sel_bias(c, g)
            if causal:
                s = jnp.where(k0 + kidx <= qpos, s, NEG)
            update(s, vst_ref[g * HEAD_DIM:(g + 1) * HEAD_DIM, pl.ds(k0, KEY_CHUNK)], g, ms_ref, ls_ref, accs_ref)

    def sel_body(c, carry):
        sel_chunk(c, False)
        return carry

    lax.fori_loop(0, i, sel_body, 0)
    sel_chunk(i, True)

    def win_chunk(c):
        k0 = pl.multiple_of(t0 - WINDOW + c * KEY_CHUNK, KEY_CHUNK)
        kblk = kw_ref[pl.ds(k0, KEY_CHUNK), :]
        for g in range(N_KV):
            s = _dot(kblk, qpad[g])
            if c == 0:
                s = jnp.where(qpos - (k0 + kidx) < WINDOW, s, NEG)
            elif c == WINDOW // KEY_CHUNK:
                s = jnp.where(k0 + kidx <= qpos, s, NEG)
            update(s, vwt_ref[g * HEAD_DIM:(g + 1) * HEAD_DIM, pl.ds(k0, KEY_CHUNK)], g, mw_ref, lw_ref, accw_ref)

    n_win = WINDOW // KEY_CHUNK
    for c in range(n_win):
        pl.when(t0 >= WINDOW - c * KEY_CHUNK)(functools.partial(win_chunk, c))
    win_chunk(n_win)

    for g in range(N_KV):
        gate = lambda br: jnp.concatenate(
            [g3_ref[br * N_HEADS + Q_PER_KV * g + r:br * N_HEADS + Q_PER_KV * g + r + 1, :]
             for r in range(Q_PER_KV)], axis=1)
        o = (gate(0) * ocmp_ref[g] + gate(1) * (accs_ref[g] / ls_ref[g]) + gate(2) * (accw_ref[g] / lw_ref[g]))
        for r in range(Q_PER_KV):
            o_ref[(Q_PER_KV * g + r) * HEAD_DIM:(Q_PER_KV * g + r + 1) * HEAD_DIM, :] = (
                o[:, r * tq:(r + 1) * tq].astype(BF16))


def _attn_prompt(qn_t, qr_t, g3_t, kc, vc, ks, vst, kw, vwt, b, t):
    tq = KEY_CHUNK
    nq = t // tq
    lq = Q_PER_KV * tq
    n_cmp = kc.shape[2]
    qcol = lambda h: pl.BlockSpec((h, tq), lambda bi, i: (0, bi * nq + i))
    per_b = pl.BlockSpec((None, N_KV, n_cmp, HEAD_DIM), lambda bi, i: (bi, 0, 0, 0))
    keys = pl.BlockSpec((t, LANES), lambda bi, i: (bi, 0))
    vals = pl.BlockSpec((LANES, t), lambda bi, i: (0, bi))
    st = lambda r: pltpu.VMEM((N_KV, r, lq), F32)
    return pl.pallas_call(
        functools.partial(_attn_prompt_kernel, tq=tq),
        grid=(b, nq),
        in_specs=[qcol(512), qcol(512), qcol(3 * N_HEADS), per_b, per_b, keys, vals, keys, vals],
        out_specs=qcol(512),
        out_shape=jax.ShapeDtypeStruct((512, b * t), BF16),
        scratch_shapes=[pltpu.VMEM((N_KV, n_cmp, tq), F32), st(HEAD_DIM),
                        st(1), st(1), st(HEAD_DIM), st(1), st(1), st(HEAD_DIM)],
        compiler_params=_params("parallel", "arbitrary"),
        name="attn_prompt",
    )(qn_t, qr_t, g3_t, kc, vc, ks, vst, kw, vwt)


def _attn_sample_kernel(pt_ref, *refs, n_pages, n_blk, pos0, tq):
    pages = refs[:ATTN_PAGES]
    (q_ref, qnt_ref, kc_ref, vc_ref, kn_ref, vn_ref, kwn_ref, vwn_ref, win_ref, g3_ref, expand_ref, rsum_ref,
     o_ref, sel_ref, m_ref, l_ref, acc_ref, ocmp_ref) = refs[ATTN_PAGES:]
    del pt_ref
    j = pl.program_id(1)
    lq = Q_PER_KV * tq
    nbp = sel_ref.shape[1]
    n_cmp = kc_ref.shape[1]
    step_keys = ATTN_PAGES * PAGE_SIZE

    @pl.when(j == 0)
    def _():
        lane = lax.broadcasted_iota(jnp.int32, (1, lq), 1)
        qpos = pos0 + _mod(lane, tq)
        blk = lax.broadcasted_iota(jnp.int32, (nbp, 1), 0)
        for g in range(N_KV):
            cm = blk[:n_cmp] < _div(qpos + 1, CMP_BLOCK)
            s = jnp.where(cm, _dot(kc_ref[g].astype(BF16), qnt_ref[g]), NEG)
            p = jnp.exp(s - jnp.max(s, axis=0, keepdims=True)) * cm.astype(F32)
            p = p / jnp.maximum(jnp.sum(p, axis=0, keepdims=True), 1e-30)
            ocmp_ref[g] = _dot_t0(p.astype(BF16), vc_ref[g].astype(BF16))
            imp = _split_dot(p, rsum_ref[...])
            imp = jnp.concatenate([imp, jnp.zeros((nbp - n_cmp, lq), F32)], axis=0)
            imp = _block_importance(imp, blk, _div(qpos, CMP_BLOCK))
            imp = jnp.where(blk >= n_blk, -jnp.inf, imp)
            sel_ref[g] = _topk_mask(imp, N_SELECT)
        m_ref[...] = jnp.full(m_ref.shape, NEG, F32)
        l_ref[...] = jnp.zeros(l_ref.shape, F32)
        acc_ref[...] = jnp.zeros(acc_ref.shape, F32)

    def update(g, s, pv):
        m = m_ref[g]
        m_new = jnp.maximum(m, jnp.max(s, axis=1, keepdims=True))
        alpha = jnp.exp(m - m_new)
        p = jnp.exp(s - m_new)
        l_ref[g] = alpha * l_ref[g] + jnp.sum(p, axis=1, keepdims=True)
        acc_ref[g] = alpha * acc_ref[g] + pv(p)
        m_ref[g] = m_new

    expand = expand_ref[:, pl.ds(pl.multiple_of(j * step_keys, step_keys), step_keys)]
    for g in range(N_KV):
        kt = jnp.concatenate([pg[0, g].astype(BF16) for pg in pages], axis=1)
        vt = jnp.concatenate([pg[1, g].astype(BF16) for pg in pages], axis=1)
        picked = _dot_t0(sel_ref[g, :n_cmp, :].astype(BF16), expand)
        s = _dot(q_ref[g], kt) + (picked - 1.0) * (-NEG)
        update(g, s, lambda p: _dot_t1(p.astype(BF16), vt))

    @pl.when(j == n_pages // ATTN_PAGES - 1)
    def _():
        rowi = _mod(lax.broadcasted_iota(jnp.int32, (lq, 1), 0), tq)
        keyi = lax.broadcasted_iota(jnp.int32, (1, tq), 1)
        causal = keyi <= rowi
        wlane = lax.broadcasted_iota(jnp.int32, (1, WINDOW), 1)
        kpos = pos0 - WINDOW + wlane
        in_win = ((pos0 + rowi) - kpos < WINDOW) & (kpos >= 0)
        cur_blk = pos0 // CMP_BLOCK
        first_row = (lax.broadcasted_iota(jnp.int32, (8, tq), 0) == cur_blk % 8).astype(F32)
        for g in range(N_KV):
            q = q_ref[g]
            qf = q.astype(F32)
            base = (cur_blk // 8) * 8
            picked = _dot_t0(sel_ref[g, base:base + 8, :], first_row)
            s = jnp.where(causal & (picked > 0.5), _dot_t1(qf, kn_ref[g]), NEG)
            update(g, s, lambda p: _dot(p, vn_ref[g]))
            o_sel = acc_ref[g] / l_ref[g]

            s_pre = jnp.where(in_win, _dot(q, win_ref[0, g].astype(BF16)), NEG)
            s_new = jnp.where(causal, _dot_t1(qf, kwn_ref[g]), NEG)
            m = jnp.maximum(jnp.max(s_pre, axis=1, keepdims=True), jnp.max(s_new, axis=1, keepdims=True))
            p_pre = jnp.exp(s_pre - m) * in_win.astype(F32)
            p_new = jnp.exp(s_new - m) * causal.astype(F32)
            l = jnp.sum(p_pre, axis=1, keepdims=True) + jnp.sum(p_new, axis=1, keepdims=True)
            o_win = (_dot_t1(p_pre.astype(BF16), win_ref[1, g].astype(BF16)) + _dot(p_new, vwn_ref[g]))
            o_win = o_win / jnp.maximum(l, 1e-30)

            g3 = g3_ref[g]
            o_ref[g] = g3[:, 0:1] * ocmp_ref[g] + g3[:, 1:2] * o_sel + g3[:, 2:3] * o_win


def _attn_sample(page_table, cache_t, layer, q, qn_t, kc, vc, kn, vn, kwn, vwn, win_t, g3, expand, rsum, pos0, tq):
    db, n_pages = page_table.shape
    lq = Q_PER_KV * tq
    n_cmp = kc.shape[2]
    n_blk = -(-(pos0 + tq) // CMP_BLOCK)
    nbp = -(-n_blk // 8) * 8
    steps = n_pages // ATTN_PAGES

    def page_spec(k):
        return pl.BlockSpec((None, None, 2, N_KV, HEAD_DIM, PAGE_SIZE),
                            lambda b, j, pt: (layer, pt[b, j * ATTN_PAGES + k], 1, 0, 0, 0))

    per_b = lambda *s: pl.BlockSpec((None,) + s, lambda b, j, pt: (b,) + (0,) * len(s))
    win_spec = pl.BlockSpec((None, None, 2, N_KV, HEAD_DIM, WINDOW), lambda b, j, pt: (layer, b, 0, 0, 0, 0))
    const = lambda *s: pl.BlockSpec(s, lambda b, j, pt: (0,) * len(s))
    grid_spec = pltpu.PrefetchScalarGridSpec(
        num_scalar_prefetch=1,
        grid=(db, steps),
        in_specs=[page_spec(k) for k in range(ATTN_PAGES)] + [
            per_b(N_KV, lq, HEAD_DIM), per_b(N_KV, HEAD_DIM, lq), per_b(N_KV, n_cmp, HEAD_DIM),
            per_b(N_KV, n_cmp, HEAD_DIM), per_b(N_KV, tq, HEAD_DIM), per_b(N_KV, tq, HEAD_DIM),
            per_b(N_KV, tq, HEAD_DIM), per_b(N_KV, tq, HEAD_DIM), win_spec,
            per_b(N_KV, lq, 8), const(n_cmp, n_pages * PAGE_SIZE), const(lq, lq)],
        out_specs=per_b(N_KV, lq, HEAD_DIM),
        scratch_shapes=[pltpu.VMEM((N_KV, nbp, lq), F32), pltpu.VMEM((N_KV, lq, 1), F32),
                        pltpu.VMEM((N_KV, lq, 1), F32), pltpu.VMEM((N_KV, lq, HEAD_DIM), F32),
                        pltpu.VMEM((N_KV, lq, HEAD_DIM), F32)])
    return pl.pallas_call(
        functools.partial(_attn_sample_kernel, n_pages=n_pages, n_blk=n_blk, pos0=pos0, tq=tq),
        grid_spec=grid_spec,
        out_shape=jax.ShapeDtypeStruct((db, N_KV, lq, HEAD_DIM), F32),
        compiler_params=_params("parallel", "arbitrary"),
        name="attn_sample",
    )(page_table, *([cache_t] * ATTN_PAGES), q, qn_t, kc, vc, kn, vn, kwn, vwn, win_t, g3, expand, rsum)


def _merge_kernel(x_ref, y_ref, o_ref, gm_ref, gt_ref, sc_ref, sh_ref, gn_ref, wc_ref, wa_ref, wo_ref,
                  wr_ref, br_ref, x1_ref, h2_ref, dw_ref, *, o_transposed):
    gm = gm_ref[...]
    attn = _dot_t0(o_ref[...], wa_ref[...]) if o_transposed else _dot(o_ref[...], wa_ref[...])
    merged = gm[:, :D_MODEL].astype(F32) * _dot(y_ref[...], wc_ref[...]) + gm[:, D_MODEL:].astype(F32) * attn
    x1 = x_ref[...] + gt_ref[...] * _dot(merged.astype(BF16), wo_ref[...])
    x1_ref[...] = x1
    h2 = _rms(x1) * gn_ref[...]
    h2 = h2 * (1.0 + sc_ref[...]) + sh_ref[...]
    h2_ref[...] = h2.astype(BF16)

    w_hi = wr_ref[0]
    w_lo = wr_ref[1]
    h_hi = h2.astype(BF16)
    h_lo = (h2 - h_hi.astype(F32)).astype(BF16)
    aff = _sigmoid(_dot(h_hi, w_hi) + _dot(h_lo, w_hi) + _dot(h_hi, w_lo))
    score = aff + br_ref[...]
    e = lax.broadcasted_iota(jnp.int32, score.shape, 1)
    grp = _div(e, EXPERTS_PER_GROUP)
    e = e.astype(F32)
    big = float(LANES)

    def top2(j):
        vals = jnp.where(grp == j, score, -jnp.inf)
        m1 = jnp.max(vals, axis=-1, keepdims=True)
        i1 = jnp.min(jnp.where(vals == m1, e, big), axis=-1, keepdims=True)
        vals = jnp.where(e == i1, -jnp.inf, vals)
        m2 = jnp.max(vals, axis=-1, keepdims=True)
        i2 = jnp.min(jnp.where(vals == m2, e, big), axis=-1, keepdims=True)
        return m1 + m2, i1, i2

    best, ia, ib = top2(0)
    for j in range(1, N_GROUPS):
        gs, i1, i2 = top2(j)
        better = gs > best
        best = jnp.where(better, gs, best)
        ia = jnp.where(better, i1, ia)
        ib = jnp.where(better, i2, ib)
    hit_a = e == ia
    hit_b = e == ib
    aff_a = jnp.sum(jnp.where(hit_a, aff, 0.0), axis=-1, keepdims=True)
    aff_b = jnp.sum(jnp.where(hit_b, aff, 0.0), axis=-1, keepdims=True)
    tot = aff_a + aff_b
    dw_ref[...] = jnp.where(hit_a, aff_a / tot, 0.0) + jnp.where(hit_b, aff_b / tot, 0.0)


def _merge(x, y, o, gm, mods, lw, shared, rows_per_batch, tm, o_transposed):
    n = x.shape[0]
    gt, gt_spec = _mod_operand(mods["gt1"], rows_per_batch, tm)
    sc, sc_spec = _mod_operand(mods["sc2"], rows_per_batch, tm)
    sh, sh_spec = _mod_operand(mods["sh2"], rows_per_batch, tm)
    row = lambda w: pl.BlockSpec((tm, w), lambda i: (i, 0))
    o_spec = pl.BlockSpec((512, tm), lambda i: (0, i)) if o_transposed else row(512)
    return pl.pallas_call(
        functools.partial(_merge_kernel, o_transposed=o_transposed),
        grid=(n // tm,),
        in_specs=[row(D_MODEL), row(D_CONV), o_spec, row(2 * D_MODEL), gt_spec, sc_spec, sh_spec,
                  _const_spec((1, D_MODEL)), _const_spec((D_CONV, D_MODEL)), _const_spec((512, D_MODEL)),
                  _const_spec((D_MODEL, D_MODEL)), _const_spec((2, D_MODEL, LANES)), _const_spec((1, LANES))],
        out_specs=[row(D_MODEL), row(D_MODEL), row(LANES)],
        out_shape=[jax.ShapeDtypeStruct((n, D_MODEL), F32), jax.ShapeDtypeStruct((n, D_MODEL), BF16),
                   jax.ShapeDtypeStruct((n, LANES), F32)],
        compiler_params=_params("parallel"),
        name="merge",
    )(x, y, o, gm, gt, sc, sh, lw["g_norm_ffn"], lw["w_conv_out"], lw["w_attn_out"], lw["w_out"],
      shared["w_router"], shared["b_router"])


def _moe_kernel(h_ref, dw_ref, x_ref, gt_ref, wg_ref, wu_ref, wd_ref, o_ref, acc_ref):
    ex = pl.program_id(1)

    @pl.when(ex == 0)
    def _():
        acc_ref[...] = jnp.zeros(acc_ref.shape, F32)

    h = h_ref[...]
    lane = lax.broadcasted_iota(jnp.int32, dw_ref.shape, 1)
    w_col = jnp.sum(jnp.where(lane == ex, dw_ref[...], 0.0), axis=-1, keepdims=True)
    hid = _silu(_dot(h, wg_ref[...])) * _dot(h, wu_ref[...]) * w_col
    acc_ref[...] += _dot(hid.astype(BF16), wd_ref[...])

    @pl.when(ex == N_EXPERTS - 1)
    def _():
        o_ref[...] = x_ref[...] + gt_ref[...] * acc_ref[...]


def _moe(h2, dw, x1, mods, lw, rows_per_batch, tm):
    n = h2.shape[0]
    gt, gt_spec = _mod_operand(mods["gt2"], rows_per_batch, tm)
    gt_spec = pl.BlockSpec(gt_spec.block_shape, lambda i, ex, f=gt_spec.index_map: f(i))
    row = lambda w: pl.BlockSpec((tm, w), lambda i, ex: (i, 0))
    return pl.pallas_call(
        _moe_kernel,
        grid=(n // tm, N_EXPERTS),
        in_specs=[row(D_MODEL), row(LANES), row(D_MODEL), gt_spec,
                  pl.BlockSpec((None, D_MODEL, D_EXPERT), lambda i, ex: (ex, 0, 0)),
                  pl.BlockSpec((None, D_MODEL, D_EXPERT), lambda i, ex: (ex, 0, 0)),
                  pl.BlockSpec((None, D_EXPERT, D_MODEL), lambda i, ex: (ex, 0, 0))],
        out_specs=row(D_MODEL),
        out_shape=jax.ShapeDtypeStruct((n, D_MODEL), F32),
        scratch_shapes=[pltpu.VMEM((tm, D_MODEL), F32)],
        compiler_params=_params("parallel", "arbitrary"),
        name="moe",
    )(h2, dw, x1, gt, lw["w_exp_gate"], lw["w_exp_up"], lw["w_exp_down"])


def _rope_tables(pos):
    half = HEAD_DIM // 2
    inv = ROPE_THETA ** (-jnp.arange(half, dtype=F32) / half)
    ang = pos.astype(F32)[:, None] * inv[None, :]
    cos, sin = jnp.cos(ang), jnp.sin(ang)
    cos = jnp.concatenate([cos, cos], axis=1)
    sin = jnp.concatenate([-sin, sin], axis=1)
    return jnp.concatenate([cos, cos], axis=1), jnp.concatenate([sin, sin], axis=1)


def _split_mod(mod):
    names = ("sh1", "sc1", "gt1", "sh2", "sc2", "gt2")
    return dict(zip(names, jnp.split(mod, 6, axis=-1)))


def _paged_compress_weights(cmp_pe, cmp_w1, cmp_b1, cmp_w2, cmp_b2, g_kc, seg):
    depth = cmp_w1.shape[0]
    hidden = cmp_w1.shape[-1]
    eye = jnp.eye(2, dtype=F32)
    w1 = cmp_w1.reshape(depth, 2, CMP_BLOCK, HEAD_DIM, hidden).transpose(0, 1, 3, 2, 4)
    w1 = jnp.einsum("lsdph,jk->lsdjpkh", w1, eye).reshape(depth, 2, HEAD_DIM, PAGE_SIZE, 2 * hidden)
    w2 = jnp.einsum("lshd,jk->lsjhkd", cmp_w2, eye).reshape(depth, 2, 2 * hidden, LANES)
    pe = jnp.tile(cmp_pe.transpose(0, 1, 3, 2), (1, 1, 1, 2))
    return dict(w1=w1.astype(BF16), w2=w2.astype(BF16), pe=pe,
                b1=jnp.tile(cmp_b1, (1, 1, 2))[:, :, None, :], b2=jnp.tile(cmp_b2, (1, 1, 2))[:, :, None, :],
                g_kc=jnp.tile(g_kc, (1, 2))[:, None, :], seg=seg[:LANES, :LANES])


def kernel(x_prompt, x_sample, c_prompt, c_sample, cache_kv, state_win_kv, state_conv, page_table, w_ada, b_ada, g_norm_mix, g_norm_ffn, w_in, w_dw, b_dw, ln_conv_g, ln_conv_b, w_conv_out, g_q, g_kc, g_ks, g_kw, cmp_pe, cmp_w1, cmp_b1, cmp_w2, cmp_b2, w_attn_out, w_out, w_router, b_router, w_exp_gate, w_exp_up, w_exp_down):
    depth = w_in.shape[0]
    b, t, _ = x_prompt.shape
    db, dt, _ = x_sample.shape
    n_pages = page_table.shape[1]
    past_len = n_pages * PAGE_SIZE
    win_buf = state_win_kv.shape[2]
    tm_p, tm_s = 512, db * dt
    lq_s = Q_PER_KV * dt
    n_cmp_s = (past_len + dt) // CMP_BLOCK
    assert t % tm_p == 0 and t % KEY_CHUNK == 0 and win_buf == WINDOW and WINDOW % KEY_CHUNK == 0
    assert n_cmp_s * CMP_BLOCK == past_len and n_pages % ATTN_PAGES == 0 and n_pages % GATHER_PAGES == 0

    c0, c1, c2, c3 = 2 * D_CONV, 2 * D_CONV + 512, 2 * D_CONV + 512 + 768, 2 * D_CONV + 512 + 768 + 3 * N_HEADS
    w_in_b = w_in.astype(BF16)
    w_dw_p = jnp.pad(w_dw, ((0, 0), (0, CONV_HALO - CONV_WIDTH), (0, 0)))
    pe256 = cmp_pe[:, :, :, None, :].repeat(N_KV, axis=3).transpose(0, 2, 1, 3, 4).reshape(depth, CMP_BLOCK, 256)
    pe_tile = jnp.tile(pe256, (1, tm_p // CMP_BLOCK, 1))
    layers = []
    for l in range(depth):
        layers.append(dict(
            g_norm_mix=g_norm_mix[l][None], g_norm_ffn=g_norm_ffn[l][None],
            w_u=w_in_b[l][:, :c0], w_q=w_in_b[l][:, c0:c1], w_kv=w_in_b[l][:, c1:c2],
            w_g=jnp.pad(w_in_b[l][:, c2:c3], ((0, 0), (0, LANES - 3 * N_HEADS))), w_m=w_in_b[l][:, c3:],
            g_q=jnp.tile(g_q[l], N_HEADS)[None], g_ks=jnp.tile(g_ks[l], N_KV)[None], g_kw=jnp.tile(g_kw[l], N_KV)[None],
            g_kc=g_kc[l][None], pe_tile=pe_tile[l],
            w_dw=w_dw_p[l], b_dw=b_dw[l][None], ln_conv_g=ln_conv_g[l][None], ln_conv_b=ln_conv_b[l][None],
            w_conv_out=w_conv_out[l].astype(BF16), w_attn_out=w_attn_out[l].astype(BF16), w_out=w_out[l].astype(BF16),
            cmp_w1=cmp_w1[l].astype(BF16), cmp_b1=cmp_b1[l][:, None, :], cmp_w2=cmp_w2[l].astype(BF16),
            cmp_b2=cmp_b2[l][:, None, :],
            w_exp_gate=w_exp_gate[l].astype(BF16), w_exp_up=w_exp_up[l].astype(BF16),
            w_exp_down=w_exp_down[l].astype(BF16)))
    wr = jnp.pad(w_router, ((0, 0), (0, LANES - N_EXPERTS)))
    wr_hi = wr.astype(BF16)
    wr_lo = (wr - wr_hi.astype(F32)).astype(BF16)
    shared = dict(w_router=jnp.stack([wr_hi, wr_lo]),
                  b_router=jnp.pad(b_router, (0, LANES - N_EXPERTS))[None])
    seg = (jnp.arange(512)[:, None] // HEAD_DIM == jnp.arange(512)[None, :] // HEAD_DIM).astype(BF16)
    cos_p, sin_p = _rope_tables(jnp.arange(t, dtype=jnp.int32))
    cos_s, sin_s = _rope_tables(past_len + jnp.arange(dt, dtype=jnp.int32))
    tabs_p = dict(cos=cos_p, sin=sin_p, seg=seg)
    tabs_s = dict(cos=jnp.tile(cos_s, (db, 1)), sin=jnp.tile(sin_s, (db, 1)), seg=seg)
    rsum = (jnp.arange(lq_s)[:, None] % dt == jnp.arange(lq_s)[None, :] % dt).astype(BF16)
    expand = (jnp.arange(n_cmp_s)[:, None] == jnp.arange(past_len)[None, :] // CMP_BLOCK).astype(BF16)

    rows = b + db
    rows_p = -(-rows // 8) * 8
    c_all = jnp.pad(jnp.concatenate([c_prompt, c_sample], axis=0), ((0, rows_p - rows), (0, 0)))
    mod_all = _ada_all_layers(c_all, w_ada.astype(BF16), b_ada[:, None, :])

    cache_t = cache_kv.transpose(0, 1, 3, 4, 5, 2)
    win_t = state_win_kv.transpose(0, 1, 3, 4, 5, 2)

    paged = _compress_paged(_gather_pages(page_table, cache_t),
                            _paged_compress_weights(cmp_pe, cmp_w1, cmp_b1, cmp_w2, cmp_b2, g_kc, seg))
    paged = paged.reshape(depth, 2, db, n_pages, N_KV, 2, HEAD_DIM).transpose(0, 1, 2, 4, 3, 5, 6)
    paged = paged.reshape(depth, 2, db, N_KV, n_cmp_s, HEAD_DIM)

    xp = x_prompt.reshape(b * t, D_MODEL)
    xs = x_sample.reshape(db * dt, D_MODEL)
    outs = {k: [] for k in ("kv_p", "kv_s", "win_p", "win_s", "conv_p", "conv_s")}
    zero_prefix = jnp.zeros((b, CONV_HALO, D_CONV), F32)

    for l in range(depth):
        lw = layers[l]
        mods_p = _split_mod(mod_all[l, :b])
        mods_s = _split_mod(mod_all[l, b:b + db])

        (ug, gm, qn_t, qr_t, rows_t, win_tp, g3_t, cmp_in, ks, kw, vst, vwt) = _inproj(
            xp, mods_p, lw, tabs_p, t, tm_p, True)
        y = _conv_branch(ug.reshape(b, t, D_CONV), zero_prefix, lw, 512)
        n_blk = t // CMP_BLOCK
        kvc = _compress(_flatten_blocks(cmp_in, b, n_blk), lw).reshape(2, b, N_KV, n_blk, HEAD_DIM)
        o_t = _attn_prompt(qn_t, qr_t, g3_t, kvc[0], kvc[1], ks, vst, kw, vwt, b, t)
        x1, h2, dw = _merge(xp, y.reshape(b * t, D_CONV), o_t, gm, mods_p, lw, shared, t, tm_p, True)
        xp = _moe(h2, dw, x1, mods_p, lw, t, tm_p)
        outs["kv_p"].append(rows_t.reshape(b, 4, N_KV, HEAD_DIM, t).transpose(0, 4, 1, 2, 3))
        outs["win_p"].append(win_tp.reshape(b, 2, N_KV, HEAD_DIM, t)[..., t - win_buf:].transpose(0, 4, 1, 2, 3))
        outs["conv_p"].append(ug.reshape(b, t, D_CONV)[:, t - (CONV_WIDTH - 1):])

        ug, gm, qn, qr, rows_new, win_new, g3 = _inproj(xs, mods_s, lw, tabs_s, dt, tm_s, False)
        ug3 = ug.reshape(db, dt, D_CONV)
        conv_ext = jnp.concatenate([state_conv[l], ug3], axis=1)
        prefix = jnp.pad(state_conv[l], ((0, 0), (CONV_HALO - (CONV_WIDTH - 1), 0), (0, 0)))
        y = _conv_branch(ug3, prefix, lw, dt)
        r5 = rows_new.reshape(db, dt, 4, N_KV, HEAD_DIM)
        w5 = win_new.reshape(db, dt, 2, N_KV, HEAD_DIM)
        per_group = lambda a: a.transpose(0, 2, 1, 3)
        q5 = lambda a: a.reshape(db, dt, N_KV, Q_PER_KV, HEAD_DIM)
        q_rows = q5(qr).transpose(0, 2, 3, 1, 4).reshape(db, N_KV, lq_s, HEAD_DIM)
        qn_cols = q5(qn).transpose(0, 2, 4, 3, 1).reshape(db, N_KV, HEAD_DIM, lq_s)
        g3_rows = g3[:, :3 * N_HEADS].reshape(db, dt, 3, N_KV, Q_PER_KV).transpose(0, 3, 4, 1, 2)
        g3_rows = jnp.pad(g3_rows.reshape(db, N_KV, lq_s, 3), ((0, 0), (0, 0), (0, 0), (0, 5)))
        o_s = _attn_sample(page_table, cache_t, l, q_rows, qn_cols, paged[l, 0], paged[l, 1],
                           per_group(r5[:, :, 2]), per_group(r5[:, :, 3]), per_group(w5[:, :, 0]),
                           per_group(w5[:, :, 1]), win_t, g3_rows, expand, rsum, past_len, dt)
        o = o_s.reshape(db, N_KV, Q_PER_KV, dt, HEAD_DIM).transpose(0, 3, 1, 2, 4).reshape(db * dt, 512).astype(BF16)
        x1, h2, dw = _merge(xs, y.reshape(db * dt, D_CONV), o, gm, mods_s, lw, shared, dt, tm_s, False)
        xs = _moe(h2, dw, x1, mods_s, lw, dt, tm_s)
        outs["kv_s"].append(r5)
        outs["win_s"].append(jnp.concatenate([state_win_kv[l], w5], axis=1)[:, dt:])
        outs["conv_s"].append(conv_ext[:, dt:])

    return (xp.reshape(b, t, D_MODEL), xs.reshape(db, dt, D_MODEL),
            jnp.stack(outs["kv_p"]), jnp.stack(outs["kv_s"]), jnp.stack(outs["win_p"]), jnp.stack(outs["win_s"]),
            jnp.stack(outs["conv_p"]), jnp.stack(outs["conv_s"]))
```

```python
import functools

import jax
import jax.numpy as jnp
from jax import lax
from jax.experimental import pallas as pl
from jax.experimental.pallas import tpu as pltpu

F32 = jnp.float32
BF16 = jnp.bfloat16

D_MODEL = 1024
D_CONV = 512
CONV_WIDTH = 31
CONV_HALO = 32
CONV_PAD = 16
N_HEADS = 8
HEAD_DIM = 64
N_KV = 2
Q_PER_KV = 4
CMP_BLOCK = 64
N_SELECT = 16
WINDOW = 512
N_EXPERTS = 16
N_GROUPS = 4
EXPERTS_PER_GROUP = 4
D_EXPERT = 256
PAGE_SIZE = 128
ROPE_THETA = 10000.0
ATTN_SCALE = HEAD_DIM ** -0.5
FORCED_SCORE = 1e4
EPS = 1e-6
NEG = -1e30
LANES = 128
KEY_CHUNK = 256
GATHER_PAGES = 8
ATTN_PAGES = 16
MOE_EXPERTS_PER_STEP = 4
LOG2E = 1.4426950408889634
VMEM_LIMIT = 56 * 1024 * 1024


def _params(*sem):
    return pltpu.CompilerParams(dimension_semantics=sem, vmem_limit_bytes=VMEM_LIMIT)


def _dot(a, b):
    return jnp.dot(a, b, preferred_element_type=F32)


def _dot_t0(a, b):
    return lax.dot_general(a, b, (((0,), (0,)), ((), ())), preferred_element_type=F32)


def _dot_t1(a, b):
    return lax.dot_general(a, b, (((1,), (1,)), ((), ())), preferred_element_type=F32)


def _shift(d):
    assert d & (d - 1) == 0
    return d.bit_length() - 1


def _div(x, d):
    return lax.shift_right_logical(x, jnp.int32(_shift(d)))


def _mod(x, d):
    return x & jnp.int32(d - 1)


def _sigmoid(x):
    return 1.0 / (1.0 + jnp.exp(-x))


def _silu(x):
    return x * _sigmoid(x)


def _rms(x):
    return x * lax.rsqrt(jnp.mean(x * x, axis=-1, keepdims=True) + EPS)


def _split_dot(x, w):
    hi = x.astype(BF16)
    lo = (x - hi.astype(F32)).astype(BF16)
    return _dot(hi, w) + _dot(lo, w)


def _head_rms(x, seg):
    ss = _split_dot(x * x, seg)
    return x * lax.rsqrt(ss * (1.0 / HEAD_DIM) + EPS)


def _rope(x, cos, sin_signed):
    w = x.shape[-1]
    lane = lax.broadcasted_iota(jnp.int32, x.shape, 1)
    first_half = _mod(lane, HEAD_DIM) < (HEAD_DIM // 2)
    partner = jnp.where(first_half, pltpu.roll(x, w - HEAD_DIM // 2, 1), pltpu.roll(x, HEAD_DIM // 2, 1))
    return x * cos + partner * sin_signed


def _topk_mask(imp, k):
    nb = imp.shape[0]
    blk = lax.broadcasted_iota(jnp.int32, imp.shape, 0).astype(F32)
    sel = jnp.zeros(imp.shape, F32)
    for _ in range(k):
        mx = jnp.max(imp, axis=0, keepdims=True)
        first = jnp.min(jnp.where(imp == mx, blk, float(nb)), axis=0, keepdims=True)
        hit = blk == first
        sel = jnp.where(hit, 1.0, sel)
        imp = jnp.where(hit, -jnp.inf, imp)
    return sel


def _block_importance(imp, blk, cur):
    return jnp.where((blk == cur) | (blk == 0), FORCED_SCORE, jnp.where(blk > cur, -1.0, imp))


def _ada_kernel(c_ref, w_ref, b_ref, o_ref):
    o_ref[...] = _dot(_silu(c_ref[...]).astype(BF16), w_ref[...]) + b_ref[...]


def _ada_all_layers(c_all, w_ada, b_ada):
    depth, rows = w_ada.shape[0], c_all.shape[0]
    tn = 1536
    return pl.pallas_call(
        _ada_kernel,
        grid=(depth, 6 * D_MODEL // tn),
        in_specs=[pl.BlockSpec((rows, D_MODEL), lambda l, j: (0, 0)),
                  pl.BlockSpec((None, D_MODEL, tn), lambda l, j: (l, 0, j)),
                  pl.BlockSpec((None, 1, tn), lambda l, j: (l, 0, j))],
        out_specs=pl.BlockSpec((None, rows, tn), lambda l, j: (l, 0, j)),
        out_shape=jax.ShapeDtypeStruct((depth, rows, 6 * D_MODEL), F32),
        compiler_params=_params("parallel", "parallel"),
        name="ada",
    )(c_all, w_ada, b_ada)


def _mod_operand(mod, rows_per_batch, tm):
    if rows_per_batch % tm == 0:
        per = rows_per_batch // tm
        return mod[:, None, :], pl.BlockSpec((None, 1, D_MODEL), lambda i: (i // per, 0, 0))
    return jnp.repeat(mod, rows_per_batch, axis=0), pl.BlockSpec((tm, D_MODEL), lambda i: (i, 0))


def _const_spec(shape):
    nd = len(shape)
    return pl.BlockSpec(shape, lambda i: (0,) * nd)


def _inproj_kernel(x_ref, gn_ref, sc_ref, sh_ref, wu_ref, wq_ref, wkv_ref, wg_ref, wm_ref,
                   gq_ref, gks_ref, gkw_ref, cos_ref, sin_ref, seg_ref, pe_ref, ug_ref, gm_ref, *out, transposed):
    h = _rms(x_ref[...]) * gn_ref[...]
    h = h * (1.0 + sc_ref[...]) + sh_ref[...]
    hb = h.astype(BF16)

    u = _dot(hb, wu_ref[...])
    ug_ref[...] = u[:, :D_CONV] * _sigmoid(u[:, D_CONV:])
    gm_ref[...] = _sigmoid(_dot(hb, wm_ref[...])).astype(BF16)
    g3 = _sigmoid(_dot(hb, wg_ref[...]))

    cos1, sin1 = cos_ref[...], sin_ref[...]
    cos4 = jnp.concatenate([cos1] * 4, axis=1)
    sin4 = jnp.concatenate([sin1] * 4, axis=1)
    qn = _head_rms(_dot(hb, wq_ref[...]), seg_ref[...]) * gq_ref[...]
    qr = _rope(qn, cos4, sin4) * (ATTN_SCALE * LOG2E)
    qn = qn * (ATTN_SCALE * LOG2E)

    kv = _dot(hb, wkv_ref[...])
    seg1 = seg_ref[:LANES, :LANES]
    ks = _rope(_head_rms(kv[:, 2 * LANES:3 * LANES], seg1) * gks_ref[...], cos1, sin1)
    kw = _rope(_head_rms(kv[:, 4 * LANES:5 * LANES], seg1) * gkw_ref[...], cos1, sin1)
    vs = kv[:, 3 * LANES:4 * LANES]
    vw = kv[:, 5 * LANES:]

    if not transposed:
        qn_ref, qr_ref, rows_ref, win_ref, g3_ref = out
        qn_ref[...] = qn.astype(BF16)
        qr_ref[...] = qr.astype(BF16)
        rows_ref[:, :2 * LANES] = kv[:, :2 * LANES]
        rows_ref[:, 2 * LANES:3 * LANES] = ks
        rows_ref[:, 3 * LANES:] = vs
        win_ref[:, :LANES] = kw
        win_ref[:, LANES:] = vw
        g3_ref[...] = g3
        return

    qnt_ref, qrt_ref, rowst_ref, wint_ref, g3t_ref, cmp_ref, ks_ref, kw_ref, vst_ref, vwt_ref = out
    qnt_ref[...] = qn.T.astype(BF16)
    qrt_ref[...] = qr.T.astype(BF16)
    rowst_ref[:2 * LANES, :] = kv[:, :2 * LANES].T
    rowst_ref[2 * LANES:3 * LANES, :] = ks.T
    vst = vs.T
    rowst_ref[3 * LANES:, :] = vst
    wint_ref[:LANES, :] = kw.T
    vwt = vw.T
    wint_ref[LANES:, :] = vwt
    g3t_ref[...] = g3.T[:3 * N_HEADS, :]
    cmp_ref[...] = (kv[:, :2 * LANES] + pe_ref[...]).astype(BF16)
    ks_ref[...] = ks.astype(BF16)
    kw_ref[...] = kw.astype(BF16)
    vst_ref[...] = vst.astype(BF16)
    vwt_ref[...] = vwt.astype(BF16)


def _inproj(x, mods, lw, tabs, rows_per_batch, tm, transposed):
    n = x.shape[0]
    sc, sc_spec = _mod_operand(mods["sc1"], rows_per_batch, tm)
    sh, sh_spec = _mod_operand(mods["sh1"], rows_per_batch, tm)
    cos, sin, tab_tiles = tabs["cos"], tabs["sin"], tabs["cos"].shape[0] // tm
    tab_spec = pl.BlockSpec((tm, LANES), lambda i: (i % tab_tiles, 0))
    row = lambda w: pl.BlockSpec((tm, w), lambda i: (i, 0))
    col = lambda h: pl.BlockSpec((h, tm), lambda i: (0, i))
    sds = jax.ShapeDtypeStruct
    out_shape = [sds((n, D_CONV), F32), sds((n, 2 * D_MODEL), BF16)]
    out_specs = [row(D_CONV), row(2 * D_MODEL)]
    if transposed:
        per = rows_per_batch // tm
        nb = n // rows_per_batch
        bcol = lambda h: pl.BlockSpec((None, h, tm), lambda i: (i // per, 0, i % per))
        out_shape += [sds((512, n), BF16), sds((512, n), BF16), sds((nb, 512, rows_per_batch), F32),
                      sds((nb, 256, rows_per_batch), F32), sds((3 * N_HEADS, n), F32), sds((n, 256), BF16),
                      sds((n, LANES), BF16), sds((n, LANES), BF16), sds((LANES, n), BF16), sds((LANES, n), BF16)]
        out_specs += [col(512), col(512), bcol(512), bcol(256), col(3 * N_HEADS), row(256),
                      row(LANES), row(LANES), col(LANES), col(LANES)]
    else:
        out_shape += [sds((n, 512), BF16), sds((n, 512), BF16), sds((n, 512), F32), sds((n, 256), F32),
                      sds((n, LANES), F32)]
        out_specs += [row(512), row(512), row(512), row(256), row(LANES)]
    return pl.pallas_call(
        functools.partial(_inproj_kernel, transposed=transposed),
        grid=(n // tm,),
        in_specs=[row(D_MODEL), _const_spec((1, D_MODEL)), sc_spec, sh_spec,
                  _const_spec(lw["w_u"].shape), _const_spec(lw["w_q"].shape), _const_spec(lw["w_kv"].shape),
                  _const_spec(lw["w_g"].shape), _const_spec(lw["w_m"].shape),
                  _const_spec((1, 512)), _const_spec((1, LANES)), _const_spec((1, LANES)),
                  tab_spec, tab_spec, _const_spec((512, 512)), _const_spec((tm, 256))],
        out_specs=out_specs,
        out_shape=out_shape,
        compiler_params=_params("parallel"),
        name="inproj",
    )(x, lw["g_norm_mix"], sc, sh, lw["w_u"], lw["w_q"], lw["w_kv"], lw["w_g"], lw["w_m"],
      lw["g_q"], lw["g_ks"], lw["g_kw"], cos, sin, tabs["seg"], lw["pe_tile"][:tm])


def _conv_kernel(*refs, tc, n_tiles):
    if n_tiles > 1:
        cur_ref, halo_ref, pre_ref, w_ref, b_ref, g_ref, beta_ref, y_ref, win_ref = refs
    else:
        cur_ref, pre_ref, w_ref, b_ref, g_ref, beta_ref, y_ref, win_ref = refs
    if n_tiles > 1:
        first = pl.program_id(1) == 0

        @pl.when(first)
        def _():
            win_ref[:CONV_HALO, :] = pre_ref[...]

        @pl.when(jnp.logical_not(first))
        def _():
            win_ref[:CONV_HALO, :] = halo_ref[...]
    else:
        win_ref[:CONV_HALO, :] = pre_ref[...]
    win_ref[CONV_HALO:CONV_HALO + tc, :] = cur_ref[...]
    win_ref[CONV_HALO + tc:, :] = jnp.zeros((CONV_PAD, D_CONV), F32)

    rc = min(tc, 128)
    off = CONV_HALO - (CONV_WIDTH - 1)
    ext = rc + CONV_PAD
    for r0 in range(0, tc, rc):
        acc = None
        for s in range(8):
            z = None
            for a in range(-(-CONV_WIDTH // 8)):
                j = 8 * a + s
                if j < CONV_WIDTH:
                    term = w_ref[j:j + 1, :] * win_ref[r0 + 8 * a:r0 + 8 * a + ext, :]
                    z = term if z is None else z + term
            piece = z[off + s:off + s + rc]
            acc = piece if acc is None else acc + piece
        y = acc + b_ref[...]
        mu = jnp.mean(y, axis=-1, keepdims=True)
        yc = y - mu
        var = jnp.mean(yc * yc, axis=-1, keepdims=True)
        y = yc * lax.rsqrt(var + EPS) * g_ref[...] + beta_ref[...]
        y_ref[r0:r0 + rc, :] = _silu(y).astype(BF16)


def _conv_branch(ug, prefix, lw, tc):
    b, t, _ = ug.shape
    n_tiles = t // tc
    cur_spec = pl.BlockSpec((None, tc, D_CONV), lambda bi, i: (bi, i, 0))
    pre_spec = pl.BlockSpec((None, CONV_HALO, D_CONV), lambda bi, i: (bi, 0, 0))
    vec = lambda r: pl.BlockSpec((r, D_CONV), lambda bi, i: (0, 0))
    ops, specs = [ug], [cur_spec]
    if n_tiles > 1:
        per = tc // CONV_HALO
        ops.append(ug)
        specs.append(pl.BlockSpec((None, CONV_HALO, D_CONV), lambda bi, i: (bi, jnp.maximum(i * per - 1, 0), 0)))
    ops += [prefix, lw["w_dw"], lw["b_dw"], lw["ln_conv_g"], lw["ln_conv_b"]]
    specs += [pre_spec, vec(CONV_HALO), vec(1), vec(1), vec(1)]
    return pl.pallas_call(
        functools.partial(_conv_kernel, tc=tc, n_tiles=n_tiles),
        grid=(b, n_tiles),
        in_specs=specs,
        out_specs=pl.BlockSpec((None, tc, D_CONV), lambda bi, i: (bi, i, 0)),
        out_shape=jax.ShapeDtypeStruct((b, t, D_CONV), BF16),
        scratch_shapes=[pltpu.VMEM((CONV_HALO + tc + CONV_PAD, D_CONV), F32)],
        compiler_params=_params("parallel", "parallel"),
        name="conv",
    )(*ops)


def _compress_kernel(x_ref, w1_ref, b1_ref, w2_ref, b2_ref, gkc_ref, o_ref):
    hid = jax.nn.gelu(_dot(x_ref[...], w1_ref[...]) + b1_ref[...])
    o = _dot(hid.astype(BF16), w2_ref[...]) + b2_ref[...]
    is_key = pl.program_id(0) == 0
    o_ref[...] = jnp.where(is_key, _rms(o) * gkc_ref[...], o)


def _compress(flat, lw):
    _, m, kdim = flat.shape
    tm = min(m, 512)
    hidden = lw["cmp_w1"].shape[-1]
    return pl.pallas_call(
        _compress_kernel,
        grid=(2, m // tm),
        in_specs=[pl.BlockSpec((None, tm, kdim), lambda s, i: (s, i, 0)),
                  pl.BlockSpec((None, kdim, hidden), lambda s, i: (s, 0, 0)),
                  pl.BlockSpec((None, 1, hidden), lambda s, i: (s, 0, 0)),
                  pl.BlockSpec((None, hidden, HEAD_DIM), lambda s, i: (s, 0, 0)),
                  pl.BlockSpec((None, 1, HEAD_DIM), lambda s, i: (s, 0, 0)),
                  pl.BlockSpec((1, HEAD_DIM), lambda s, i: (0, 0))],
        out_specs=pl.BlockSpec((None, tm, HEAD_DIM), lambda s, i: (s, i, 0)),
        out_shape=jax.ShapeDtypeStruct((2, m, HEAD_DIM), F32),
        compiler_params=_params("parallel", "parallel"),
        name="compress",
    )(flat, lw["cmp_w1"], lw["cmp_b1"], lw["cmp_w2"], lw["cmp_b2"], lw["g_kc"])


def _flatten_blocks(cmp_in, b, n_blk):
    x = cmp_in.reshape(b, n_blk, CMP_BLOCK, 2, N_KV, HEAD_DIM)
    x = x.transpose(3, 0, 4, 1, 2, 5)
    return x.reshape(2, b * N_KV * n_blk, CMP_BLOCK * HEAD_DIM)


def _gather_pages_kernel(pt_ref, *refs):
    del pt_ref
    pages, o_ref = refs[:GATHER_PAGES], refs[GATHER_PAGES]
    rows = N_KV * HEAD_DIM
    for k, pg in enumerate(pages):
        for s in range(2):
            o_ref[s, k * rows:(k + 1) * rows, :] = pg[s].reshape(rows, PAGE_SIZE)


def _gather_pages(page_table, cache_t):
    depth = cache_t.shape[0]
    db, n_pages = page_table.shape
    steps = n_pages // GATHER_PAGES
    rows = N_KV * HEAD_DIM

    def page_spec(k):
        return pl.BlockSpec((None, None, 2, N_KV, HEAD_DIM, PAGE_SIZE),
                            lambda l, b, j, pt: (l, pt[b, j * GATHER_PAGES + k], 0, 0, 0, 0))

    grid_spec = pltpu.PrefetchScalarGridSpec(
        num_scalar_prefetch=1,
        grid=(depth, db, steps),
        in_specs=[page_spec(k) for k in range(GATHER_PAGES)],
        out_specs=pl.BlockSpec((None, None, 2, GATHER_PAGES * rows, PAGE_SIZE), lambda l, b, j, pt: (l, b, 0, j, 0)))
    return pl.pallas_call(
        _gather_pages_kernel,
        grid_spec=grid_spec,
        out_shape=jax.ShapeDtypeStruct((depth, db, 2, n_pages * rows, PAGE_SIZE), F32),
        compiler_params=_params("parallel", "parallel", "parallel"),
        name="gather_pages",
    )(page_table, *([cache_t] * GATHER_PAGES))


def _compress_paged_kernel(x_ref, w_ref, pe_ref, b1_ref, w2_ref, b2_ref, gkc_ref, seg_ref, o_ref, *, n_rows):
    acc = jnp.zeros((n_rows, w_ref.shape[-1]), F32)
    for d in range(0, HEAD_DIM, 2):
        a = jnp.concatenate([(x_ref[pl.ds(d + u, n_rows, stride=HEAD_DIM), :] + pe_ref[d + u:d + u + 1, :]).astype(BF16)
                             for u in range(2)], axis=1)
        acc = acc + _dot(a, w_ref[d // 2])
    hid = jax.nn.gelu(acc + b1_ref[...])
    o = _dot(hid.astype(BF16), w2_ref[...]) + b2_ref[...]
    is_key = pl.program_id(1) == 0
    o_ref[...] = jnp.where(is_key, _head_rms(o, seg_ref[...]) * gkc_ref[...], o)


def _compress_paged(x, pw):
    depth, db, _, rows, _ = x.shape
    n_rows = rows // HEAD_DIM
    hid2 = pw["w1"].shape[-1]
    sel = lambda *s: pl.BlockSpec((None, None) + s, lambda l, sl, b: (l, sl) + (0,) * len(s))
    return pl.pallas_call(
        functools.partial(_compress_paged_kernel, n_rows=n_rows),
        grid=(depth, 2, db),
        in_specs=[pl.BlockSpec((None, None, None, rows, PAGE_SIZE), lambda l, sl, b: (l, b, sl, 0, 0)),
                  sel(HEAD_DIM // 2, 2 * PAGE_SIZE, hid2), sel(HEAD_DIM, PAGE_SIZE), sel(1, hid2), sel(hid2, LANES),
                  sel(1, LANES), pl.BlockSpec((None, 1, LANES), lambda l, sl, b: (l, 0, 0)),
                  pl.BlockSpec((LANES, LANES), lambda l, sl, b: (0, 0))],
        out_specs=pl.BlockSpec((None, None, None, n_rows, LANES), lambda l, sl, b: (l, sl, b, 0, 0)),
        out_shape=jax.ShapeDtypeStruct((depth, 2, db, n_rows, LANES), F32),
        compiler_params=_params("parallel", "parallel", "parallel"),
        name="compress_paged",
    )(x, pw["w1"], pw["pe"], pw["b1"], pw["w2"], pw["b2"], pw["g_kc"], pw["seg"])


def _attn_prompt_kernel(qn_ref, qr_ref, g3_ref, kc_ref, vc_ref, ks_ref, vst_ref, kw_ref, vwt_ref,
                        o_ref, bias_ref, ocmp_ref, ms_ref, ls_ref, accs_ref, mw_ref, lw_ref, accw_ref, *, tq):
    i = pl.program_id(1)
    t0 = i * tq
    lq = Q_PER_KV * tq
    lane = lax.broadcasted_iota(jnp.int32, (1, lq), 1)
    qpos = t0 + _mod(lane, tq)
    nb = kc_ref.shape[1]
    blk = lax.broadcasted_iota(jnp.int32, (nb, 1), 0)
    zpad = jnp.zeros((HEAD_DIM, lq), BF16)

    def heads(ref, g):
        return jnp.concatenate([ref[(Q_PER_KV * g + r) * HEAD_DIM:(Q_PER_KV * g + r + 1) * HEAD_DIM, :]
                                for r in range(Q_PER_KV)], axis=1)

    qpad = []
    for g in range(N_KV):
        q4 = heads(qr_ref, g)
        qpad.append(jnp.concatenate([q4, zpad] if g == 0 else [zpad, q4], axis=0))

        cm = blk < _div(qpos + 1, CMP_BLOCK)
        s = jnp.where(cm, _dot(kc_ref[g].astype(BF16), heads(qn_ref, g)), NEG)
        p = jnp.exp2(s - jnp.max(s, axis=0, keepdims=True)) * cm.astype(F32)
        p = p / jnp.maximum(jnp.sum(p, axis=0, keepdims=True), 1e-30)
        ocmp_ref[g] = _dot_t0(vc_ref[g].astype(BF16), p.astype(BF16))

        imp = p[:, :tq]
        for r in range(1, Q_PER_KV):
            imp = imp + p[:, r * tq:(r + 1) * tq]
        imp = _block_importance(imp, blk, _div(qpos[:, :tq], CMP_BLOCK))
        bias_ref[g] = (_topk_mask(imp, N_SELECT) - 1.0) * (-NEG)

    for m_ref, l_ref, acc_ref in ((ms_ref, ls_ref, accs_ref), (mw_ref, lw_ref, accw_ref)):
        m_ref[...] = jnp.full(m_ref.shape, NEG, F32)
        l_ref[...] = jnp.zeros(l_ref.shape, F32)
        acc_ref[...] = jnp.zeros(acc_ref.shape, F32)

    kidx = lax.broadcasted_iota(jnp.int32, (KEY_CHUNK, 1), 0)
    row8 = lax.broadcasted_iota(jnp.int32, (8, 1), 0)
    blocks_per_chunk = KEY_CHUNK // CMP_BLOCK

    tpos = qpos[:, :tq]

    chains = [(g, slice(r * tq, (r + 1) * tq)) for g in range(N_KV) for r in range(Q_PER_KV)]

    def qk(k_ref, kc0):
        kblk = k_ref[pl.ds(kc0, KEY_CHUNK), :]
        return tuple(_dot(kblk, qpad[g][:, sl]) for g, sl in chains)

    def softmax_pv(scores, bias, vt_ref, kc0, m_ref, l_ref, acc_ref):
        m_all, l_all, acc_all = m_ref[...], l_ref[...], acc_ref[...]
        m_out, l_out, alphas, probs = [], [], [], []
        for (g, sl), s in zip(chains, scores):
            s = s + bias[g]
            m = m_all[g, :, sl]
            m_new = jnp.maximum(m, jnp.max(s, axis=0, keepdims=True))
            alpha = jnp.exp2(m - m_new)
            p = jnp.exp2(s - m_new)
            m_out.append(m_new)
            l_out.append(alpha * l_all[g, :, sl] + jnp.sum(p, axis=0, keepdims=True))
            alphas.append(alpha)
            probs.append(p.astype(BF16))
        vts = [vt_ref[g * HEAD_DIM:(g + 1) * HEAD_DIM, pl.ds(kc0, KEY_CHUNK)] for g in range(N_KV)]
        acc_out = [alpha * acc_all[g, :, sl] + _dot(vts[g], p)
                   for (g, sl), alpha, p in zip(chains, alphas, probs)]
        gather = lambda parts: jnp.stack([jnp.concatenate(parts[g * Q_PER_KV:(g + 1) * Q_PER_KV], axis=1)
                                          for g in range(N_KV)])
        m_ref[...] = gather(m_out)
        l_ref[...] = gather(l_out)
        acc_ref[...] = gather(acc_out)

    def sel_start(c):
        return pl.multiple_of(c * KEY_CHUNK, KEY_CHUNK)

    def sel_bias(c, causal):
        first = c * blocks_per_chunk
        r0 = first % 8
        bias = []
        for g in range(N_KV):
            rows8 = bias_ref[g, pl.ds(pl.multiple_of((first // 8) * 8, 8), 8), :]
            b = jnp.concatenate(
                [jnp.broadcast_to(jnp.sum(jnp.where(row8 == r0 + j, rows8, 0.0), axis=0, keepdims=True),
                                  (CMP_BLOCK, tq)) for j in range(blocks_per_chunk)], axis=0)
            bias.append(jnp.where(sel_start(c) + kidx <= tpos, b, NEG) if causal else b)
        return bias

    def sel_chunk(c, causal):
        softmax_pv(qk(ks_ref, sel_start(c)), sel_bias(c, causal), vst_ref, sel_start(c), ms_ref, ls_ref, accs_ref)

    def sel_body(c, carry):
        sel_chunk(c, False)
        return carry

    lax.fori_loop(0, i, sel_body, 0)
    sel_chunk(i, True)

    n_win = WINDOW // KEY_CHUNK
    for c in range(n_win + 1):
        k0 = t0 - WINDOW + c * KEY_CHUNK
        start = pl.multiple_of(jnp.maximum(k0, 0), KEY_CHUNK)
        if c == n_win:
            bias_w = jnp.where(k0 + kidx <= tpos, 0.0, NEG)
        elif c == 0:
            bias_w = jnp.where(tpos - (k0 + kidx) < jnp.where(k0 >= 0, WINDOW, -(2 ** 30)), 0.0, NEG)
        else:
            bias_w = jnp.broadcast_to(jnp.where(k0 >= 0, 0.0, NEG), (1, 1))
        softmax_pv(qk(kw_ref, start), [bias_w] * N_KV, vwt_ref, start, mw_ref, lw_ref, accw_ref)

    for g in range(N_KV):
        gate = lambda br: jnp.concatenate(
            [g3_ref[br * N_HEADS + Q_PER_KV * g + r:br * N_HEADS + Q_PER_KV * g + r + 1, :]
             for r in range(Q_PER_KV)], axis=1)
        o = (gate(0) * ocmp_ref[g] + gate(1) * (accs_ref[g] / ls_ref[g]) + gate(2) * (accw_ref[g] / lw_ref[g]))
        for r in range(Q_PER_KV):
            o_ref[(Q_PER_KV * g + r) * HEAD_DIM:(Q_PER_KV * g + r + 1) * HEAD_DIM, :] = (
                o[:, r * tq:(r + 1) * tq].astype(BF16))


def _attn_prompt(qn_t, qr_t, g3_t, kc, vc, ks, vst, kw, vwt, b, t):
    tq = KEY_CHUNK
    nq = t // tq
    lq = Q_PER_KV * tq
    n_cmp = kc.shape[2]
    qcol = lambda h: pl.BlockSpec((h, tq), lambda bi, i: (0, bi * nq + i))
    per_b = pl.BlockSpec((None, N_KV, n_cmp, HEAD_DIM), lambda bi, i: (bi, 0, 0, 0))
    keys = pl.BlockSpec((t, LANES), lambda bi, i: (bi, 0))
    vals = pl.BlockSpec((LANES, t), lambda bi, i: (0, bi))
    st = lambda r: pltpu.VMEM((N_KV, r, lq), F32)
    return pl.pallas_call(
        functools.partial(_attn_prompt_kernel, tq=tq),
        grid=(b, nq),
        in_specs=[qcol(512), qcol(512), qcol(3 * N_HEADS), per_b, per_b, keys, vals, keys, vals],
        out_specs=qcol(512),
        out_shape=jax.ShapeDtypeStruct((512, b * t), BF16),
        scratch_shapes=[pltpu.VMEM((N_KV, n_cmp, tq), F32), st(HEAD_DIM),
                        st(1), st(1), st(HEAD_DIM), st(1), st(1), st(HEAD_DIM)],
        compiler_params=_params("parallel", "arbitrary"),
        name="attn_prompt",
    )(qn_t, qr_t, g3_t, kc, vc, ks, vst, kw, vwt)


def _attn_sample_kernel(pt_ref, *refs, n_pages, n_blk, pos0, tq):
    pages = refs[:ATTN_PAGES]
    (q_ref, qnt_ref, kc_ref, vc_ref, kn_ref, vn_ref, kwn_ref, vwn_ref, win_ref, g3_ref, expand_ref, rsum_ref,
     o_ref, sel_ref, m_ref, l_ref, acc_ref, ocmp_ref) = refs[ATTN_PAGES:]
    del pt_ref
    j = pl.program_id(1)
    lq = Q_PER_KV * tq
    nbp = sel_ref.shape[1]
    n_cmp = kc_ref.shape[1]
    step_keys = ATTN_PAGES * PAGE_SIZE

    @pl.when(j == 0)
    def _():
        lane = lax.broadcasted_iota(jnp.int32, (1, lq), 1)
        qpos = pos0 + _mod(lane, tq)
        blk = lax.broadcasted_iota(jnp.int32, (nbp, 1), 0)
        for g in range(N_KV):
            cm = blk[:n_cmp] < _div(qpos + 1, CMP_BLOCK)
            s = jnp.where(cm, _dot(kc_ref[g].astype(BF16), qnt_ref[g]), NEG)
            p = jnp.exp2(s - jnp.max(s, axis=0, keepdims=True)) * cm.astype(F32)
            p = p / jnp.maximum(jnp.sum(p, axis=0, keepdims=True), 1e-30)
            ocmp_ref[g] = _dot_t0(p.astype(BF16), vc_ref[g].astype(BF16))
            imp = _split_dot(p, rsum_ref[...])
            imp = jnp.concatenate([imp, jnp.zeros((nbp - n_cmp, lq), F32)], axis=0)
            imp = _block_importance(imp, blk, _div(qpos, CMP_BLOCK))
            imp = jnp.where(blk >= n_blk, -jnp.inf, imp)
            sel_ref[g] = _topk_mask(imp, N_SELECT)
        m_ref[...] = jnp.full(m_ref.shape, NEG, F32)
        l_ref[...] = jnp.zeros(l_ref.shape, F32)
        acc_ref[...] = jnp.zeros(acc_ref.shape, F32)

    def update(g, s, pv):
        m = m_ref[g]
        m_new = jnp.maximum(m, jnp.max(s, axis=1, keepdims=True))
        alpha = jnp.exp2(m - m_new)
        p = jnp.exp2(s - m_new)
        l_ref[g] = alpha * l_ref[g] + jnp.sum(p, axis=1, keepdims=True)
        acc_ref[g] = alpha * acc_ref[g] + pv(p)
        m_ref[g] = m_new

    expand = expand_ref[:, pl.ds(pl.multiple_of(j * step_keys, step_keys), step_keys)]
    scores = []
    for g in range(N_KV):
        kt = jnp.concatenate([pg[0, g].astype(BF16) for pg in pages], axis=1)
        picked = _dot_t0(sel_ref[g, :n_cmp, :].astype(BF16), expand)
        scores.append(_dot(q_ref[g], kt) + (picked - 1.0) * (-NEG))
    m_all, l_all, acc_all = m_ref[...], l_ref[...], acc_ref[...]
    m_out, l_out, alphas, probs = [], [], [], []
    for g, s in enumerate(scores):
        m_new = jnp.maximum(m_all[g], jnp.max(s, axis=1, keepdims=True))
        alpha = jnp.exp2(m_all[g] - m_new)
        p = jnp.exp2(s - m_new)
        m_out.append(m_new)
        l_out.append(alpha * l_all[g] + jnp.sum(p, axis=1, keepdims=True))
        alphas.append(alpha)
        probs.append(p.astype(BF16))
    acc_out = []
    for g in range(N_KV):
        vt = jnp.concatenate([pg[1, g].astype(BF16) for pg in pages], axis=1)
        acc_out.append(alphas[g] * acc_all[g] + _dot_t1(probs[g], vt))
    m_ref[...] = jnp.stack(m_out)
    l_ref[...] = jnp.stack(l_out)
    acc_ref[...] = jnp.stack(acc_out)

    @pl.when(j == n_pages // ATTN_PAGES - 1)
    def _():
        rowi = _mod(lax.broadcasted_iota(jnp.int32, (lq, 1), 0), tq)
        keyi = lax.broadcasted_iota(jnp.int32, (1, tq), 1)
        causal = keyi <= rowi
        wlane = lax.broadcasted_iota(jnp.int32, (1, WINDOW), 1)
        kpos = pos0 - WINDOW + wlane
        in_win = ((pos0 + rowi) - kpos < WINDOW) & (kpos >= 0)
        cur_blk = pos0 // CMP_BLOCK
        first_row = (lax.broadcasted_iota(jnp.int32, (8, tq), 0) == cur_blk % 8).astype(F32)
        for g in range(N_KV):
            q = q_ref[g]
            qf = q.astype(F32)
            base = (cur_blk // 8) * 8
            picked = _dot_t0(sel_ref[g, base:base + 8, :], first_row)
            s = jnp.where(causal & (picked > 0.5), _dot_t1(qf, kn_ref[g]), NEG)
            update(g, s, lambda p: _dot(p, vn_ref[g]))
            o_sel = acc_ref[g] / l_ref[g]

            s_pre = jnp.where(in_win, _dot(q, win_ref[0, g].astype(BF16)), NEG)
            s_new = jnp.where(causal, _dot_t1(qf, kwn_ref[g]), NEG)
            m = jnp.maximum(jnp.max(s_pre, axis=1, keepdims=True), jnp.max(s_new, axis=1, keepdims=True))
            p_pre = jnp.exp2(s_pre - m) * in_win.astype(F32)
            p_new = jnp.exp2(s_new - m) * causal.astype(F32)
            l = jnp.sum(p_pre, axis=1, keepdims=True) + jnp.sum(p_new, axis=1, keepdims=True)
            o_win = (_dot_t1(p_pre.astype(BF16), win_ref[1, g].astype(BF16)) + _dot(p_new, vwn_ref[g]))
            o_win = o_win / jnp.maximum(l, 1e-30)

            g3 = g3_ref[g]
            o_ref[g] = g3[:, 0:1] * ocmp_ref[g] + g3[:, 1:2] * o_sel + g3[:, 2:3] * o_win


def _attn_sample(page_table, cache_t, layer, q, qn_t, kc, vc, kn, vn, kwn, vwn, win_t, g3, expand, rsum, pos0, tq):
    db, n_pages = page_table.shape
    lq = Q_PER_KV * tq
    n_cmp = kc.shape[2]
    n_blk = -(-(pos0 + tq) // CMP_BLOCK)
    nbp = -(-n_blk // 8) * 8
    steps = n_pages // ATTN_PAGES

    def page_spec(k):
        return pl.BlockSpec((None, None, 2, N_KV, HEAD_DIM, PAGE_SIZE),
                            lambda b, j, pt: (layer, pt[b, j * ATTN_PAGES + k], 1, 0, 0, 0))

    per_b = lambda *s: pl.BlockSpec((None,) + s, lambda b, j, pt: (b,) + (0,) * len(s))
    win_spec = pl.BlockSpec((None, None, 2, N_KV, HEAD_DIM, WINDOW), lambda b, j, pt: (layer, b, 0, 0, 0, 0))
    const = lambda *s: pl.BlockSpec(s, lambda b, j, pt: (0,) * len(s))
    grid_spec = pltpu.PrefetchScalarGridSpec(
        num_scalar_prefetch=1,
        grid=(db, steps),
        in_specs=[page_spec(k) for k in range(ATTN_PAGES)] + [
            per_b(N_KV, lq, HEAD_DIM), per_b(N_KV, HEAD_DIM, lq), per_b(N_KV, n_cmp, HEAD_DIM),
            per_b(N_KV, n_cmp, HEAD_DIM), per_b(N_KV, tq, HEAD_DIM), per_b(N_KV, tq, HEAD_DIM),
            per_b(N_KV, tq, HEAD_DIM), per_b(N_KV, tq, HEAD_DIM), win_spec,
            per_b(N_KV, lq, 8), const(n_cmp, n_pages * PAGE_SIZE), const(lq, lq)],
        out_specs=per_b(N_KV, lq, HEAD_DIM),
        scratch_shapes=[pltpu.VMEM((N_KV, nbp, lq), F32), pltpu.VMEM((N_KV, lq, 1), F32),
                        pltpu.VMEM((N_KV, lq, 1), F32), pltpu.VMEM((N_KV, lq, HEAD_DIM), F32),
                        pltpu.VMEM((N_KV, lq, HEAD_DIM), F32)])
    return pl.pallas_call(
        functools.partial(_attn_sample_kernel, n_pages=n_pages, n_blk=n_blk, pos0=pos0, tq=tq),
        grid_spec=grid_spec,
        out_shape=jax.ShapeDtypeStruct((db, N_KV, lq, HEAD_DIM), F32),
        compiler_params=_params("parallel", "arbitrary"),
        name="attn_sample",
    )(page_table, *([cache_t] * ATTN_PAGES), q, qn_t, kc, vc, kn, vn, kwn, vwn, win_t, g3, expand, rsum)


def _merge_kernel(x_ref, y_ref, o_ref, gm_ref, gt_ref, sc_ref, sh_ref, gn_ref, wc_ref, wa_ref, wo_ref,
                  wr_ref, br_ref, x1_ref, h2_ref, dw_ref, *, o_transposed):
    gm = gm_ref[...]
    attn = _dot_t0(o_ref[...], wa_ref[...]) if o_transposed else _dot(o_ref[...], wa_ref[...])
    merged = gm[:, :D_MODEL].astype(F32) * _dot(y_ref[...], wc_ref[...]) + gm[:, D_MODEL:].astype(F32) * attn
    x1 = x_ref[...] + gt_ref[...] * _dot(merged.astype(BF16), wo_ref[...])
    x1_ref[...] = x1
    h2 = _rms(x1) * gn_ref[...]
    h2 = h2 * (1.0 + sc_ref[...]) + sh_ref[...]
    h2_ref[...] = h2.astype(BF16)

    w_hi = wr_ref[0]
    w_lo = wr_ref[1]
    h_hi = h2.astype(BF16)
    h_lo = (h2 - h_hi.astype(F32)).astype(BF16)
    aff = _sigmoid(_dot(h_hi, w_hi) + _dot(h_lo, w_hi) + _dot(h_hi, w_lo))
    score = aff + br_ref[...]
    e = lax.broadcasted_iota(jnp.int32, score.shape, 1)
    grp = _div(e, EXPERTS_PER_GROUP)
    e = e.astype(F32)
    big = float(LANES)

    def top2(j):
        vals = jnp.where(grp == j, score, -jnp.inf)
        m1 = jnp.max(vals, axis=-1, keepdims=True)
        i1 = jnp.min(jnp.where(vals == m1, e, big), axis=-1, keepdims=True)
        vals = jnp.where(e == i1, -jnp.inf, vals)
        m2 = jnp.max(vals, axis=-1, keepdims=True)
        i2 = jnp.min(jnp.where(vals == m2, e, big), axis=-1, keepdims=True)
        return m1 + m2, i1, i2

    best, ia, ib = top2(0)
    for j in range(1, N_GROUPS):
        gs, i1, i2 = top2(j)
        better = gs > best
        best = jnp.where(better, gs, best)
        ia = jnp.where(better, i1, ia)
        ib = jnp.where(better, i2, ib)
    hit_a = e == ia
    hit_b = e == ib
    aff_a = jnp.sum(jnp.where(hit_a, aff, 0.0), axis=-1, keepdims=True)
    aff_b = jnp.sum(jnp.where(hit_b, aff, 0.0), axis=-1, keepdims=True)
    tot = aff_a + aff_b
    dw_ref[...] = jnp.where(hit_a, aff_a / tot, 0.0) + jnp.where(hit_b, aff_b / tot, 0.0)


def _merge(x, y, o, gm, mods, lw, shared, rows_per_batch, tm, o_transposed):
    n = x.shape[0]
    gt, gt_spec = _mod_operand(mods["gt1"], rows_per_batch, tm)
    sc, sc_spec = _mod_operand(mods["sc2"], rows_per_batch, tm)
    sh, sh_spec = _mod_operand(mods["sh2"], rows_per_batch, tm)
    row = lambda w: pl.BlockSpec((tm, w), lambda i: (i, 0))
    o_spec = pl.BlockSpec((512, tm), lambda i: (0, i)) if o_transposed else row(512)
    return pl.pallas_call(
        functools.partial(_merge_kernel, o_transposed=o_transposed),
        grid=(n // tm,),
        in_specs=[row(D_MODEL), row(D_CONV), o_spec, row(2 * D_MODEL), gt_spec, sc_spec, sh_spec,
                  _const_spec((1, D_MODEL)), _const_spec((D_CONV, D_MODEL)), _const_spec((512, D_MODEL)),
                  _const_spec((D_MODEL, D_MODEL)), _const_spec((2, D_MODEL, LANES)), _const_spec((1, LANES))],
        out_specs=[row(D_MODEL), row(D_MODEL), row(LANES)],
        out_shape=[jax.ShapeDtypeStruct((n, D_MODEL), F32), jax.ShapeDtypeStruct((n, D_MODEL), BF16),
                   jax.ShapeDtypeStruct((n, LANES), F32)],
        compiler_params=_params("parallel"),
        name="merge",
    )(x, y, o, gm, gt, sc, sh, lw["g_norm_ffn"], lw["w_conv_out"], lw["w_attn_out"], lw["w_out"],
      shared["w_router"], shared["b_router"])


def _moe_kernel(h_ref, dw_ref, x_ref, gt_ref, wg_ref, wu_ref, wd_ref, o_ref, acc_ref):
    ex = pl.program_id(1)

    @pl.when(ex == 0)
    def _():
        acc_ref[...] = jnp.zeros(acc_ref.shape, F32)

    h = h_ref[...]
    dw = dw_ref[...]
    lane = lax.broadcasted_iota(jnp.int32, dw.shape, 1)
    part = None
    for j in range(MOE_EXPERTS_PER_STEP):
        w_col = jnp.sum(jnp.where(lane == ex * MOE_EXPERTS_PER_STEP + j, dw, 0.0), axis=-1, keepdims=True)
        hid = _silu(_dot(h, wg_ref[j])) * _dot(h, wu_ref[j]) * w_col
        out = _dot(hid.astype(BF16), wd_ref[j])
        part = out if part is None else part + out
    acc_ref[...] += part

    @pl.when(ex == N_EXPERTS // MOE_EXPERTS_PER_STEP - 1)
    def _():
        o_ref[...] = x_ref[...] + gt_ref[...] * acc_ref[...]


def _moe(h2, dw, x1, mods, lw, rows_per_batch, tm):
    n = h2.shape[0]
    gt, gt_spec = _mod_operand(mods["gt2"], rows_per_batch, tm)
    gt_spec = pl.BlockSpec(gt_spec.block_shape, lambda i, ex, f=gt_spec.index_map: f(i))
    row = lambda w: pl.BlockSpec((tm, w), lambda i, ex: (i, 0))
    return pl.pallas_call(
        _moe_kernel,
        grid=(n // tm, N_EXPERTS // MOE_EXPERTS_PER_STEP),
        in_specs=[row(D_MODEL), row(LANES), row(D_MODEL), gt_spec,
                  pl.BlockSpec((MOE_EXPERTS_PER_STEP, D_MODEL, D_EXPERT), lambda i, ex: (ex, 0, 0)),
                  pl.BlockSpec((MOE_EXPERTS_PER_STEP, D_MODEL, D_EXPERT), lambda i, ex: (ex, 0, 0)),
                  pl.BlockSpec((MOE_EXPERTS_PER_STEP, D_EXPERT, D_MODEL), lambda i, ex: (ex, 0, 0))],
        out_specs=row(D_MODEL),
        out_shape=jax.ShapeDtypeStruct((n, D_MODEL), F32),
        scratch_shapes=[pltpu.VMEM((tm, D_MODEL), F32)],
        compiler_params=_params("parallel", "arbitrary"),
        name="moe",
    )(h2, dw, x1, gt, lw["w_exp_gate"], lw["w_exp_up"], lw["w_exp_down"])


def _rope_tables(pos):
    half = HEAD_DIM // 2
    inv = ROPE_THETA ** (-jnp.arange(half, dtype=F32) / half)
    ang = pos.astype(F32)[:, None] * inv[None, :]
    cos, sin = jnp.cos(ang), jnp.sin(ang)
    cos = jnp.concatenate([cos, cos], axis=1)
    sin = jnp.concatenate([-sin, sin], axis=1)
    return jnp.concatenate([cos, cos], axis=1), jnp.concatenate([sin, sin], axis=1)


def _split_mod(mod):
    names = ("sh1", "sc1", "gt1", "sh2", "sc2", "gt2")
    return dict(zip(names, jnp.split(mod, 6, axis=-1)))


def _paged_compress_weights(cmp_pe, cmp_w1, cmp_b1, cmp_w2, cmp_b2, g_kc, seg):
    depth = cmp_w1.shape[0]
    hidden = cmp_w1.shape[-1]
    eye = jnp.eye(2, dtype=F32)
    w1 = cmp_w1.reshape(depth, 2, CMP_BLOCK, HEAD_DIM, hidden).transpose(0, 1, 3, 2, 4)
    w1 = jnp.einsum("lsdph,jk->lsdjpkh", w1, eye).reshape(depth, 2, HEAD_DIM // 2, 2 * PAGE_SIZE, 2 * hidden)
    w2 = jnp.einsum("lshd,jk->lsjhkd", cmp_w2, eye).reshape(depth, 2, 2 * hidden, LANES)
    pe = jnp.tile(cmp_pe.transpose(0, 1, 3, 2), (1, 1, 1, 2))
    return dict(w1=w1.astype(BF16), w2=w2.astype(BF16), pe=pe,
                b1=jnp.tile(cmp_b1, (1, 1, 2))[:, :, None, :], b2=jnp.tile(cmp_b2, (1, 1, 2))[:, :, None, :],
                g_kc=jnp.tile(g_kc, (1, 2))[:, None, :], seg=seg[:LANES, :LANES])


def kernel(x_prompt, x_sample, c_prompt, c_sample, cache_kv, state_win_kv, state_conv, page_table, w_ada, b_ada, g_norm_mix, g_norm_ffn, w_in, w_dw, b_dw, ln_conv_g, ln_conv_b, w_conv_out, g_q, g_kc, g_ks, g_kw, cmp_pe, cmp_w1, cmp_b1, cmp_w2, cmp_b2, w_attn_out, w_out, w_router, b_router, w_exp_gate, w_exp_up, w_exp_down):
    depth = w_in.shape[0]
    b, t, _ = x_prompt.shape
    db, dt, _ = x_sample.shape
    n_pages = page_table.shape[1]
    past_len = n_pages * PAGE_SIZE
    win_buf = state_win_kv.shape[2]
    tm_p, tm_s = 512, db * dt
    lq_s = Q_PER_KV * dt
    n_cmp_s = (past_len + dt) // CMP_BLOCK
    assert t % tm_p == 0 and t % KEY_CHUNK == 0 and win_buf == WINDOW and WINDOW % KEY_CHUNK == 0
    assert n_cmp_s * CMP_BLOCK == past_len and n_pages % ATTN_PAGES == 0 and n_pages % GATHER_PAGES == 0

    c0, c1, c2, c3 = 2 * D_CONV, 2 * D_CONV + 512, 2 * D_CONV + 512 + 768, 2 * D_CONV + 512 + 768 + 3 * N_HEADS
    w_in_b = w_in.astype(BF16)
    w_dw_p = jnp.pad(w_dw, ((0, 0), (0, CONV_HALO - CONV_WIDTH), (0, 0)))
    pe256 = cmp_pe[:, :, :, None, :].repeat(N_KV, axis=3).transpose(0, 2, 1, 3, 4).reshape(depth, CMP_BLOCK, 256)
    pe_tile = jnp.tile(pe256, (1, tm_p // CMP_BLOCK, 1))
    layers = []
    for l in range(depth):
        layers.append(dict(
            g_norm_mix=g_norm_mix[l][None], g_norm_ffn=g_norm_ffn[l][None],
            w_u=w_in_b[l][:, :c0], w_q=w_in_b[l][:, c0:c1], w_kv=w_in_b[l][:, c1:c2],
            w_g=jnp.pad(w_in_b[l][:, c2:c3], ((0, 0), (0, LANES - 3 * N_HEADS))), w_m=w_in_b[l][:, c3:],
            g_q=jnp.tile(g_q[l], N_HEADS)[None], g_ks=jnp.tile(g_ks[l], N_KV)[None], g_kw=jnp.tile(g_kw[l], N_KV)[None],
            g_kc=g_kc[l][None], pe_tile=pe_tile[l],
            w_dw=w_dw_p[l], b_dw=b_dw[l][None], ln_conv_g=ln_conv_g[l][None], ln_conv_b=ln_conv_b[l][None],
            w_conv_out=w_conv_out[l].astype(BF16), w_attn_out=w_attn_out[l].astype(BF16), w_out=w_out[l].astype(BF16),
            cmp_w1=cmp_w1[l].astype(BF16), cmp_b1=cmp_b1[l][:, None, :], cmp_w2=cmp_w2[l].astype(BF16),
            cmp_b2=cmp_b2[l][:, None, :],
            w_exp_gate=w_exp_gate[l].astype(BF16), w_exp_up=w_exp_up[l].astype(BF16),
            w_exp_down=w_exp_down[l].astype(BF16)))
    wr = jnp.pad(w_router, ((0, 0), (0, LANES - N_EXPERTS)))
    wr_hi = wr.astype(BF16)
    wr_lo = (wr - wr_hi.astype(F32)).astype(BF16)
    shared = dict(w_router=jnp.stack([wr_hi, wr_lo]),
                  b_router=jnp.pad(b_router, (0, LANES - N_EXPERTS))[None])
    seg = (jnp.arange(512)[:, None] // HEAD_DIM == jnp.arange(512)[None, :] // HEAD_DIM).astype(BF16)
    cos_p, sin_p = _rope_tables(jnp.arange(t, dtype=jnp.int32))
    cos_s, sin_s = _rope_tables(past_len + jnp.arange(dt, dtype=jnp.int32))
    tabs_p = dict(cos=cos_p, sin=sin_p, seg=seg)
    tabs_s = dict(cos=jnp.tile(cos_s, (db, 1)), sin=jnp.tile(sin_s, (db, 1)), seg=seg)
    rsum = (jnp.arange(lq_s)[:, None] % dt == jnp.arange(lq_s)[None, :] % dt).astype(BF16)
    expand = (jnp.arange(n_cmp_s)[:, None] == jnp.arange(past_len)[None, :] // CMP_BLOCK).astype(BF16)

    rows = b + db
    rows_p = -(-rows // 8) * 8
    c_all = jnp.pad(jnp.concatenate([c_prompt, c_sample], axis=0), ((0, rows_p - rows), (0, 0)))
    mod_all = _ada_all_layers(c_all, w_ada.astype(BF16), b_ada[:, None, :])

    cache_t = cache_kv.transpose(0, 1, 3, 4, 5, 2)
    win_t = state_win_kv.transpose(0, 1, 3, 4, 5, 2)

    paged = _compress_paged(_gather_pages(page_table, cache_t),
                            _paged_compress_weights(cmp_pe, cmp_w1, cmp_b1, cmp_w2, cmp_b2, g_kc, seg))
    paged = paged.reshape(depth, 2, db, n_pages, N_KV, 2, HEAD_DIM).transpose(0, 1, 2, 4, 3, 5, 6)
    paged = paged.reshape(depth, 2, db, N_KV, n_cmp_s, HEAD_DIM)

    xp = x_prompt.reshape(b * t, D_MODEL)
    xs = x_sample.reshape(db * dt, D_MODEL)
    outs = {k: [] for k in ("kv_p", "kv_s", "win_p", "win_s", "conv_p", "conv_s")}
    zero_prefix = jnp.zeros((b, CONV_HALO, D_CONV), F32)

    for l in range(depth):
        lw = layers[l]
        mods_p = _split_mod(mod_all[l, :b])
        mods_s = _split_mod(mod_all[l, b:b + db])

        (ug, gm, qn_t, qr_t, rows_t, win_tp, g3_t, cmp_in, ks, kw, vst, vwt) = _inproj(
            xp, mods_p, lw, tabs_p, t, tm_p, True)
        y = _conv_branch(ug.reshape(b, t, D_CONV), zero_prefix, lw, 512)
        n_blk = t // CMP_BLOCK
        kvc = _compress(_flatten_blocks(cmp_in, b, n_blk), lw).reshape(2, b, N_KV, n_blk, HEAD_DIM)
        o_t = _attn_prompt(qn_t, qr_t, g3_t, kvc[0], kvc[1], ks, vst, kw, vwt, b, t)
        x1, h2, dw = _merge(xp, y.reshape(b * t, D_CONV), o_t, gm, mods_p, lw, shared, t, tm_p, True)
        xp = _moe(h2, dw, x1, mods_p, lw, t, tm_p)
        outs["kv_p"].append(rows_t.reshape(b, 4, N_KV, HEAD_DIM, t).transpose(0, 4, 1, 2, 3))
        outs["win_p"].append(win_tp.reshape(b, 2, N_KV, HEAD_DIM, t)[..., t - win_buf:].transpose(0, 4, 1, 2, 3))
        outs["conv_p"].append(ug.reshape(b, t, D_CONV)[:, t - (CONV_WIDTH - 1):])

        ug, gm, qn, qr, rows_new, win_new, g3 = _inproj(xs, mods_s, lw, tabs_s, dt, tm_s, False)
        ug3 = ug.reshape(db, dt, D_CONV)
        conv_ext = jnp.concatenate([state_conv[l], ug3], axis=1)
        prefix = jnp.pad(state_conv[l], ((0, 0), (CONV_HALO - (CONV_WIDTH - 1), 0), (0, 0)))
        y = _conv_branch(ug3, prefix, lw, dt)
        r5 = rows_new.reshape(db, dt, 4, N_KV, HEAD_DIM)
        w5 = win_new.reshape(db, dt, 2, N_KV, HEAD_DIM)
        per_group = lambda a: a.transpose(0, 2, 1, 3)
        q5 = lambda a: a.reshape(db, dt, N_KV, Q_PER_KV, HEAD_DIM)
        q_rows = q5(qr).transpose(0, 2, 3, 1, 4).reshape(db, N_KV, lq_s, HEAD_DIM)
        qn_cols = q5(qn).transpose(0, 2, 4, 3, 1).reshape(db, N_KV, HEAD_DIM, lq_s)
        g3_rows = g3[:, :3 * N_HEADS].reshape(db, dt, 3, N_KV, Q_PER_KV).transpose(0, 3, 4, 1, 2)
        g3_rows = jnp.pad(g3_rows.reshape(db, N_KV, lq_s, 3), ((0, 0), (0, 0), (0, 0), (0, 5)))
        o_s = _attn_sample(page_table, cache_t, l, q_rows, qn_cols, paged[l, 0], paged[l, 1],
                           per_group(r5[:, :, 2]), per_group(r5[:, :, 3]), per_group(w5[:, :, 0]),
                           per_group(w5[:, :, 1]), win_t, g3_rows, expand, rsum, past_len, dt)
        o = o_s.reshape(db, N_KV, Q_PER_KV, dt, HEAD_DIM).transpose(0, 3, 1, 2, 4).reshape(db * dt, 512).astype(BF16)
        x1, h2, dw = _merge(xs, y.reshape(db * dt, D_CONV), o, gm, mods_s, lw, shared, dt, tm_s, False)
        xs = _moe(h2, dw, x1, mods_s, lw, dt, tm_s)
        outs["kv_s"].append(r5)
        outs["win_s"].append(jnp.concatenate([state_win_kv[l], w5], axis=1)[:, dt:])
        outs["conv_s"].append(conv_ext[:, dt:])

    return (xp.reshape(b, t, D_MODEL), xs.reshape(db, dt, D_MODEL),
            jnp.stack(outs["kv_p"]), jnp.stack(outs["kv_s"]), jnp.stack(outs["win_p"]), jnp.stack(outs["win_s"]),
            jnp.stack(outs["conv_p"]), jnp.stack(outs["conv_s"]))
```

```python
import functools

import jax
import jax.numpy as jnp
from jax import lax
from jax.experimental import pallas as pl
from jax.experimental.pallas import tpu as pltpu

F32 = jnp.float32
BF16 = jnp.bfloat16

D_MODEL = 1024
D_CONV = 512
CONV_WIDTH = 31
CONV_HALO = 32
CONV_PAD = 16
N_HEADS = 8
HEAD_DIM = 64
N_KV = 2
Q_PER_KV = 4
CMP_BLOCK = 64
N_SELECT = 16
WINDOW = 512
N_EXPERTS = 16
N_GROUPS = 4
EXPERTS_PER_GROUP = 4
D_EXPERT = 256
PAGE_SIZE = 128
ROPE_THETA = 10000.0
ATTN_SCALE = HEAD_DIM ** -0.5
FORCED_SCORE = 1e4
EPS = 1e-6
NEG = -1e30
LANES = 128
KEY_CHUNK = 256
SEL_CHUNK = 512
ATTN_PAGES = 16
MOE_EXPERTS_PER_STEP = 4
LOG2E = 1.4426950408889634
VMEM_LIMIT = 56 * 1024 * 1024


def _params(*sem):
    return pltpu.CompilerParams(dimension_semantics=sem, vmem_limit_bytes=VMEM_LIMIT)


def _dot(a, b):
    return jnp.dot(a, b, preferred_element_type=F32)


def _dot_t0(a, b):
    return lax.dot_general(a, b, (((0,), (0,)), ((), ())), preferred_element_type=F32)


def _dot_t1(a, b):
    return lax.dot_general(a, b, (((1,), (1,)), ((), ())), preferred_element_type=F32)


def _shift(d):
    assert d & (d - 1) == 0
    return d.bit_length() - 1


def _div(x, d):
    return lax.shift_right_logical(x, jnp.int32(_shift(d)))


def _mod(x, d):
    return x & jnp.int32(d - 1)


def _sigmoid(x):
    return 1.0 / (1.0 + jnp.exp(-x))


def _silu(x):
    return x * _sigmoid(x)


def _rms(x):
    return x * lax.rsqrt(jnp.mean(x * x, axis=-1, keepdims=True) + EPS)


def _split_dot(x, w):
    hi = x.astype(BF16)
    lo = (x - hi.astype(F32)).astype(BF16)
    return _dot(hi, w) + _dot(lo, w)


def _head_rms(x, seg):
    ss = _split_dot(x * x, seg)
    return x * lax.rsqrt(ss * (1.0 / HEAD_DIM) + EPS)


def _rope(x, cos, sin_signed):
    w = x.shape[-1]
    lane = lax.broadcasted_iota(jnp.int32, x.shape, 1)
    first_half = _mod(lane, HEAD_DIM) < (HEAD_DIM // 2)
    partner = jnp.where(first_half, pltpu.roll(x, w - HEAD_DIM // 2, 1), pltpu.roll(x, HEAD_DIM // 2, 1))
    return x * cos + partner * sin_signed


def _topk_mask(imp, k):
    nb = imp.shape[0]
    blk = lax.broadcasted_iota(jnp.int32, imp.shape, 0).astype(F32)
    sel = jnp.zeros(imp.shape, F32)
    for _ in range(k):
        mx = jnp.max(imp, axis=0, keepdims=True)
        first = jnp.min(jnp.where(imp == mx, blk, float(nb)), axis=0, keepdims=True)
        hit = blk == first
        sel = jnp.where(hit, 1.0, sel)
        imp = jnp.where(hit, -jnp.inf, imp)
    return sel


def _block_importance(imp, blk, cur):
    return jnp.where((blk == cur) | (blk == 0), FORCED_SCORE, jnp.where(blk > cur, -1.0, imp))


def _ada_kernel(c_ref, w_ref, b_ref, o_ref):
    o_ref[...] = _dot(_silu(c_ref[...]).astype(BF16), w_ref[...]) + b_ref[...]


def _ada_all_layers(c_all, w_ada, b_ada):
    depth, rows = w_ada.shape[0], c_all.shape[0]
    tn = 1536
    return pl.pallas_call(
        _ada_kernel,
        grid=(depth, 6 * D_MODEL // tn),
        in_specs=[pl.BlockSpec((rows, D_MODEL), lambda l, j: (0, 0)),
                  pl.BlockSpec((None, D_MODEL, tn), lambda l, j: (l, 0, j)),
                  pl.BlockSpec((None, 1, tn), lambda l, j: (l, 0, j))],
        out_specs=pl.BlockSpec((None, rows, tn), lambda l, j: (l, 0, j)),
        out_shape=jax.ShapeDtypeStruct((depth, rows, 6 * D_MODEL), F32),
        compiler_params=_params("parallel", "parallel"),
        name="ada",
    )(c_all, w_ada, b_ada)


def _mod_operand(mod, rows_per_batch, tm):
    if rows_per_batch % tm == 0:
        per = rows_per_batch // tm
        return mod[:, None, :], pl.BlockSpec((None, 1, D_MODEL), lambda i: (i // per, 0, 0))
    return jnp.repeat(mod, rows_per_batch, axis=0), pl.BlockSpec((tm, D_MODEL), lambda i: (i, 0))


def _const_spec(shape):
    nd = len(shape)
    return pl.BlockSpec(shape, lambda i: (0,) * nd)


def _inproj_kernel(x_ref, gn_ref, sc_ref, sh_ref, wu_ref, wq_ref, wkv_ref, wg_ref, wm_ref,
                   gq_ref, gks_ref, gkw_ref, cos_ref, sin_ref, seg_ref, pe_ref, ug_ref, gm_ref, *out, transposed):
    h = _rms(x_ref[...]) * gn_ref[...]
    h = h * (1.0 + sc_ref[...]) + sh_ref[...]
    hb = h.astype(BF16)

    u = _dot(hb, wu_ref[...])
    ug_ref[...] = u[:, :D_CONV] * _sigmoid(u[:, D_CONV:])
    gm_ref[...] = _sigmoid(_dot(hb, wm_ref[...])).astype(BF16)
    g3 = _sigmoid(_dot(hb, wg_ref[...]))

    cos1, sin1 = cos_ref[...], sin_ref[...]
    cos4 = jnp.concatenate([cos1] * 4, axis=1)
    sin4 = jnp.concatenate([sin1] * 4, axis=1)
    qn = _head_rms(_dot(hb, wq_ref[...]), seg_ref[...]) * gq_ref[...]
    qr = _rope(qn, cos4, sin4) * (ATTN_SCALE * LOG2E)
    qn = qn * (ATTN_SCALE * LOG2E)

    kv = _dot(hb, wkv_ref[...])
    seg1 = seg_ref[:LANES, :LANES]
    ks = _rope(_head_rms(kv[:, 2 * LANES:3 * LANES], seg1) * gks_ref[...], cos1, sin1)
    kw = _rope(_head_rms(kv[:, 4 * LANES:5 * LANES], seg1) * gkw_ref[...], cos1, sin1)
    vs = kv[:, 3 * LANES:4 * LANES]
    vw = kv[:, 5 * LANES:]

    if not transposed:
        qn_ref, qr_ref, rows_ref, win_ref, g3_ref = out
        qn_ref[...] = qn.astype(BF16)
        qr_ref[...] = qr.astype(BF16)
        rows_ref[:, :2 * LANES] = kv[:, :2 * LANES]
        rows_ref[:, 2 * LANES:3 * LANES] = ks
        rows_ref[:, 3 * LANES:] = vs
        win_ref[:, :LANES] = kw
        win_ref[:, LANES:] = vw
        g3_ref[...] = g3
        return

    qnt_ref, qrt_ref, rowst_ref, wint_ref, g3t_ref, cmp_ref, ks_ref, kw_ref, vst_ref, vwt_ref = out
    qnt_ref[...] = qn.T.astype(BF16)
    qrt_ref[...] = qr.T.astype(BF16)
    rowst_ref[:2 * LANES, :] = kv[:, :2 * LANES].T
    rowst_ref[2 * LANES:3 * LANES, :] = ks.T
    vst = vs.T
    rowst_ref[3 * LANES:, :] = vst
    wint_ref[:LANES, :] = kw.T
    vwt = vw.T
    wint_ref[LANES:, :] = vwt
    g3t_ref[...] = g3.T[:3 * N_HEADS, :]
    cmp_ref[...] = (kv[:, :2 * LANES] + pe_ref[...]).astype(BF16)
    ks_ref[...] = ks.astype(BF16)
    kw_ref[...] = kw.astype(BF16)
    vst_ref[...] = vst.astype(BF16)
    vwt_ref[...] = vwt.astype(BF16)


def _inproj(x, mods, lw, tabs, rows_per_batch, tm, transposed):
    n = x.shape[0]
    sc, sc_spec = _mod_operand(mods["sc1"], rows_per_batch, tm)
    sh, sh_spec = _mod_operand(mods["sh1"], rows_per_batch, tm)
    cos, sin, tab_tiles = tabs["cos"], tabs["sin"], tabs["cos"].shape[0] // tm
    tab_spec = pl.BlockSpec((tm, LANES), lambda i: (i % tab_tiles, 0))
    row = lambda w: pl.BlockSpec((tm, w), lambda i: (i, 0))
    col = lambda h: pl.BlockSpec((h, tm), lambda i: (0, i))
    sds = jax.ShapeDtypeStruct
    out_shape = [sds((n, D_CONV), F32), sds((n, 2 * D_MODEL), BF16)]
    out_specs = [row(D_CONV), row(2 * D_MODEL)]
    if transposed:
        per = rows_per_batch // tm
        nb = n // rows_per_batch
        bcol = lambda h: pl.BlockSpec((None, h, tm), lambda i: (i // per, 0, i % per))
        out_shape += [sds((512, n), BF16), sds((512, n), BF16), sds((nb, 512, rows_per_batch), F32),
                      sds((nb, 256, rows_per_batch), F32), sds((3 * N_HEADS, n), F32), sds((n, 256), BF16),
                      sds((n, LANES), BF16), sds((n, LANES), BF16), sds((LANES, n), BF16), sds((LANES, n), BF16)]
        out_specs += [col(512), col(512), bcol(512), bcol(256), col(3 * N_HEADS), row(256),
                      row(LANES), row(LANES), col(LANES), col(LANES)]
    else:
        out_shape += [sds((n, 512), BF16), sds((n, 512), BF16), sds((n, 512), F32), sds((n, 256), F32),
                      sds((n, LANES), F32)]
        out_specs += [row(512), row(512), row(512), row(256), row(LANES)]
    return pl.pallas_call(
        functools.partial(_inproj_kernel, transposed=transposed),
        grid=(n // tm,),
        in_specs=[row(D_MODEL), _const_spec((1, D_MODEL)), sc_spec, sh_spec,
                  _const_spec(lw["w_u"].shape), _const_spec(lw["w_q"].shape), _const_spec(lw["w_kv"].shape),
                  _const_spec(lw["w_g"].shape), _const_spec(lw["w_m"].shape),
                  _const_spec((1, 512)), _const_spec((1, LANES)), _const_spec((1, LANES)),
                  tab_spec, tab_spec, _const_spec((512, 512)), _const_spec((tm, 256))],
        out_specs=out_specs,
        out_shape=out_shape,
        compiler_params=_params("parallel"),
        name="inproj",
    )(x, lw["g_norm_mix"], sc, sh, lw["w_u"], lw["w_q"], lw["w_kv"], lw["w_g"], lw["w_m"],
      lw["g_q"], lw["g_ks"], lw["g_kw"], cos, sin, tabs["seg"], lw["pe_tile"][:tm])


def _conv_kernel(*refs, tc, n_tiles):
    if n_tiles > 1:
        cur_ref, halo_ref, pre_ref, w_ref, b_ref, g_ref, beta_ref, y_ref, win_ref = refs
    else:
        cur_ref, pre_ref, w_ref, b_ref, g_ref, beta_ref, y_ref, win_ref = refs
    if n_tiles > 1:
        first = pl.program_id(1) == 0

        @pl.when(first)
        def _():
            win_ref[:CONV_HALO, :] = pre_ref[...]

        @pl.when(jnp.logical_not(first))
        def _():
            win_ref[:CONV_HALO, :] = halo_ref[...]
    else:
        win_ref[:CONV_HALO, :] = pre_ref[...]
    win_ref[CONV_HALO:CONV_HALO + tc, :] = cur_ref[...]
    win_ref[CONV_HALO + tc:, :] = jnp.zeros((CONV_PAD, D_CONV), F32)

    rc = min(tc, 128)
    off = CONV_HALO - (CONV_WIDTH - 1)
    ext = rc + CONV_PAD
    for r0 in range(0, tc, rc):
        acc = None
        for s in range(8):
            z = None
            for a in range(-(-CONV_WIDTH // 8)):
                j = 8 * a + s
                if j < CONV_WIDTH:
                    term = w_ref[j:j + 1, :] * win_ref[r0 + 8 * a:r0 + 8 * a + ext, :]
                    z = term if z is None else z + term
            piece = z[off + s:off + s + rc]
            acc = piece if acc is None else acc + piece
        y = acc + b_ref[...]
        mu = jnp.mean(y, axis=-1, keepdims=True)
        yc = y - mu
        var = jnp.mean(yc * yc, axis=-1, keepdims=True)
        y = yc * lax.rsqrt(var + EPS) * g_ref[...] + beta_ref[...]
        y_ref[r0:r0 + rc, :] = _silu(y).astype(BF16)


def _conv_branch(ug, prefix, lw, tc):
    b, t, _ = ug.shape
    n_tiles = t // tc
    cur_spec = pl.BlockSpec((None, tc, D_CONV), lambda bi, i: (bi, i, 0))
    pre_spec = pl.BlockSpec((None, CONV_HALO, D_CONV), lambda bi, i: (bi, 0, 0))
    vec = lambda r: pl.BlockSpec((r, D_CONV), lambda bi, i: (0, 0))
    ops, specs = [ug], [cur_spec]
    if n_tiles > 1:
        per = tc // CONV_HALO
        ops.append(ug)
        specs.append(pl.BlockSpec((None, CONV_HALO, D_CONV), lambda bi, i: (bi, jnp.maximum(i * per - 1, 0), 0)))
    ops += [prefix, lw["w_dw"], lw["b_dw"], lw["ln_conv_g"], lw["ln_conv_b"]]
    specs += [pre_spec, vec(CONV_HALO), vec(1), vec(1), vec(1)]
    return pl.pallas_call(
        functools.partial(_conv_kernel, tc=tc, n_tiles=n_tiles),
        grid=(b, n_tiles),
        in_specs=specs,
        out_specs=pl.BlockSpec((None, tc, D_CONV), lambda bi, i: (bi, i, 0)),
        out_shape=jax.ShapeDtypeStruct((b, t, D_CONV), BF16),
        scratch_shapes=[pltpu.VMEM((CONV_HALO + tc + CONV_PAD, D_CONV), F32)],
        compiler_params=_params("parallel", "parallel"),
        name="conv",
    )(*ops)


def _compress_kernel(x_ref, w1_ref, b1_ref, w2_ref, b2_ref, gkc_ref, o_ref):
    hid = jax.nn.gelu(_dot(x_ref[...], w1_ref[...]) + b1_ref[...])
    o = _dot(hid.astype(BF16), w2_ref[...]) + b2_ref[...]
    is_key = pl.program_id(0) == 0
    o_ref[...] = jnp.where(is_key, _rms(o) * gkc_ref[...], o)


def _compress(flat, lw):
    _, m, kdim = flat.shape
    tm = min(m, 512)
    hidden = lw["cmp_w1"].shape[-1]
    return pl.pallas_call(
        _compress_kernel,
        grid=(2, m // tm),
        in_specs=[pl.BlockSpec((None, tm, kdim), lambda s, i: (s, i, 0)),
                  pl.BlockSpec((None, kdim, hidden), lambda s, i: (s, 0, 0)),
                  pl.BlockSpec((None, 1, hidden), lambda s, i: (s, 0, 0)),
                  pl.BlockSpec((None, hidden, HEAD_DIM), lambda s, i: (s, 0, 0)),
                  pl.BlockSpec((None, 1, HEAD_DIM), lambda s, i: (s, 0, 0)),
                  pl.BlockSpec((1, HEAD_DIM), lambda s, i: (0, 0))],
        out_specs=pl.BlockSpec((None, tm, HEAD_DIM), lambda s, i: (s, i, 0)),
        out_shape=jax.ShapeDtypeStruct((2, m, HEAD_DIM), F32),
        compiler_params=_params("parallel", "parallel"),
        name="compress",
    )(flat, lw["cmp_w1"], lw["cmp_b1"], lw["cmp_w2"], lw["cmp_b2"], lw["g_kc"])


def _flatten_blocks(cmp_in, b, n_blk):
    x = cmp_in.reshape(b, n_blk, CMP_BLOCK, 2, N_KV, HEAD_DIM)
    x = x.transpose(3, 0, 4, 1, 2, 5)
    return x.reshape(2, b * N_KV * n_blk, CMP_BLOCK * HEAD_DIM)


def _page_copies(pt_ref, cache_ref, buf_ref, sem_ref, layer, slot, seq, buf_slot, n_pages):
    rows = N_KV * HEAD_DIM
    return [pltpu.make_async_copy(cache_ref.at[layer, pt_ref[seq, p], slot],
                                  buf_ref.at[buf_slot, pl.ds(p * rows, rows), :], sem_ref.at[buf_slot])
            for p in range(n_pages)]


def _compress_paged_kernel(pt_ref, cache_ref, w_ref, pe_ref, b1_ref, w2_ref, b2_ref, gkc_ref, seg_ref, o_ref,
                           buf_ref, sem_ref, *, n_pages, n_seq):
    n_rows = n_pages * N_KV
    step = (pl.program_id(0) * 2 + pl.program_id(1)) * n_seq + pl.program_id(2)
    n_steps = pl.num_programs(0) * 2 * n_seq
    cur = step % 2

    def copies(s, buf_slot):
        seq = s % n_seq
        return _page_copies(pt_ref, cache_ref, buf_ref, sem_ref, s // (2 * n_seq), (s // n_seq) % 2, seq, buf_slot,
                            n_pages)

    @pl.when(step == 0)
    def _():
        for cp in copies(step, cur):
            cp.start()

    @pl.when(step + 1 < n_steps)
    def _():
        for cp in copies(step + 1, 1 - cur):
            cp.start()

    for cp in copies(step, cur):
        cp.wait()

    acc = jnp.zeros((n_rows, w_ref.shape[-1]), F32)
    for d in range(0, HEAD_DIM, 2):
        a = jnp.concatenate(
            [(buf_ref[cur, pl.ds(d + u, n_rows, stride=HEAD_DIM), :] + pe_ref[d + u:d + u + 1, :]).astype(BF16)
             for u in range(2)], axis=1)
        acc = acc + _dot(a, w_ref[d // 2])
    hid = jax.nn.gelu(acc + b1_ref[...])
    o = _dot(hid.astype(BF16), w2_ref[...]) + b2_ref[...]
    is_key = pl.program_id(1) == 0
    o_ref[...] = jnp.where(is_key, _head_rms(o, seg_ref[...]) * gkc_ref[...], o)


def _compress_paged(page_table, cache_pages, pw):
    depth = cache_pages.shape[0]
    db, n_pages = page_table.shape
    n_rows = n_pages * N_KV
    hid2 = pw["w1"].shape[-1]
    sel = lambda *s: pl.BlockSpec((None, None) + s, lambda l, sl, b, pt: (l, sl) + (0,) * len(s))
    grid_spec = pltpu.PrefetchScalarGridSpec(
        num_scalar_prefetch=1,
        grid=(depth, 2, db),
        in_specs=[pl.BlockSpec(memory_space=pl.ANY),
                  sel(HEAD_DIM // 2, 2 * PAGE_SIZE, hid2), sel(HEAD_DIM, PAGE_SIZE), sel(1, hid2), sel(hid2, LANES),
                  sel(1, LANES), pl.BlockSpec((None, 1, LANES), lambda l, sl, b, pt: (l, 0, 0)),
                  pl.BlockSpec((LANES, LANES), lambda l, sl, b, pt: (0, 0))],
        out_specs=pl.BlockSpec((None, None, None, n_rows, LANES), lambda l, sl, b, pt: (l, sl, b, 0, 0)),
        scratch_shapes=[pltpu.VMEM((2, n_rows * HEAD_DIM, PAGE_SIZE), F32), pltpu.SemaphoreType.DMA((2,))])
    return pl.pallas_call(
        functools.partial(_compress_paged_kernel, n_pages=n_pages, n_seq=db),
        grid_spec=grid_spec,
        out_shape=jax.ShapeDtypeStruct((depth, 2, db, n_rows, LANES), F32),
        compiler_params=_params("arbitrary", "arbitrary", "arbitrary"),
        name="compress_paged",
    )(page_table, cache_pages, pw["w1"], pw["pe"], pw["b1"], pw["w2"], pw["b2"], pw["g_kc"], pw["seg"])


def _attn_prompt_kernel(qn_ref, qr_ref, g3_ref, kc_ref, vc_ref, ks_ref, vst_ref, kw_ref, vwt_ref,
                        o_ref, bias_ref, ocmp_ref, ms_ref, ls_ref, accs_ref, mw_ref, lw_ref, accw_ref, *, tq):
    i = pl.program_id(1)
    t0 = i * tq
    lq = Q_PER_KV * tq
    lane = lax.broadcasted_iota(jnp.int32, (1, lq), 1)
    qpos = t0 + _mod(lane, tq)
    nb = kc_ref.shape[1]
    blk = lax.broadcasted_iota(jnp.int32, (nb, 1), 0)
    zpad = jnp.zeros((HEAD_DIM, lq), BF16)

    def heads(ref, g):
        return jnp.concatenate([ref[(Q_PER_KV * g + r) * HEAD_DIM:(Q_PER_KV * g + r + 1) * HEAD_DIM, :]
                                for r in range(Q_PER_KV)], axis=1)

    qpad = []
    for g in range(N_KV):
        q4 = heads(qr_ref, g)
        qpad.append(jnp.concatenate([q4, zpad] if g == 0 else [zpad, q4], axis=0))

        cm = blk < _div(qpos + 1, CMP_BLOCK)
        s = jnp.where(cm, _dot(kc_ref[g].astype(BF16), heads(qn_ref, g)), NEG)
        p = jnp.exp2(s - jnp.max(s, axis=0, keepdims=True)) * cm.astype(F32)
        p = p / jnp.maximum(jnp.sum(p, axis=0, keepdims=True), 1e-30)
        ocmp_ref[g] = _dot_t0(vc_ref[g].astype(BF16), p.astype(BF16))

        imp = p[:, :tq]
        for r in range(1, Q_PER_KV):
            imp = imp + p[:, r * tq:(r + 1) * tq]
        imp = _block_importance(imp, blk, _div(qpos[:, :tq], CMP_BLOCK))
        bias_ref[g] = (_topk_mask(imp, N_SELECT) - 1.0) * (-NEG)

    for m_ref, l_ref, acc_ref in ((ms_ref, ls_ref, accs_ref), (mw_ref, lw_ref, accw_ref)):
        m_ref[...] = jnp.full(m_ref.shape, NEG, F32)
        l_ref[...] = jnp.zeros(l_ref.shape, F32)
        acc_ref[...] = jnp.zeros(acc_ref.shape, F32)

    kidx = lax.broadcasted_iota(jnp.int32, (KEY_CHUNK, 1), 0)

    tpos = qpos[:, :tq]

    chains = [(g, slice(r * tq, (r + 1) * tq)) for g in range(N_KV) for r in range(Q_PER_KV)]

    def qk(k_ref, kc0, n_keys):
        kblk = k_ref[pl.ds(kc0, n_keys), :]
        return tuple(_dot(kblk, qpad[g][:, sl]) for g, sl in chains)

    def softmax_pv(scores, bias, vt_ref, kc0, n_keys, m_ref, l_ref, acc_ref):
        m_all, l_all, acc_all = m_ref[...], l_ref[...], acc_ref[...]
        m_out, l_out, alphas, probs = [], [], [], []
        for (g, sl), s in zip(chains, scores):
            s = s + bias[g]
            m = m_all[g, :, sl]
            m_new = jnp.maximum(m, jnp.max(s, axis=0, keepdims=True))
            alpha = jnp.exp2(m - m_new)
            p = jnp.exp2(s - m_new)
            m_out.append(m_new)
            l_out.append(alpha * l_all[g, :, sl] + jnp.sum(p, axis=0, keepdims=True))
            alphas.append(alpha)
            probs.append(p.astype(BF16))
        vts = [vt_ref[g * HEAD_DIM:(g + 1) * HEAD_DIM, pl.ds(kc0, n_keys)] for g in range(N_KV)]
        acc_out = [alpha * acc_all[g, :, sl] + _dot(vts[g], p)
                   for (g, sl), alpha, p in zip(chains, alphas, probs)]
        gather = lambda parts: jnp.stack([jnp.concatenate(parts[g * Q_PER_KV:(g + 1) * Q_PER_KV], axis=1)
                                          for g in range(N_KV)])
        m_ref[...] = gather(m_out)
        l_ref[...] = gather(l_out)
        acc_ref[...] = gather(acc_out)

    kidx2 = lax.broadcasted_iota(jnp.int32, (SEL_CHUNK, 1), 0)

    def sel_chunk(c, causal):
        k0 = pl.multiple_of(c * SEL_CHUNK, SEL_CHUNK)
        bias = []
        for g in range(N_KV):
            rows8 = bias_ref[g, pl.ds(pl.multiple_of(c * 8, 8), 8), :]
            b = jnp.concatenate([jnp.broadcast_to(rows8[j:j + 1, :], (CMP_BLOCK, tq)) for j in range(8)], axis=0)
            bias.append(jnp.where(k0 + kidx2 <= tpos, b, NEG) if causal else b)
        softmax_pv(qk(ks_ref, k0, SEL_CHUNK), bias, vst_ref, k0, SEL_CHUNK, ms_ref, ls_ref, accs_ref)

    def sel_body(c, carry):
        sel_chunk(c, False)
        return carry

    last = (t0 + tq - 1) // SEL_CHUNK
    lax.fori_loop(0, last, sel_body, 0)
    sel_chunk(last, True)

    n_win = WINDOW // KEY_CHUNK
    for c in range(n_win + 1):
        k0 = t0 - WINDOW + c * KEY_CHUNK
        start = pl.multiple_of(jnp.maximum(k0, 0), KEY_CHUNK)
        if c == n_win:
            bias_w = jnp.where(k0 + kidx <= tpos, 0.0, NEG)
        elif c == 0:
            bias_w = jnp.where(tpos - (k0 + kidx) < jnp.where(k0 >= 0, WINDOW, -(2 ** 30)), 0.0, NEG)
        else:
            bias_w = jnp.broadcast_to(jnp.where(k0 >= 0, 0.0, NEG), (1, 1))
        softmax_pv(qk(kw_ref, start, KEY_CHUNK), [bias_w] * N_KV, vwt_ref, start, KEY_CHUNK, mw_ref, lw_ref, accw_ref)

    for g in range(N_KV):
        gate = lambda br: jnp.concatenate(
            [g3_ref[br * N_HEADS + Q_PER_KV * g + r:br * N_HEADS + Q_PER_KV * g + r + 1, :]
             for r in range(Q_PER_KV)], axis=1)
        o = (gate(0) * ocmp_ref[g] + gate(1) * (accs_ref[g] / ls_ref[g]) + gate(2) * (accw_ref[g] / lw_ref[g]))
        for r in range(Q_PER_KV):
            o_ref[(Q_PER_KV * g + r) * HEAD_DIM:(Q_PER_KV * g + r + 1) * HEAD_DIM, :] = (
                o[:, r * tq:(r + 1) * tq].astype(BF16))


def _attn_prompt(qn_t, qr_t, g3_t, kc, vc, ks, vst, kw, vwt, b, t):
    tq = KEY_CHUNK
    nq = t // tq
    lq = Q_PER_KV * tq
    n_cmp = kc.shape[2]
    qcol = lambda h: pl.BlockSpec((h, tq), lambda bi, i: (0, bi * nq + i))
    per_b = pl.BlockSpec((None, N_KV, n_cmp, HEAD_DIM), lambda bi, i: (bi, 0, 0, 0))
    keys = pl.BlockSpec((t, LANES), lambda bi, i: (bi, 0))
    vals = pl.BlockSpec((LANES, t), lambda bi, i: (0, bi))
    st = lambda r: pltpu.VMEM((N_KV, r, lq), F32)
    return pl.pallas_call(
        functools.partial(_attn_prompt_kernel, tq=tq),
        grid=(b, nq),
        in_specs=[qcol(512), qcol(512), qcol(3 * N_HEADS), per_b, per_b, keys, vals, keys, vals],
        out_specs=qcol(512),
        out_shape=jax.ShapeDtypeStruct((512, b * t), BF16),
        scratch_shapes=[pltpu.VMEM((N_KV, n_cmp, tq), F32), st(HEAD_DIM),
                        st(1), st(1), st(HEAD_DIM), st(1), st(1), st(HEAD_DIM)],
        compiler_params=_params("parallel", "arbitrary"),
        name="attn_prompt",
    )(qn_t, qr_t, g3_t, kc, vc, ks, vst, kw, vwt)


def _attn_sample_kernel(pt_ref, *refs, n_pages, n_blk, pos0, tq):
    pages = refs[:ATTN_PAGES]
    (q_ref, qnt_ref, kc_ref, vc_ref, kn_ref, vn_ref, kwn_ref, vwn_ref, win_ref, g3_ref, expand_ref, rsum_ref,
     o_ref, sel_ref, m_ref, l_ref, acc_ref, ocmp_ref) = refs[ATTN_PAGES:]
    del pt_ref
    j = pl.program_id(1)
    lq = Q_PER_KV * tq
    nbp = sel_ref.shape[1]
    n_cmp = kc_ref.shape[1]
    step_keys = ATTN_PAGES * PAGE_SIZE

    @pl.when(j == 0)
    def _():
        lane = lax.broadcasted_iota(jnp.int32, (1, lq), 1)
        qpos = pos0 + _mod(lane, tq)
        blk = lax.broadcasted_iota(jnp.int32, (nbp, 1), 0)
        for g in range(N_KV):
            cm = blk[:n_cmp] < _div(qpos + 1, CMP_BLOCK)
            s = jnp.where(cm, _dot(kc_ref[g].astype(BF16), qnt_ref[g]), NEG)
            p = jnp.exp2(s - jnp.max(s, axis=0, keepdims=True)) * cm.astype(F32)
            p = p / jnp.maximum(jnp.sum(p, axis=0, keepdims=True), 1e-30)
            ocmp_ref[g] = _dot_t0(p.astype(BF16), vc_ref[g].astype(BF16))
            imp = _split_dot(p, rsum_ref[...])
            imp = jnp.concatenate([imp, jnp.zeros((nbp - n_cmp, lq), F32)], axis=0)
            imp = _block_importance(imp, blk, _div(qpos, CMP_BLOCK))
            imp = jnp.where(blk >= n_blk, -jnp.inf, imp)
            sel_ref[g] = _topk_mask(imp, N_SELECT)
        m_ref[...] = jnp.full(m_ref.shape, NEG, F32)
        l_ref[...] = jnp.zeros(l_ref.shape, F32)
        acc_ref[...] = jnp.zeros(acc_ref.shape, F32)

    def update(g, s, pv):
        m = m_ref[g]
        m_new = jnp.maximum(m, jnp.max(s, axis=1, keepdims=True))
        alpha = jnp.exp2(m - m_new)
        p = jnp.exp2(s - m_new)
        l_ref[g] = alpha * l_ref[g] + jnp.sum(p, axis=1, keepdims=True)
        acc_ref[g] = alpha * acc_ref[g] + pv(p)
        m_ref[g] = m_new

    expand = expand_ref[:, pl.ds(pl.multiple_of(j * step_keys, step_keys), step_keys)]
    scores = []
    for g in range(N_KV):
        kt = jnp.concatenate([pg[0, g].astype(BF16) for pg in pages], axis=1)
        picked = _dot_t0(sel_ref[g, :n_cmp, :].astype(BF16), expand)
        scores.append(_dot(q_ref[g], kt) + (picked - 1.0) * (-NEG))
    m_all, l_all, acc_all = m_ref[...], l_ref[...], acc_ref[...]
    m_out, l_out, alphas, probs = [], [], [], []
    for g, s in enumerate(scores):
        m_new = jnp.maximum(m_all[g], jnp.max(s, axis=1, keepdims=True))
        alpha = jnp.exp2(m_all[g] - m_new)
        p = jnp.exp2(s - m_new)
        m_out.append(m_new)
        l_out.append(alpha * l_all[g] + jnp.sum(p, axis=1, keepdims=True))
        alphas.append(alpha)
        probs.append(p.astype(BF16))
    acc_out = []
    for g in range(N_KV):
        vt = jnp.concatenate([pg[1, g].astype(BF16) for pg in pages], axis=1)
        acc_out.append(alphas[g] * acc_all[g] + _dot_t1(probs[g], vt))
    m_ref[...] = jnp.stack(m_out)
    l_ref[...] = jnp.stack(l_out)
    acc_ref[...] = jnp.stack(acc_out)

    @pl.when(j == n_pages // ATTN_PAGES - 1)
    def _():
        rowi = _mod(lax.broadcasted_iota(jnp.int32, (lq, 1), 0), tq)
        keyi = lax.broadcasted_iota(jnp.int32, (1, tq), 1)
        causal = keyi <= rowi
        wlane = lax.broadcasted_iota(jnp.int32, (1, WINDOW), 1)
        kpos = pos0 - WINDOW + wlane
        in_win = ((pos0 + rowi) - kpos < WINDOW) & (kpos >= 0)
        cur_blk = pos0 // CMP_BLOCK
        first_row = (lax.broadcasted_iota(jnp.int32, (8, tq), 0) == cur_blk % 8).astype(F32)
        for g in range(N_KV):
            q = q_ref[g]
            qf = q.astype(F32)
            base = (cur_blk // 8) * 8
            picked = _dot_t0(sel_ref[g, base:base + 8, :], first_row)
            s = jnp.where(causal & (picked > 0.5), _dot_t1(qf, kn_ref[g]), NEG)
            update(g, s, lambda p: _dot(p, vn_ref[g]))
            o_sel = acc_ref[g] / l_ref[g]

            s_pre = jnp.where(in_win, _dot(q, win_ref[0, g].astype(BF16)), NEG)
            s_new = jnp.where(causal, _dot_t1(qf, kwn_ref[g]), NEG)
            m = jnp.maximum(jnp.max(s_pre, axis=1, keepdims=True), jnp.max(s_new, axis=1, keepdims=True))
            p_pre = jnp.exp2(s_pre - m) * in_win.astype(F32)
            p_new = jnp.exp2(s_new - m) * causal.astype(F32)
            l = jnp.sum(p_pre, axis=1, keepdims=True) + jnp.sum(p_new, axis=1, keepdims=True)
            o_win = (_dot_t1(p_pre.astype(BF16), win_ref[1, g].astype(BF16)) + _dot(p_new, vwn_ref[g]))
            o_win = o_win / jnp.maximum(l, 1e-30)

            g3 = g3_ref[g]
            o_ref[g] = g3[:, 0:1] * ocmp_ref[g] + g3[:, 1:2] * o_sel + g3[:, 2:3] * o_win


def _attn_sample(page_table, cache_t, layer, q, qn_t, kc, vc, kn, vn, kwn, vwn, win_t, g3, expand, rsum, pos0, tq):
    db, n_pages = page_table.shape
    lq = Q_PER_KV * tq
    n_cmp = kc.shape[2]
    n_blk = -(-(pos0 + tq) // CMP_BLOCK)
    nbp = -(-n_blk // 8) * 8
    steps = n_pages // ATTN_PAGES

    def page_spec(k):
        return pl.BlockSpec((None, None, 2, N_KV, HEAD_DIM, PAGE_SIZE),
                            lambda b, j, pt: (layer, pt[b, j * ATTN_PAGES + k], 1, 0, 0, 0))

    per_b = lambda *s: pl.BlockSpec((None,) + s, lambda b, j, pt: (b,) + (0,) * len(s))
    win_spec = pl.BlockSpec((None, None, 2, N_KV, HEAD_DIM, WINDOW), lambda b, j, pt: (layer, b, 0, 0, 0, 0))
    const = lambda *s: pl.BlockSpec(s, lambda b, j, pt: (0,) * len(s))
    grid_spec = pltpu.PrefetchScalarGridSpec(
        num_scalar_prefetch=1,
        grid=(db, steps),
        in_specs=[page_spec(k) for k in range(ATTN_PAGES)] + [
            per_b(N_KV, lq, HEAD_DIM), per_b(N_KV, HEAD_DIM, lq), per_b(N_KV, n_cmp, HEAD_DIM),
            per_b(N_KV, n_cmp, HEAD_DIM), per_b(N_KV, tq, HEAD_DIM), per_b(N_KV, tq, HEAD_DIM),
            per_b(N_KV, tq, HEAD_DIM), per_b(N_KV, tq, HEAD_DIM), win_spec,
            per_b(N_KV, lq, 8), const(n_cmp, n_pages * PAGE_SIZE), const(lq, lq)],
        out_specs=per_b(N_KV, lq, HEAD_DIM),
        scratch_shapes=[pltpu.VMEM((N_KV, nbp, lq), F32), pltpu.VMEM((N_KV, lq, 1), F32),
                        pltpu.VMEM((N_KV, lq, 1), F32), pltpu.VMEM((N_KV, lq, HEAD_DIM), F32),
                        pltpu.VMEM((N_KV, lq, HEAD_DIM), F32)])
    return pl.pallas_call(
        functools.partial(_attn_sample_kernel, n_pages=n_pages, n_blk=n_blk, pos0=pos0, tq=tq),
        grid_spec=grid_spec,
        out_shape=jax.ShapeDtypeStruct((db, N_KV, lq, HEAD_DIM), F32),
        compiler_params=_params("parallel", "arbitrary"),
        name="attn_sample",
    )(page_table, *([cache_t] * ATTN_PAGES), q, qn_t, kc, vc, kn, vn, kwn, vwn, win_t, g3, expand, rsum)


def _merge_kernel(x_ref, y_ref, o_ref, gm_ref, gt_ref, sc_ref, sh_ref, gn_ref, wc_ref, wa_ref, wo_ref,
                  wr_ref, br_ref, x1_ref, h2_ref, dw_ref, *, o_transposed):
    tm = x_ref.shape[0]
    halves = [slice(h * (tm // 2), (h + 1) * (tm // 2)) for h in range(2)]
    rows_of = lambda ref, rows: ref[...] if ref.shape[0] == 1 else ref[rows, :]

    branch = []
    for rows in halves:
        attn = _dot_t0(o_ref[:, rows], wa_ref[...]) if o_transposed else _dot(o_ref[rows, :], wa_ref[...])
        branch.append((_dot(y_ref[rows, :], wc_ref[...]), attn))
    mixed = []
    for rows, (conv, attn) in zip(halves, branch):
        gm = gm_ref[rows, :]
        merged = gm[:, :D_MODEL].astype(F32) * conv + gm[:, D_MODEL:].astype(F32) * attn
        mixed.append(_dot(merged.astype(BF16), wo_ref[...]))
    x1s, h2s, affs = [], [], []
    for rows, mix in zip(halves, mixed):
        x1 = x_ref[rows, :] + rows_of(gt_ref, rows) * mix
        h2 = _rms(x1) * gn_ref[...]
        h2 = h2 * (1.0 + rows_of(sc_ref, rows)) + rows_of(sh_ref, rows)
        h_hi = h2.astype(BF16)
        h_lo = (h2 - h_hi.astype(F32)).astype(BF16)
        x1s.append(x1)
        h2s.append(h_hi)
        affs.append(_sigmoid(_dot(h_hi, wr_ref[0]) + _dot(h_lo, wr_ref[0]) + _dot(h_hi, wr_ref[1])))
    x1_ref[...] = jnp.concatenate(x1s, axis=0)
    h2_ref[...] = jnp.concatenate(h2s, axis=0)

    dws = []
    for aff in affs:
        score = aff + br_ref[...]
        e = lax.broadcasted_iota(jnp.int32, score.shape, 1)
        grp = _div(e, EXPERTS_PER_GROUP)
        e = e.astype(F32)
        big = float(LANES)

        def top2(j):
            vals = jnp.where(grp == j, score, -jnp.inf)
            m1 = jnp.max(vals, axis=-1, keepdims=True)
            i1 = jnp.min(jnp.where(vals == m1, e, big), axis=-1, keepdims=True)
            vals = jnp.where(e == i1, -jnp.inf, vals)
            m2 = jnp.max(vals, axis=-1, keepdims=True)
            i2 = jnp.min(jnp.where(vals == m2, e, big), axis=-1, keepdims=True)
            return m1 + m2, i1, i2

        best, ia, ib = top2(0)
        for j in range(1, N_GROUPS):
            gs, i1, i2 = top2(j)
            better = gs > best
            best = jnp.where(better, gs, best)
            ia = jnp.where(better, i1, ia)
            ib = jnp.where(better, i2, ib)
        hit_a = e == ia
        hit_b = e == ib
        aff_a = jnp.sum(jnp.where(hit_a, aff, 0.0), axis=-1, keepdims=True)
        aff_b = jnp.sum(jnp.where(hit_b, aff, 0.0), axis=-1, keepdims=True)
        tot = aff_a + aff_b
        dws.append(jnp.where(hit_a, aff_a / tot, 0.0) + jnp.where(hit_b, aff_b / tot, 0.0))
    dw_ref[...] = jnp.concatenate(dws, axis=0)


def _merge(x, y, o, gm, mods, lw, shared, rows_per_batch, tm, o_transposed):
    n = x.shape[0]
    gt, gt_spec = _mod_operand(mods["gt1"], rows_per_batch, tm)
    sc, sc_spec = _mod_operand(mods["sc2"], rows_per_batch, tm)
    sh, sh_spec = _mod_operand(mods["sh2"], rows_per_batch, tm)
    row = lambda w: pl.BlockSpec((tm, w), lambda i: (i, 0))
    o_spec = pl.BlockSpec((512, tm), lambda i: (0, i)) if o_transposed else row(512)
    return pl.pallas_call(
        functools.partial(_merge_kernel, o_transposed=o_transposed),
        grid=(n // tm,),
        in_specs=[row(D_MODEL), row(D_CONV), o_spec, row(2 * D_MODEL), gt_spec, sc_spec, sh_spec,
                  _const_spec((1, D_MODEL)), _const_spec((D_CONV, D_MODEL)), _const_spec((512, D_MODEL)),
                  _const_spec((D_MODEL, D_MODEL)), _const_spec((2, D_MODEL, LANES)), _const_spec((1, LANES))],
        out_specs=[row(D_MODEL), row(D_MODEL), row(LANES)],
        out_shape=[jax.ShapeDtypeStruct((n, D_MODEL), F32), jax.ShapeDtypeStruct((n, D_MODEL), BF16),
                   jax.ShapeDtypeStruct((n, LANES), F32)],
        compiler_params=_params("parallel"),
        name="merge",
    )(x, y, o, gm, gt, sc, sh, lw["g_norm_ffn"], lw["w_conv_out"], lw["w_attn_out"], lw["w_out"],
      shared["w_router"], shared["b_router"])


def _moe_kernel(h_ref, dw_ref, x_ref, gt_ref, wg_ref, wu_ref, wd_ref, o_ref, acc_ref):
    ex = pl.program_id(1)

    @pl.when(ex == 0)
    def _():
        acc_ref[...] = jnp.zeros(acc_ref.shape, F32)

    h = h_ref[...]
    dw = dw_ref[...]
    lane = lax.broadcasted_iota(jnp.int32, dw.shape, 1)
    part = None
    for j in range(MOE_EXPERTS_PER_STEP):
        w_col = jnp.sum(jnp.where(lane == ex * MOE_EXPERTS_PER_STEP + j, dw, 0.0), axis=-1, keepdims=True)
        hid = _silu(_dot(h, wg_ref[j])) * _dot(h, wu_ref[j]) * w_col
        out = _dot(hid.astype(BF16), wd_ref[j])
        part = out if part is None else part + out
    acc_ref[...] += part

    @pl.when(ex == N_EXPERTS // MOE_EXPERTS_PER_STEP - 1)
    def _():
        o_ref[...] = x_ref[...] + gt_ref[...] * acc_ref[...]


def _moe(h2, dw, x1, mods, lw, rows_per_batch, tm):
    n = h2.shape[0]
    gt, gt_spec = _mod_operand(mods["gt2"], rows_per_batch, tm)
    gt_spec = pl.BlockSpec(gt_spec.block_shape, lambda i, ex, f=gt_spec.index_map: f(i))
    row = lambda w: pl.BlockSpec((tm, w), lambda i, ex: (i, 0))
    return pl.pallas_call(
        _moe_kernel,
        grid=(n // tm, N_EXPERTS // MOE_EXPERTS_PER_STEP),
        in_specs=[row(D_MODEL), row(LANES), row(D_MODEL), gt_spec,
                  pl.BlockSpec((MOE_EXPERTS_PER_STEP, D_MODEL, D_EXPERT), lambda i, ex: (ex, 0, 0)),
                  pl.BlockSpec((MOE_EXPERTS_PER_STEP, D_MODEL, D_EXPERT), lambda i, ex: (ex, 0, 0)),
                  pl.BlockSpec((MOE_EXPERTS_PER_STEP, D_EXPERT, D_MODEL), lambda i, ex: (ex, 0, 0))],
        out_specs=row(D_MODEL),
        out_shape=jax.ShapeDtypeStruct((n, D_MODEL), F32),
        scratch_shapes=[pltpu.VMEM((tm, D_MODEL), F32)],
        compiler_params=_params("parallel", "arbitrary"),
        name="moe",
    )(h2, dw, x1, gt, lw["w_exp_gate"], lw["w_exp_up"], lw["w_exp_down"])


def _rope_tables(pos):
    half = HEAD_DIM // 2
    inv = ROPE_THETA ** (-jnp.arange(half, dtype=F32) / half)
    ang = pos.astype(F32)[:, None] * inv[None, :]
    cos, sin = jnp.cos(ang), jnp.sin(ang)
    cos = jnp.concatenate([cos, cos], axis=1)
    sin = jnp.concatenate([-sin, sin], axis=1)
    return jnp.concatenate([cos, cos], axis=1), jnp.concatenate([sin, sin], axis=1)


def _split_mod(mod):
    names = ("sh1", "sc1", "gt1", "sh2", "sc2", "gt2")
    return dict(zip(names, jnp.split(mod, 6, axis=-1)))


def _paged_compress_weights(cmp_pe, cmp_w1, cmp_b1, cmp_w2, cmp_b2, g_kc, seg):
    depth = cmp_w1.shape[0]
    hidden = cmp_w1.shape[-1]
    eye = jnp.eye(2, dtype=BF16)
    w1 = cmp_w1.astype(BF16).reshape(depth, 2, CMP_BLOCK, HEAD_DIM, hidden).transpose(0, 1, 3, 2, 4)
    w1 = jnp.einsum("lsdph,jk->lsdjpkh", w1, eye).reshape(depth, 2, HEAD_DIM // 2, 2 * PAGE_SIZE, 2 * hidden)
    w2 = jnp.einsum("lshd,jk->lsjhkd", cmp_w2.astype(BF16), eye).reshape(depth, 2, 2 * hidden, LANES)
    pe = jnp.tile(cmp_pe.transpose(0, 1, 3, 2), (1, 1, 1, 2))
    return dict(w1=w1, w2=w2, pe=pe,
                b1=jnp.tile(cmp_b1, (1, 1, 2))[:, :, None, :], b2=jnp.tile(cmp_b2, (1, 1, 2))[:, :, None, :],
                g_kc=jnp.tile(g_kc, (1, 2))[:, None, :], seg=seg[:LANES, :LANES])


def kernel(x_prompt, x_sample, c_prompt, c_sample, cache_kv, state_win_kv, state_conv, page_table, w_ada, b_ada, g_norm_mix, g_norm_ffn, w_in, w_dw, b_dw, ln_conv_g, ln_conv_b, w_conv_out, g_q, g_kc, g_ks, g_kw, cmp_pe, cmp_w1, cmp_b1, cmp_w2, cmp_b2, w_attn_out, w_out, w_router, b_router, w_exp_gate, w_exp_up, w_exp_down):
    depth = w_in.shape[0]
    b, t, _ = x_prompt.shape
    db, dt, _ = x_sample.shape
    n_pages = page_table.shape[1]
    past_len = n_pages * PAGE_SIZE
    win_buf = state_win_kv.shape[2]
    tm_p, tm_s = 512, db * dt
    lq_s = Q_PER_KV * dt
    n_cmp_s = (past_len + dt) // CMP_BLOCK
    assert t % tm_p == 0 and t % KEY_CHUNK == 0 and win_buf == WINDOW and WINDOW % KEY_CHUNK == 0
    assert n_cmp_s * CMP_BLOCK == past_len and n_pages % ATTN_PAGES == 0

    c0, c1, c2, c3 = 2 * D_CONV, 2 * D_CONV + 512, 2 * D_CONV + 512 + 768, 2 * D_CONV + 512 + 768 + 3 * N_HEADS
    w_in_b = w_in.astype(BF16)
    w_dw_p = jnp.pad(w_dw, ((0, 0), (0, CONV_HALO - CONV_WIDTH), (0, 0)))
    pe256 = cmp_pe[:, :, :, None, :].repeat(N_KV, axis=3).transpose(0, 2, 1, 3, 4).reshape(depth, CMP_BLOCK, 256)
    pe_tile = jnp.tile(pe256, (1, tm_p // CMP_BLOCK, 1))
    layers = []
    for l in range(depth):
        layers.append(dict(
            g_norm_mix=g_norm_mix[l][None], g_norm_ffn=g_norm_ffn[l][None],
            w_u=w_in_b[l][:, :c0], w_q=w_in_b[l][:, c0:c1], w_kv=w_in_b[l][:, c1:c2],
            w_g=jnp.pad(w_in_b[l][:, c2:c3], ((0, 0), (0, LANES - 3 * N_HEADS))), w_m=w_in_b[l][:, c3:],
            g_q=jnp.tile(g_q[l], N_HEADS)[None], g_ks=jnp.tile(g_ks[l], N_KV)[None], g_kw=jnp.tile(g_kw[l], N_KV)[None],
            g_kc=g_kc[l][None], pe_tile=pe_tile[l],
            w_dw=w_dw_p[l], b_dw=b_dw[l][None], ln_conv_g=ln_conv_g[l][None], ln_conv_b=ln_conv_b[l][None],
            w_conv_out=w_conv_out[l].astype(BF16), w_attn_out=w_attn_out[l].astype(BF16), w_out=w_out[l].astype(BF16),
            cmp_w1=cmp_w1[l].astype(BF16), cmp_b1=cmp_b1[l][:, None, :], cmp_w2=cmp_w2[l].astype(BF16),
            cmp_b2=cmp_b2[l][:, None, :],
            w_exp_gate=w_exp_gate[l].astype(BF16), w_exp_up=w_exp_up[l].astype(BF16),
            w_exp_down=w_exp_down[l].astype(BF16)))
    wr = jnp.pad(w_router, ((0, 0), (0, LANES - N_EXPERTS)))
    wr_hi = wr.astype(BF16)
    wr_lo = (wr - wr_hi.astype(F32)).astype(BF16)
    shared = dict(w_router=jnp.stack([wr_hi, wr_lo]),
                  b_router=jnp.pad(b_router, (0, LANES - N_EXPERTS))[None])
    seg = (jnp.arange(512)[:, None] // HEAD_DIM == jnp.arange(512)[None, :] // HEAD_DIM).astype(BF16)
    cos_p, sin_p = _rope_tables(jnp.arange(t, dtype=jnp.int32))
    cos_s, sin_s = _rope_tables(past_len + jnp.arange(dt, dtype=jnp.int32))
    tabs_p = dict(cos=cos_p, sin=sin_p, seg=seg)
    tabs_s = dict(cos=jnp.tile(cos_s, (db, 1)), sin=jnp.tile(sin_s, (db, 1)), seg=seg)
    rsum = (jnp.arange(lq_s)[:, None] % dt == jnp.arange(lq_s)[None, :] % dt).astype(BF16)
    expand = (jnp.arange(n_cmp_s)[:, None] == jnp.arange(past_len)[None, :] // CMP_BLOCK).astype(BF16)

    rows = b + db
    rows_p = -(-rows // 8) * 8
    c_all = jnp.pad(jnp.concatenate([c_prompt, c_sample], axis=0), ((0, rows_p - rows), (0, 0)))
    mod_all = _ada_all_layers(c_all, w_ada.astype(BF16), b_ada[:, None, :])

    cache_t = cache_kv.transpose(0, 1, 3, 4, 5, 2)
    win_t = state_win_kv.transpose(0, 1, 3, 4, 5, 2)

    paged = _compress_paged(page_table, cache_t.reshape(depth, -1, 4, N_KV * HEAD_DIM, PAGE_SIZE),
                            _paged_compress_weights(cmp_pe, cmp_w1, cmp_b1, cmp_w2, cmp_b2, g_kc, seg))
    paged = paged.reshape(depth, 2, db, n_pages, N_KV, 2, HEAD_DIM).transpose(0, 1, 2, 4, 3, 5, 6)
    paged = paged.reshape(depth, 2, db, N_KV, n_cmp_s, HEAD_DIM)

    xp = x_prompt.reshape(b * t, D_MODEL)
    xs = x_sample.reshape(db * dt, D_MODEL)
    outs = {k: [] for k in ("kv_p", "kv_s", "win_p", "win_s", "conv_p", "conv_s")}
    zero_prefix = jnp.zeros((b, CONV_HALO, D_CONV), F32)

    for l in range(depth):
        lw = layers[l]
        mods_p = _split_mod(mod_all[l, :b])
        mods_s = _split_mod(mod_all[l, b:b + db])

        (ug, gm, qn_t, qr_t, rows_t, win_tp, g3_t, cmp_in, ks, kw, vst, vwt) = _inproj(
            xp, mods_p, lw, tabs_p, t, tm_p, True)
        y = _conv_branch(ug.reshape(b, t, D_CONV), zero_prefix, lw, 512)
        n_blk = t // CMP_BLOCK
        kvc = _compress(_flatten_blocks(cmp_in, b, n_blk), lw).reshape(2, b, N_KV, n_blk, HEAD_DIM)
        o_t = _attn_prompt(qn_t, qr_t, g3_t, kvc[0], kvc[1], ks, vst, kw, vwt, b, t)
        x1, h2, dw = _merge(xp, y.reshape(b * t, D_CONV), o_t, gm, mods_p, lw, shared, t, tm_p, True)
        xp = _moe(h2, dw, x1, mods_p, lw, t, tm_p)
        outs["kv_p"].append(rows_t.reshape(b, 4, N_KV, HEAD_DIM, t).transpose(0, 4, 1, 2, 3))
        outs["win_p"].append(win_tp.reshape(b, 2, N_KV, HEAD_DIM, t)[..., t - win_buf:].transpose(0, 4, 1, 2, 3))
        outs["conv_p"].append(ug.reshape(b, t, D_CONV)[:, t - (CONV_WIDTH - 1):])

        ug, gm, qn, qr, rows_new, win_new, g3 = _inproj(xs, mods_s, lw, tabs_s, dt, tm_s, False)
        ug3 = ug.reshape(db, dt, D_CONV)
        conv_ext = jnp.concatenate([state_conv[l], ug3], axis=1)
        prefix = jnp.pad(state_conv[l], ((0, 0), (CONV_HALO - (CONV_WIDTH - 1), 0), (0, 0)))
        y = _conv_branch(ug3, prefix, lw, dt)
        r5 = rows_new.reshape(db, dt, 4, N_KV, HEAD_DIM)
        w5 = win_new.reshape(db, dt, 2, N_KV, HEAD_DIM)
        per_group = lambda a: a.transpose(0, 2, 1, 3)
        q5 = lambda a: a.reshape(db, dt, N_KV, Q_PER_KV, HEAD_DIM)
        q_rows = q5(qr).transpose(0, 2, 3, 1, 4).reshape(db, N_KV, lq_s, HEAD_DIM)
        qn_cols = q5(qn).transpose(0, 2, 4, 3, 1).reshape(db, N_KV, HEAD_DIM, lq_s)
        g3_rows = g3[:, :3 * N_HEADS].reshape(db, dt, 3, N_KV, Q_PER_KV).transpose(0, 3, 4, 1, 2)
        g3_rows = jnp.pad(g3_rows.reshape(db, N_KV, lq_s, 3), ((0, 0), (0, 0), (0, 0), (0, 5)))
        o_s = _attn_sample(page_table, cache_t, l, q_rows, qn_cols, paged[l, 0], paged[l, 1],
                           per_group(r5[:, :, 2]), per_group(r5[:, :, 3]), per_group(w5[:, :, 0]),
                           per_group(w5[:, :, 1]), win_t, g3_rows, expand, rsum, past_len, dt)
        o = o_s.reshape(db, N_KV, Q_PER_KV, dt, HEAD_DIM).transpose(0, 3, 1, 2, 4).reshape(db * dt, 512).astype(BF16)
        x1, h2, dw = _merge(xs, y.reshape(db * dt, D_CONV), o, gm, mods_s, lw, shared, dt, tm_s, False)
        xs = _moe(h2, dw, x1, mods_s, lw, dt, tm_s)
        outs["kv_s"].append(r5)
        outs["win_s"].append(jnp.concatenate([state_win_kv[l], w5], axis=1)[:, dt:])
        outs["conv_s"].append(conv_ext[:, dt:])

    return (xp.reshape(b, t, D_MODEL), xs.reshape(db, dt, D_MODEL),
            jnp.stack(outs["kv_p"]), jnp.stack(outs["kv_s"]), jnp.stack(outs["win_p"]), jnp.stack(outs["win_s"]),
            jnp.stack(outs["conv_p"]), jnp.stack(outs["conv_s"]))
```

```python
import functools

import jax
import jax.numpy as jnp
from jax import lax
from jax.experimental import pallas as pl
from jax.experimental.pallas import tpu as pltpu

F32 = jnp.float32
BF16 = jnp.bfloat16

D_MODEL = 1024
D_CONV = 512
CONV_WIDTH = 31
CONV_HALO = 32
CONV_PAD = 16
N_HEADS = 8
HEAD_DIM = 64
N_KV = 2
Q_PER_KV = 4
CMP_BLOCK = 64
N_SELECT = 16
WINDOW = 512
N_EXPERTS = 16
N_GROUPS = 4
EXPERTS_PER_GROUP = 4
D_EXPERT = 256
PAGE_SIZE = 128
ROPE_THETA = 10000.0
ATTN_SCALE = HEAD_DIM ** -0.5
FORCED_SCORE = 1e4
EPS = 1e-6
NEG = -1e30
LANES = 128
KEY_CHUNK = 256
SEL_CHUNK = 512
VT_ROWS = 80
ATTN_PAGES = 16
MOE_EXPERTS_PER_STEP = 4
LOG2E = 1.4426950408889634
VMEM_LIMIT = 56 * 1024 * 1024


def _params(*sem):
    return pltpu.CompilerParams(dimension_semantics=sem, vmem_limit_bytes=VMEM_LIMIT)


def _dot(a, b):
    return jnp.dot(a, b, preferred_element_type=F32)


def _dot_t0(a, b):
    return lax.dot_general(a, b, (((0,), (0,)), ((), ())), preferred_element_type=F32)


def _dot_t1(a, b):
    return lax.dot_general(a, b, (((1,), (1,)), ((), ())), preferred_element_type=F32)


def _shift(d):
    assert d & (d - 1) == 0
    return d.bit_length() - 1


def _div(x, d):
    return lax.shift_right_logical(x, jnp.int32(_shift(d)))


def _mod(x, d):
    return x & jnp.int32(d - 1)


def _sigmoid(x):
    return 1.0 / (1.0 + jnp.exp(-x))


def _silu(x):
    return x * _sigmoid(x)


def _rms(x):
    return x * lax.rsqrt(jnp.mean(x * x, axis=-1, keepdims=True) + EPS)


def _split_dot(x, w):
    hi = x.astype(BF16)
    lo = (x - hi.astype(F32)).astype(BF16)
    return _dot(hi, w) + _dot(lo, w)


def _head_rms(x, seg):
    ss = _split_dot(x * x, seg)
    return x * lax.rsqrt(ss * (1.0 / HEAD_DIM) + EPS)


def _rope(x, cos, sin_signed):
    w = x.shape[-1]
    lane = lax.broadcasted_iota(jnp.int32, x.shape, 1)
    first_half = _mod(lane, HEAD_DIM) < (HEAD_DIM // 2)
    partner = jnp.where(first_half, pltpu.roll(x, w - HEAD_DIM // 2, 1), pltpu.roll(x, HEAD_DIM // 2, 1))
    return x * cos + partner * sin_signed


def _topk_mask(imp, k):
    nb = imp.shape[0]
    blk = lax.broadcasted_iota(jnp.int32, imp.shape, 0).astype(F32)
    sel = jnp.zeros(imp.shape, F32)
    for _ in range(k):
        mx = jnp.max(imp, axis=0, keepdims=True)
        first = jnp.min(jnp.where(imp == mx, blk, float(nb)), axis=0, keepdims=True)
        hit = blk == first
        sel = jnp.where(hit, 1.0, sel)
        imp = jnp.where(hit, -jnp.inf, imp)
    return sel


def _block_importance(imp, blk, cur):
    return jnp.where((blk == cur) | (blk == 0), FORCED_SCORE, jnp.where(blk > cur, -1.0, imp))


def _ada_kernel(c_ref, w_ref, b_ref, o_ref):
    o_ref[...] = _dot(_silu(c_ref[...]).astype(BF16), w_ref[...]) + b_ref[...]


def _ada_all_layers(c_all, w_ada, b_ada):
    depth, rows = w_ada.shape[0], c_all.shape[0]
    tn = 1536
    return pl.pallas_call(
        _ada_kernel,
        grid=(depth, 6 * D_MODEL // tn),
        in_specs=[pl.BlockSpec((rows, D_MODEL), lambda l, j: (0, 0)),
                  pl.BlockSpec((None, D_MODEL, tn), lambda l, j: (l, 0, j)),
                  pl.BlockSpec((None, 1, tn), lambda l, j: (l, 0, j))],
        out_specs=pl.BlockSpec((None, rows, tn), lambda l, j: (l, 0, j)),
        out_shape=jax.ShapeDtypeStruct((depth, rows, 6 * D_MODEL), F32),
        compiler_params=_params("parallel", "parallel"),
        name="ada",
    )(c_all, w_ada, b_ada)


def _mod_operand(mod, rows_per_batch, tm):
    if rows_per_batch % tm == 0:
        per = rows_per_batch // tm
        return mod[:, None, :], pl.BlockSpec((None, 1, D_MODEL), lambda i: (i // per, 0, 0))
    return jnp.repeat(mod, rows_per_batch, axis=0), pl.BlockSpec((tm, D_MODEL), lambda i: (i, 0))


def _const_spec(shape):
    nd = len(shape)
    return pl.BlockSpec(shape, lambda i: (0,) * nd)


def _inproj_kernel(x_ref, gn_ref, sc_ref, sh_ref, wu_ref, wq_ref, wkv_ref, wg_ref, wm_ref,
                   gq_ref, gks_ref, gkw_ref, cos_ref, sin_ref, seg_ref, pe_ref, ug_ref, gm_ref, *out, transposed):
    h = _rms(x_ref[...]) * gn_ref[...]
    h = h * (1.0 + sc_ref[...]) + sh_ref[...]
    hb = h.astype(BF16)

    u = _dot(hb, wu_ref[...])
    ug_ref[...] = u[:, :D_CONV] * _sigmoid(u[:, D_CONV:])
    gm_ref[...] = _sigmoid(_dot(hb, wm_ref[...])).astype(BF16)
    g3 = _sigmoid(_dot(hb, wg_ref[...]))

    cos1, sin1 = cos_ref[...], sin_ref[...]
    cos4 = jnp.concatenate([cos1] * 4, axis=1)
    sin4 = jnp.concatenate([sin1] * 4, axis=1)
    qn = _head_rms(_dot(hb, wq_ref[...]), seg_ref[...]) * gq_ref[...]
    qr = _rope(qn, cos4, sin4) * (ATTN_SCALE * LOG2E)
    qn = qn * (ATTN_SCALE * LOG2E)

    kv = _dot(hb, wkv_ref[...])
    seg1 = seg_ref[:LANES, :LANES]
    ks = _rope(_head_rms(kv[:, 2 * LANES:3 * LANES], seg1) * gks_ref[...], cos1, sin1)
    kw = _rope(_head_rms(kv[:, 4 * LANES:5 * LANES], seg1) * gkw_ref[...], cos1, sin1)
    vs = kv[:, 3 * LANES:4 * LANES]
    vw = kv[:, 5 * LANES:]

    if not transposed:
        qn_ref, qr_ref, rows_ref, win_ref, g3_ref = out
        qn_ref[...] = qn.astype(BF16)
        qr_ref[...] = qr.astype(BF16)
        rows_ref[:, :2 * LANES] = kv[:, :2 * LANES]
        rows_ref[:, 2 * LANES:3 * LANES] = ks
        rows_ref[:, 3 * LANES:] = vs
        win_ref[:, :LANES] = kw
        win_ref[:, LANES:] = vw
        g3_ref[...] = g3
        return

    qnt_ref, qrt_ref, rowst_ref, wint_ref, g3t_ref, cmp_ref, ks_ref, kw_ref, vst_ref, vwt_ref = out
    qnt_ref[...] = qn.T.astype(BF16)
    qrt_ref[...] = qr.T.astype(BF16)
    rowst_ref[:2 * LANES, :] = kv[:, :2 * LANES].T
    rowst_ref[2 * LANES:3 * LANES, :] = ks.T
    vst = vs.T
    rowst_ref[3 * LANES:, :] = vst
    wint_ref[:LANES, :] = kw.T
    vwt = vw.T
    wint_ref[LANES:, :] = vwt
    g3t_ref[...] = g3.T[:3 * N_HEADS, :]
    cmp_ref[...] = (kv[:, :2 * LANES] + pe_ref[...]).astype(BF16)
    ks_ref[...] = ks.astype(BF16)
    kw_ref[...] = kw.astype(BF16)
    ones = jnp.ones((VT_ROWS - HEAD_DIM, vst.shape[1]), BF16)
    for ref, vt in ((vst_ref, vst), (vwt_ref, vwt)):
        for g in range(N_KV):
            ref[g * VT_ROWS:g * VT_ROWS + HEAD_DIM, :] = vt[g * HEAD_DIM:(g + 1) * HEAD_DIM, :].astype(BF16)
            ref[g * VT_ROWS + HEAD_DIM:(g + 1) * VT_ROWS, :] = ones


def _inproj(x, mods, lw, tabs, rows_per_batch, tm, transposed):
    n = x.shape[0]
    sc, sc_spec = _mod_operand(mods["sc1"], rows_per_batch, tm)
    sh, sh_spec = _mod_operand(mods["sh1"], rows_per_batch, tm)
    cos, sin, tab_tiles = tabs["cos"], tabs["sin"], tabs["cos"].shape[0] // tm
    tab_spec = pl.BlockSpec((tm, LANES), lambda i: (i % tab_tiles, 0))
    row = lambda w: pl.BlockSpec((tm, w), lambda i: (i, 0))
    col = lambda h: pl.BlockSpec((h, tm), lambda i: (0, i))
    sds = jax.ShapeDtypeStruct
    out_shape = [sds((n, D_CONV), F32), sds((n, 2 * D_MODEL), BF16)]
    out_specs = [row(D_CONV), row(2 * D_MODEL)]
    if transposed:
        per = rows_per_batch // tm
        nb = n // rows_per_batch
        bcol = lambda h: pl.BlockSpec((None, h, tm), lambda i: (i // per, 0, i % per))
        out_shape += [sds((512, n), BF16), sds((512, n), BF16), sds((nb, 512, rows_per_batch), F32),
                      sds((nb, 256, rows_per_batch), F32), sds((3 * N_HEADS, n), F32), sds((n, 256), BF16),
                      sds((n, LANES), BF16), sds((n, LANES), BF16), sds((N_KV * VT_ROWS, n), BF16),
                      sds((N_KV * VT_ROWS, n), BF16)]
        out_specs += [col(512), col(512), bcol(512), bcol(256), col(3 * N_HEADS), row(256),
                      row(LANES), row(LANES), col(N_KV * VT_ROWS), col(N_KV * VT_ROWS)]
    else:
        out_shape += [sds((n, 512), BF16), sds((n, 512), BF16), sds((n, 512), F32), sds((n, 256), F32),
                      sds((n, LANES), F32)]
        out_specs += [row(512), row(512), row(512), row(256), row(LANES)]
    return pl.pallas_call(
        functools.partial(_inproj_kernel, transposed=transposed),
        grid=(n // tm,),
        in_specs=[row(D_MODEL), _const_spec((1, D_MODEL)), sc_spec, sh_spec,
                  _const_spec(lw["w_u"].shape), _const_spec(lw["w_q"].shape), _const_spec(lw["w_kv"].shape),
                  _const_spec(lw["w_g"].shape), _const_spec(lw["w_m"].shape),
                  _const_spec((1, 512)), _const_spec((1, LANES)), _const_spec((1, LANES)),
                  tab_spec, tab_spec, _const_spec((512, 512)), _const_spec((tm, 256))],
        out_specs=out_specs,
        out_shape=out_shape,
        compiler_params=_params("parallel"),
        name="inproj",
    )(x, lw["g_norm_mix"], sc, sh, lw["w_u"], lw["w_q"], lw["w_kv"], lw["w_g"], lw["w_m"],
      lw["g_q"], lw["g_ks"], lw["g_kw"], cos, sin, tabs["seg"], lw["pe_tile"][:tm])


def _conv_kernel(*refs, tc, n_tiles):
    if n_tiles > 1:
        cur_ref, halo_ref, pre_ref, w_ref, b_ref, g_ref, beta_ref, y_ref, win_ref = refs
    else:
        cur_ref, pre_ref, w_ref, b_ref, g_ref, beta_ref, y_ref, win_ref = refs
    if n_tiles > 1:
        first = pl.program_id(1) == 0

        @pl.when(first)
        def _():
            win_ref[:CONV_HALO, :] = pre_ref[...]

        @pl.when(jnp.logical_not(first))
        def _():
            win_ref[:CONV_HALO, :] = halo_ref[...]
    else:
        win_ref[:CONV_HALO, :] = pre_ref[...]
    win_ref[CONV_HALO:CONV_HALO + tc, :] = cur_ref[...]
    win_ref[CONV_HALO + tc:, :] = jnp.zeros((CONV_PAD, D_CONV), F32)

    rc = min(tc, 128)
    off = CONV_HALO - (CONV_WIDTH - 1)
    ext = rc + CONV_PAD
    for r0 in range(0, tc, rc):
        acc = None
        for s in range(8):
            z = None
            for a in range(-(-CONV_WIDTH // 8)):
                j = 8 * a + s
                if j < CONV_WIDTH:
                    term = w_ref[j:j + 1, :] * win_ref[r0 + 8 * a:r0 + 8 * a + ext, :]
                    z = term if z is None else z + term
            piece = z[off + s:off + s + rc]
            acc = piece if acc is None else acc + piece
        y = acc + b_ref[...]
        mu = jnp.mean(y, axis=-1, keepdims=True)
        yc = y - mu
        var = jnp.mean(yc * yc, axis=-1, keepdims=True)
        y = yc * lax.rsqrt(var + EPS) * g_ref[...] + beta_ref[...]
        y_ref[r0:r0 + rc, :] = _silu(y).astype(BF16)


def _conv_branch(ug, prefix, lw, tc):
    b, t, _ = ug.shape
    n_tiles = t // tc
    cur_spec = pl.BlockSpec((None, tc, D_CONV), lambda bi, i: (bi, i, 0))
    pre_spec = pl.BlockSpec((None, CONV_HALO, D_CONV), lambda bi, i: (bi, 0, 0))
    vec = lambda r: pl.BlockSpec((r, D_CONV), lambda bi, i: (0, 0))
    ops, specs = [ug], [cur_spec]
    if n_tiles > 1:
        per = tc // CONV_HALO
        ops.append(ug)
        specs.append(pl.BlockSpec((None, CONV_HALO, D_CONV), lambda bi, i: (bi, jnp.maximum(i * per - 1, 0), 0)))
    ops += [prefix, lw["w_dw"], lw["b_dw"], lw["ln_conv_g"], lw["ln_conv_b"]]
    specs += [pre_spec, vec(CONV_HALO), vec(1), vec(1), vec(1)]
    return pl.pallas_call(
        functools.partial(_conv_kernel, tc=tc, n_tiles=n_tiles),
        grid=(b, n_tiles),
        in_specs=specs,
        out_specs=pl.BlockSpec((None, tc, D_CONV), lambda bi, i: (bi, i, 0)),
        out_shape=jax.ShapeDtypeStruct((b, t, D_CONV), BF16),
        scratch_shapes=[pltpu.VMEM((CONV_HALO + tc + CONV_PAD, D_CONV), F32)],
        compiler_params=_params("parallel", "parallel"),
        name="conv",
    )(*ops)


def _compress_kernel(x_ref, w1_ref, b1_ref, w2_ref, b2_ref, gkc_ref, o_ref):
    hid = jax.nn.gelu(_dot(x_ref[...], w1_ref[...]) + b1_ref[...])
    o = _dot(hid.astype(BF16), w2_ref[...]) + b2_ref[...]
    is_key = pl.program_id(0) == 0
    o_ref[...] = jnp.where(is_key, _rms(o) * gkc_ref[...], o)


def _compress(flat, lw):
    _, m, kdim = flat.shape
    tm = min(m, 512)
    hidden = lw["cmp_w1"].shape[-1]
    return pl.pallas_call(
        _compress_kernel,
        grid=(2, m // tm),
        in_specs=[pl.BlockSpec((None, tm, kdim), lambda s, i: (s, i, 0)),
                  pl.BlockSpec((None, kdim, hidden), lambda s, i: (s, 0, 0)),
                  pl.BlockSpec((None, 1, hidden), lambda s, i: (s, 0, 0)),
                  pl.BlockSpec((None, hidden, HEAD_DIM), lambda s, i: (s, 0, 0)),
                  pl.BlockSpec((None, 1, HEAD_DIM), lambda s, i: (s, 0, 0)),
                  pl.BlockSpec((1, HEAD_DIM), lambda s, i: (0, 0))],
        out_specs=pl.BlockSpec((None, tm, HEAD_DIM), lambda s, i: (s, i, 0)),
        out_shape=jax.ShapeDtypeStruct((2, m, HEAD_DIM), F32),
        compiler_params=_params("parallel", "parallel"),
        name="compress",
    )(flat, lw["cmp_w1"], lw["cmp_b1"], lw["cmp_w2"], lw["cmp_b2"], lw["g_kc"])


def _flatten_blocks(cmp_in, b, n_blk):
    x = cmp_in.reshape(b, n_blk, CMP_BLOCK, 2, N_KV, HEAD_DIM)
    x = x.transpose(3, 0, 4, 1, 2, 5)
    return x.reshape(2, b * N_KV * n_blk, CMP_BLOCK * HEAD_DIM)


def _page_copies(pt_ref, cache_ref, buf_ref, sem_ref, layer, slot, seq, buf_slot, n_pages):
    rows = N_KV * HEAD_DIM
    return [pltpu.make_async_copy(cache_ref.at[layer, pt_ref[seq, p], slot],
                                  buf_ref.at[buf_slot, pl.ds(p * rows, rows), :], sem_ref.at[buf_slot])
            for p in range(n_pages)]


def _compress_paged_kernel(pt_ref, cache_ref, w_ref, pe_ref, b1_ref, w2_ref, b2_ref, gkc_ref, seg_ref, o_ref,
                           buf_ref, sem_ref, *, n_pages, n_seq):
    n_rows = n_pages * N_KV
    step = (pl.program_id(0) * 2 + pl.program_id(1)) * n_seq + pl.program_id(2)
    n_steps = pl.num_programs(0) * 2 * n_seq
    cur = step % 2

    def copies(s, buf_slot):
        seq = s % n_seq
        return _page_copies(pt_ref, cache_ref, buf_ref, sem_ref, s // (2 * n_seq), (s // n_seq) % 2, seq, buf_slot,
                            n_pages)

    @pl.when(step == 0)
    def _():
        for cp in copies(step, cur):
            cp.start()

    @pl.when(step + 1 < n_steps)
    def _():
        for cp in copies(step + 1, 1 - cur):
            cp.start()

    for cp in copies(step, cur):
        cp.wait()

    acc = jnp.zeros((n_rows, w_ref.shape[-1]), F32)
    for d in range(0, HEAD_DIM, 2):
        a = jnp.concatenate(
            [(buf_ref[cur, pl.ds(d + u, n_rows, stride=HEAD_DIM), :] + pe_ref[d + u:d + u + 1, :]).astype(BF16)
             for u in range(2)], axis=1)
        acc = acc + _dot(a, w_ref[d // 2])
    hid = jax.nn.gelu(acc + b1_ref[...])
    o = _dot(hid.astype(BF16), w2_ref[...]) + b2_ref[...]
    is_key = pl.program_id(1) == 0
    o_ref[...] = jnp.where(is_key, _head_rms(o, seg_ref[...]) * gkc_ref[...], o)


def _compress_paged(page_table, cache_pages, pw):
    depth = cache_pages.shape[0]
    db, n_pages = page_table.shape
    n_rows = n_pages * N_KV
    hid2 = pw["w1"].shape[-1]
    sel = lambda *s: pl.BlockSpec((None, None) + s, lambda l, sl, b, pt: (l, sl) + (0,) * len(s))
    grid_spec = pltpu.PrefetchScalarGridSpec(
        num_scalar_prefetch=1,
        grid=(depth, 2, db),
        in_specs=[pl.BlockSpec(memory_space=pl.ANY),
                  sel(HEAD_DIM // 2, 2 * PAGE_SIZE, hid2), sel(HEAD_DIM, PAGE_SIZE), sel(1, hid2), sel(hid2, LANES),
                  sel(1, LANES), pl.BlockSpec((None, 1, LANES), lambda l, sl, b, pt: (l, 0, 0)),
                  pl.BlockSpec((LANES, LANES), lambda l, sl, b, pt: (0, 0))],
        out_specs=pl.BlockSpec((None, None, None, n_rows, LANES), lambda l, sl, b, pt: (l, sl, b, 0, 0)),
        scratch_shapes=[pltpu.VMEM((2, n_rows * HEAD_DIM, PAGE_SIZE), F32), pltpu.SemaphoreType.DMA((2,))])
    return pl.pallas_call(
        functools.partial(_compress_paged_kernel, n_pages=n_pages, n_seq=db),
        grid_spec=grid_spec,
        out_shape=jax.ShapeDtypeStruct((depth, 2, db, n_rows, LANES), F32),
        compiler_params=_params("arbitrary", "arbitrary", "arbitrary"),
        name="compress_paged",
    )(page_table, cache_pages, pw["w1"], pw["pe"], pw["b1"], pw["w2"], pw["b2"], pw["g_kc"], pw["seg"])


def _attn_prompt_kernel(qn_ref, qr_ref, g3_ref, kc_ref, vc_ref, ks_ref, vst_ref, kw_ref, vwt_ref,
                        o_ref, bias_ref, ocmp_ref, ms_ref, accs_ref, mw_ref, accw_ref, *, tq):
    i = pl.program_id(1)
    t0 = i * tq
    lq = Q_PER_KV * tq
    lane = lax.broadcasted_iota(jnp.int32, (1, lq), 1)
    qpos = t0 + _mod(lane, tq)
    nb = kc_ref.shape[1]
    blk = lax.broadcasted_iota(jnp.int32, (nb, 1), 0)
    zpad = jnp.zeros((HEAD_DIM, lq), BF16)

    def heads(ref, g):
        return jnp.concatenate([ref[(Q_PER_KV * g + r) * HEAD_DIM:(Q_PER_KV * g + r + 1) * HEAD_DIM, :]
                                for r in range(Q_PER_KV)], axis=1)

    qpad = []
    for g in range(N_KV):
        q4 = heads(qr_ref, g)
        qpad.append(jnp.concatenate([q4, zpad] if g == 0 else [zpad, q4], axis=0))

        cm = blk < _div(qpos + 1, CMP_BLOCK)
        s = jnp.where(cm, _dot(kc_ref[g].astype(BF16), heads(qn_ref, g)), NEG)
        p = jnp.exp2(s - jnp.max(s, axis=0, keepdims=True)) * cm.astype(F32)
        p = p / jnp.maximum(jnp.sum(p, axis=0, keepdims=True), 1e-30)
        ocmp_ref[g] = _dot_t0(vc_ref[g].astype(BF16), p.astype(BF16))

        imp = p[:, :tq]
        for r in range(1, Q_PER_KV):
            imp = imp + p[:, r * tq:(r + 1) * tq]
        imp = _block_importance(imp, blk, _div(qpos[:, :tq], CMP_BLOCK))
        bias_ref[g] = (_topk_mask(imp, N_SELECT) - 1.0) * (-NEG)

    for m_ref, acc_ref in ((ms_ref, accs_ref), (mw_ref, accw_ref)):
        m_ref[...] = jnp.full(m_ref.shape, NEG, F32)
        acc_ref[...] = jnp.zeros(acc_ref.shape, F32)

    kidx = lax.broadcasted_iota(jnp.int32, (KEY_CHUNK, 1), 0)

    tpos = qpos[:, :tq]

    chains = [(g, slice(r * tq, (r + 1) * tq)) for g in range(N_KV) for r in range(Q_PER_KV)]

    def qk(k_ref, kc0, n_keys):
        kblk = k_ref[pl.ds(kc0, n_keys), :]
        return tuple(_dot(kblk, qpad[g][:, sl]) for g, sl in chains)

    def softmax_pv(scores, bias, vt_ref, kc0, n_keys, m_ref, acc_ref):
        m_all, acc_all = m_ref[...], acc_ref[...]
        m_out, alphas, probs = [], [], []
        for (g, sl), s in zip(chains, scores):
            s = s + bias[g]
            m = m_all[g, :, sl]
            m_new = jnp.maximum(m, jnp.max(s, axis=0, keepdims=True))
            m_out.append(m_new)
            alphas.append(jnp.exp2(m - m_new))
            probs.append(jnp.exp2(s - m_new).astype(BF16))
        vts = [vt_ref[g * VT_ROWS:(g + 1) * VT_ROWS, pl.ds(kc0, n_keys)] for g in range(N_KV)]
        acc_out = [alpha * acc_all[g, :, sl] + _dot(vts[g], p)
                   for (g, sl), alpha, p in zip(chains, alphas, probs)]
        gather = lambda parts: jnp.stack([jnp.concatenate(parts[g * Q_PER_KV:(g + 1) * Q_PER_KV], axis=1)
                                          for g in range(N_KV)])
        m_ref[...] = gather(m_out)
        acc_ref[...] = gather(acc_out)

    kidx2 = lax.broadcasted_iota(jnp.int32, (SEL_CHUNK, 1), 0)

    def sel_chunk(c, causal):
        k0 = pl.multiple_of(c * SEL_CHUNK, SEL_CHUNK)
        bias = []
        for g in range(N_KV):
            rows8 = bias_ref[g, pl.ds(pl.multiple_of(c * 8, 8), 8), :]
            b = jnp.concatenate([jnp.broadcast_to(rows8[j:j + 1, :], (CMP_BLOCK, tq)) for j in range(8)], axis=0)
            bias.append(jnp.where(k0 + kidx2 <= tpos, b, NEG) if causal else b)
        softmax_pv(qk(ks_ref, k0, SEL_CHUNK), bias, vst_ref, k0, SEL_CHUNK, ms_ref, accs_ref)

    def sel_body(c, carry):
        sel_chunk(c, False)
        return carry

    last = (t0 + tq - 1) // SEL_CHUNK
    lax.fori_loop(0, last, sel_body, 0)
    sel_chunk(last, True)

    n_win = WINDOW // KEY_CHUNK
    for c in range(n_win + 1):
        k0 = t0 - WINDOW + c * KEY_CHUNK
        start = pl.multiple_of(jnp.maximum(k0, 0), KEY_CHUNK)
        if c == n_win:
            bias_w = jnp.where(k0 + kidx <= tpos, 0.0, NEG)
        elif c == 0:
            bias_w = jnp.where(tpos - (k0 + kidx) < jnp.where(k0 >= 0, WINDOW, -(2 ** 30)), 0.0, NEG)
        else:
            bias_w = jnp.broadcast_to(jnp.where(k0 >= 0, 0.0, NEG), (1, 1))
        softmax_pv(qk(kw_ref, start, KEY_CHUNK), [bias_w] * N_KV, vwt_ref, start, KEY_CHUNK, mw_ref, accw_ref)

    for g in range(N_KV):
        gate = lambda br: jnp.concatenate(
            [g3_ref[br * N_HEADS + Q_PER_KV * g + r:br * N_HEADS + Q_PER_KV * g + r + 1, :]
             for r in range(Q_PER_KV)], axis=1)
        branch = lambda acc_ref: acc_ref[g, :HEAD_DIM, :] / acc_ref[g, HEAD_DIM:HEAD_DIM + 1, :]
        o = gate(0) * ocmp_ref[g] + gate(1) * branch(accs_ref) + gate(2) * branch(accw_ref)
        for r in range(Q_PER_KV):
            o_ref[(Q_PER_KV * g + r) * HEAD_DIM:(Q_PER_KV * g + r + 1) * HEAD_DIM, :] = (
                o[:, r * tq:(r + 1) * tq].astype(BF16))


def _attn_prompt(qn_t, qr_t, g3_t, kc, vc, ks, vst, kw, vwt, b, t):
    tq = KEY_CHUNK
    nq = t // tq
    lq = Q_PER_KV * tq
    n_cmp = kc.shape[2]
    qcol = lambda h: pl.BlockSpec((h, tq), lambda bi, i: (0, bi * nq + i))
    per_b = pl.BlockSpec((None, N_KV, n_cmp, HEAD_DIM), lambda bi, i: (bi, 0, 0, 0))
    keys = pl.BlockSpec((t, LANES), lambda bi, i: (bi, 0))
    vals = pl.BlockSpec((N_KV * VT_ROWS, t), lambda bi, i: (0, bi))
    st = lambda r: pltpu.VMEM((N_KV, r, lq), F32)
    return pl.pallas_call(
        functools.partial(_attn_prompt_kernel, tq=tq),
        grid=(b, nq),
        in_specs=[qcol(512), qcol(512), qcol(3 * N_HEADS), per_b, per_b, keys, vals, keys, vals],
        out_specs=qcol(512),
        out_shape=jax.ShapeDtypeStruct((512, b * t), BF16),
        scratch_shapes=[pltpu.VMEM((N_KV, n_cmp, tq), F32), st(HEAD_DIM),
                        st(1), st(VT_ROWS), st(1), st(VT_ROWS)],
        compiler_params=_params("parallel", "arbitrary"),
        name="attn_prompt",
    )(qn_t, qr_t, g3_t, kc, vc, ks, vst, kw, vwt)


def _attn_sample_kernel(pt_ref, *refs, n_pages, n_blk, pos0, tq):
    pages = refs[:ATTN_PAGES]
    (q_ref, qnt_ref, kc_ref, vc_ref, kn_ref, vn_ref, kwn_ref, vwn_ref, win_ref, g3_ref, expand_ref, rsum_ref,
     o_ref, sel_ref, m_ref, l_ref, acc_ref, ocmp_ref) = refs[ATTN_PAGES:]
    del pt_ref
    j = pl.program_id(1)
    lq = Q_PER_KV * tq
    nbp = sel_ref.shape[1]
    n_cmp = kc_ref.shape[1]
    step_keys = ATTN_PAGES * PAGE_SIZE

    @pl.when(j == 0)
    def _():
        lane = lax.broadcasted_iota(jnp.int32, (1, lq), 1)
        qpos = pos0 + _mod(lane, tq)
        blk = lax.broadcasted_iota(jnp.int32, (nbp, 1), 0)
        for g in range(N_KV):
            cm = blk[:n_cmp] < _div(qpos + 1, CMP_BLOCK)
            s = jnp.where(cm, _dot(kc_ref[g].astype(BF16), qnt_ref[g]), NEG)
            p = jnp.exp2(s - jnp.max(s, axis=0, keepdims=True)) * cm.astype(F32)
            p = p / jnp.maximum(jnp.sum(p, axis=0, keepdims=True), 1e-30)
            ocmp_ref[g] = _dot_t0(p.astype(BF16), vc_ref[g].astype(BF16))
            imp = _split_dot(p, rsum_ref[...])
            imp = jnp.concatenate([imp, jnp.zeros((nbp - n_cmp, lq), F32)], axis=0)
            imp = _block_importance(imp, blk, _div(qpos, CMP_BLOCK))
            imp = jnp.where(blk >= n_blk, -jnp.inf, imp)
            sel_ref[g] = _topk_mask(imp, N_SELECT)
        m_ref[...] = jnp.full(m_ref.shape, NEG, F32)
        l_ref[...] = jnp.zeros(l_ref.shape, F32)
        acc_ref[...] = jnp.zeros(acc_ref.shape, F32)

    def update(g, s, pv):
        m = m_ref[g]
        m_new = jnp.maximum(m, jnp.max(s, axis=1, keepdims=True))
        alpha = jnp.exp2(m - m_new)
        p = jnp.exp2(s - m_new)
        l_ref[g] = alpha * l_ref[g] + jnp.sum(p, axis=1, keepdims=True)
        acc_ref[g] = alpha * acc_ref[g] + pv(p)
        m_ref[g] = m_new

    step_blocks = step_keys // CMP_BLOCK
    expand = expand_ref[...]
    scores = []
    for g in range(N_KV):
        kt = jnp.concatenate([pg[0, g].astype(BF16) for pg in pages], axis=1)
        sel_rows = sel_ref[g, pl.ds(pl.multiple_of(j * step_blocks, step_blocks), step_blocks), :]
        picked = _dot_t0(sel_rows.astype(BF16), expand)
        scores.append(_dot(q_ref[g], kt) + (picked - 1.0) * (-NEG))
    m_all, l_all, acc_all = m_ref[...], l_ref[...], acc_ref[...]
    m_out, l_out, alphas, probs = [], [], [], []
    for g, s in enumerate(scores):
        m_new = jnp.maximum(m_all[g], jnp.max(s, axis=1, keepdims=True))
        alpha = jnp.exp2(m_all[g] - m_new)
        p = jnp.exp2(s - m_new)
        m_out.append(m_new)
        l_out.append(alpha * l_all[g] + jnp.sum(p, axis=1, keepdims=True))
        alphas.append(alpha)
        probs.append(p.astype(BF16))
    acc_out = []
    for g in range(N_KV):
        vt = jnp.concatenate([pg[1, g].astype(BF16) for pg in pages], axis=1)
        acc_out.append(alphas[g] * acc_all[g] + _dot_t1(probs[g], vt))
    m_ref[...] = jnp.stack(m_out)
    l_ref[...] = jnp.stack(l_out)
    acc_ref[...] = jnp.stack(acc_out)

    @pl.when(j == n_pages // ATTN_PAGES - 1)
    def _():
        rowi = _mod(lax.broadcasted_iota(jnp.int32, (lq, 1), 0), tq)
        keyi = lax.broadcasted_iota(jnp.int32, (1, tq), 1)
        causal = keyi <= rowi
        wlane = lax.broadcasted_iota(jnp.int32, (1, WINDOW), 1)
        kpos = pos0 - WINDOW + wlane
        in_win = ((pos0 + rowi) - kpos < WINDOW) & (kpos >= 0)
        cur_blk = pos0 // CMP_BLOCK
        first_row = (lax.broadcasted_iota(jnp.int32, (8, tq), 0) == cur_blk % 8).astype(F32)
        for g in range(N_KV):
            q = q_ref[g]
            qf = q.astype(F32)
            base = (cur_blk // 8) * 8
            picked = _dot_t0(sel_ref[g, base:base + 8, :], first_row)
            s = jnp.where(causal & (picked > 0.5), _dot_t1(qf, kn_ref[g]), NEG)
            update(g, s, lambda p: _dot(p, vn_ref[g]))
            o_sel = acc_ref[g] / l_ref[g]

            s_pre = jnp.where(in_win, _dot(q, win_ref[0, g].astype(BF16)), NEG)
            s_new = jnp.where(causal, _dot_t1(qf, kwn_ref[g]), NEG)
            m = jnp.maximum(jnp.max(s_pre, axis=1, keepdims=True), jnp.max(s_new, axis=1, keepdims=True))
            p_pre = jnp.exp2(s_pre - m) * in_win.astype(F32)
            p_new = jnp.exp2(s_new - m) * causal.astype(F32)
            l = jnp.sum(p_pre, axis=1, keepdims=True) + jnp.sum(p_new, axis=1, keepdims=True)
            o_win = (_dot_t1(p_pre.astype(BF16), win_ref[1, g].astype(BF16)) + _dot(p_new, vwn_ref[g]))
            o_win = o_win / jnp.maximum(l, 1e-30)

            g3 = g3_ref[g]
            o_ref[g] = g3[:, 0:1] * ocmp_ref[g] + g3[:, 1:2] * o_sel + g3[:, 2:3] * o_win


def _attn_sample(page_table, cache_t, layer, q, qn_t, kc, vc, kn, vn, kwn, vwn, win_t, g3, expand, rsum, pos0, tq):
    db, n_pages = page_table.shape
    lq = Q_PER_KV * tq
    n_cmp = kc.shape[2]
    n_blk = -(-(pos0 + tq) // CMP_BLOCK)
    nbp = -(-n_blk // 8) * 8
    steps = n_pages // ATTN_PAGES

    def page_spec(k):
        return pl.BlockSpec((None, None, 2, N_KV, HEAD_DIM, PAGE_SIZE),
                            lambda b, j, pt: (layer, pt[b, j * ATTN_PAGES + k], 1, 0, 0, 0))

    per_b = lambda *s: pl.BlockSpec((None,) + s, lambda b, j, pt: (b,) + (0,) * len(s))
    win_spec = pl.BlockSpec((None, None, 2, N_KV, HEAD_DIM, WINDOW), lambda b, j, pt: (layer, b, 0, 0, 0, 0))
    const = lambda *s: pl.BlockSpec(s, lambda b, j, pt: (0,) * len(s))
    grid_spec = pltpu.PrefetchScalarGridSpec(
        num_scalar_prefetch=1,
        grid=(db, steps),
        in_specs=[page_spec(k) for k in range(ATTN_PAGES)] + [
            per_b(N_KV, lq, HEAD_DIM), per_b(N_KV, HEAD_DIM, lq), per_b(N_KV, n_cmp, HEAD_DIM),
            per_b(N_KV, n_cmp, HEAD_DIM), per_b(N_KV, tq, HEAD_DIM), per_b(N_KV, tq, HEAD_DIM),
            per_b(N_KV, tq, HEAD_DIM), per_b(N_KV, tq, HEAD_DIM), win_spec,
            per_b(N_KV, lq, 8), const(ATTN_PAGES * PAGE_SIZE // CMP_BLOCK, ATTN_PAGES * PAGE_SIZE), const(lq, lq)],
        out_specs=per_b(N_KV, lq, HEAD_DIM),
        scratch_shapes=[pltpu.VMEM((N_KV, nbp, lq), F32), pltpu.VMEM((N_KV, lq, 1), F32),
                        pltpu.VMEM((N_KV, lq, 1), F32), pltpu.VMEM((N_KV, lq, HEAD_DIM), F32),
                        pltpu.VMEM((N_KV, lq, HEAD_DIM), F32)])
    return pl.pallas_call(
        functools.partial(_attn_sample_kernel, n_pages=n_pages, n_blk=n_blk, pos0=pos0, tq=tq),
        grid_spec=grid_spec,
        out_shape=jax.ShapeDtypeStruct((db, N_KV, lq, HEAD_DIM), F32),
        compiler_params=_params("parallel", "arbitrary"),
        name="attn_sample",
    )(page_table, *([cache_t] * ATTN_PAGES), q, qn_t, kc, vc, kn, vn, kwn, vwn, win_t, g3, expand, rsum)


def _merge_kernel(x_ref, y_ref, o_ref, gm_ref, gt_ref, sc_ref, sh_ref, gn_ref, wc_ref, wa_ref, wo_ref,
                  wr_ref, br_ref, x1_ref, h2_ref, dw_ref, *, o_transposed):
    tm = x_ref.shape[0]
    halves = [slice(h * (tm // 2), (h + 1) * (tm // 2)) for h in range(2)]
    rows_of = lambda ref, rows: ref[...] if ref.shape[0] == 1 else ref[rows, :]

    branch = []
    for rows in halves:
        attn = _dot_t0(o_ref[:, rows], wa_ref[...]) if o_transposed else _dot(o_ref[rows, :], wa_ref[...])
        branch.append((_dot(y_ref[rows, :], wc_ref[...]), attn))
    mixed = []
    for rows, (conv, attn) in zip(halves, branch):
        gm = gm_ref[rows, :]
        merged = gm[:, :D_MODEL].astype(F32) * conv + gm[:, D_MODEL:].astype(F32) * attn
        mixed.append(_dot(merged.astype(BF16), wo_ref[...]))
    x1s, h2s, affs = [], [], []
    for rows, mix in zip(halves, mixed):
        x1 = x_ref[rows, :] + rows_of(gt_ref, rows) * mix
        h2 = _rms(x1) * gn_ref[...]
        h2 = h2 * (1.0 + rows_of(sc_ref, rows)) + rows_of(sh_ref, rows)
        h_hi = h2.astype(BF16)
        h_lo = (h2 - h_hi.astype(F32)).astype(BF16)
        x1s.append(x1)
        h2s.append(h_hi)
        affs.append(_sigmoid(_dot(h_hi, wr_ref[0]) + _dot(h_lo, wr_ref[0]) + _dot(h_hi, wr_ref[1])))
    x1_ref[...] = jnp.concatenate(x1s, axis=0)
    h2_ref[...] = jnp.concatenate(h2s, axis=0)

    dws = []
    for aff in affs:
        score = aff + br_ref[...]
        e = lax.broadcasted_iota(jnp.int32, score.shape, 1)
        grp = _div(e, EXPERTS_PER_GROUP)
        e = e.astype(F32)
        big = float(LANES)

        def top2(j):
            vals = jnp.where(grp == j, score, -jnp.inf)
            m1 = jnp.max(vals, axis=-1, keepdims=True)
            i1 = jnp.min(jnp.where(vals == m1, e, big), axis=-1, keepdims=True)
            vals = jnp.where(e == i1, -jnp.inf, vals)
            m2 = jnp.max(vals, axis=-1, keepdims=True)
            i2 = jnp.min(jnp.where(vals == m2, e, big), axis=-1, keepdims=True)
            return m1 + m2, i1, i2

        best, ia, ib = top2(0)
        for j in range(1, N_GROUPS):
            gs, i1, i2 = top2(j)
            better = gs > best
            best = jnp.where(better, gs, best)
            ia = jnp.where(better, i1, ia)
            ib = jnp.where(better, i2, ib)
        hit_a = e == ia
        hit_b = e == ib
        aff_a = jnp.sum(jnp.where(hit_a, aff, 0.0), axis=-1, keepdims=True)
        aff_b = jnp.sum(jnp.where(hit_b, aff, 0.0), axis=-1, keepdims=True)
        tot = aff_a + aff_b
        dws.append(jnp.where(hit_a, aff_a / tot, 0.0) + jnp.where(hit_b, aff_b / tot, 0.0))
    dw_ref[...] = jnp.concatenate(dws, axis=0)


def _merge(x, y, o, gm, mods, lw, shared, rows_per_batch, tm, o_transposed):
    n = x.shape[0]
    gt, gt_spec = _mod_operand(mods["gt1"], rows_per_batch, tm)
    sc, sc_spec = _mod_operand(mods["sc2"], rows_per_batch, tm)
    sh, sh_spec = _mod_operand(mods["sh2"], rows_per_batch, tm)
    row = lambda w: pl.BlockSpec((tm, w), lambda i: (i, 0))
    o_spec = pl.BlockSpec((512, tm), lambda i: (0, i)) if o_transposed else row(512)
    return pl.pallas_call(
        functools.partial(_merge_kernel, o_transposed=o_transposed),
        grid=(n // tm,),
        in_specs=[row(D_MODEL), row(D_CONV), o_spec, row(2 * D_MODEL), gt_spec, sc_spec, sh_spec,
                  _const_spec((1, D_MODEL)), _const_spec((D_CONV, D_MODEL)), _const_spec((512, D_MODEL)),
                  _const_spec((D_MODEL, D_MODEL)), _const_spec((2, D_MODEL, LANES)), _const_spec((1, LANES))],
        out_specs=[row(D_MODEL), row(D_MODEL), row(LANES)],
        out_shape=[jax.ShapeDtypeStruct((n, D_MODEL), F32), jax.ShapeDtypeStruct((n, D_MODEL), BF16),
                   jax.ShapeDtypeStruct((n, LANES), F32)],
        compiler_params=_params("parallel"),
        name="merge",
    )(x, y, o, gm, gt, sc, sh, lw["g_norm_ffn"], lw["w_conv_out"], lw["w_attn_out"], lw["w_out"],
      shared["w_router"], shared["b_router"])


def _moe_kernel(h_ref, dw_ref, x_ref, gt_ref, wg_ref, wu_ref, wd_ref, o_ref, acc_ref):
    ex = pl.program_id(1)

    @pl.when(ex == 0)
    def _():
        acc_ref[...] = jnp.zeros(acc_ref.shape, F32)

    h = h_ref[...]
    dw = dw_ref[...]
    lane = lax.broadcasted_iota(jnp.int32, dw.shape, 1)
    part = None
    for j in range(MOE_EXPERTS_PER_STEP):
        w_col = jnp.sum(jnp.where(lane == ex * MOE_EXPERTS_PER_STEP + j, dw, 0.0), axis=-1, keepdims=True)
        hid = _silu(_dot(h, wg_ref[j])) * _dot(h, wu_ref[j]) * w_col
        out = _dot(hid.astype(BF16), wd_ref[j])
        part = out if part is None else part + out
    acc_ref[...] += part

    @pl.when(ex == N_EXPERTS // MOE_EXPERTS_PER_STEP - 1)
    def _():
        o_ref[...] = x_ref[...] + gt_ref[...] * acc_ref[...]


def _moe(h2, dw, x1, mods, lw, rows_per_batch, tm):
    n = h2.shape[0]
    gt, gt_spec = _mod_operand(mods["gt2"], rows_per_batch, tm)
    gt_spec = pl.BlockSpec(gt_spec.block_shape, lambda i, ex, f=gt_spec.index_map: f(i))
    row = lambda w: pl.BlockSpec((tm, w), lambda i, ex: (i, 0))
    return pl.pallas_call(
        _moe_kernel,
        grid=(n // tm, N_EXPERTS // MOE_EXPERTS_PER_STEP),
        in_specs=[row(D_MODEL), row(LANES), row(D_MODEL), gt_spec,
                  pl.BlockSpec((MOE_EXPERTS_PER_STEP, D_MODEL, D_EXPERT), lambda i, ex: (ex, 0, 0)),
                  pl.BlockSpec((MOE_EXPERTS_PER_STEP, D_MODEL, D_EXPERT), lambda i, ex: (ex, 0, 0)),
                  pl.BlockSpec((MOE_EXPERTS_PER_STEP, D_EXPERT, D_MODEL), lambda i, ex: (ex, 0, 0))],
        out_specs=row(D_MODEL),
        out_shape=jax.ShapeDtypeStruct((n, D_MODEL), F32),
        scratch_shapes=[pltpu.VMEM((tm, D_MODEL), F32)],
        compiler_params=_params("parallel", "arbitrary"),
        name="moe",
    )(h2, dw, x1, gt, lw["w_exp_gate"], lw["w_exp_up"], lw["w_exp_down"])


def _rope_tables(pos):
    half = HEAD_DIM // 2
    inv = ROPE_THETA ** (-jnp.arange(half, dtype=F32) / half)
    ang = pos.astype(F32)[:, None] * inv[None, :]
    cos, sin = jnp.cos(ang), jnp.sin(ang)
    cos = jnp.concatenate([cos, cos], axis=1)
    sin = jnp.concatenate([-sin, sin], axis=1)
    return jnp.concatenate([cos, cos], axis=1), jnp.concatenate([sin, sin], axis=1)


def _split_mod(mod):
    names = ("sh1", "sc1", "gt1", "sh2", "sc2", "gt2")
    return dict(zip(names, jnp.split(mod, 6, axis=-1)))


def _paged_compress_weights(cmp_pe, cmp_w1, cmp_b1, cmp_w2, cmp_b2, g_kc, seg):
    depth = cmp_w1.shape[0]
    hidden = cmp_w1.shape[-1]
    eye = jnp.eye(2, dtype=BF16)
    w1 = cmp_w1.astype(BF16).reshape(depth, 2, CMP_BLOCK, HEAD_DIM, hidden).transpose(0, 1, 3, 2, 4)
    w1 = jnp.einsum("lsdph,jk->lsdjpkh", w1, eye).reshape(depth, 2, HEAD_DIM // 2, 2 * PAGE_SIZE, 2 * hidden)
    w2 = jnp.einsum("lshd,jk->lsjhkd", cmp_w2.astype(BF16), eye).reshape(depth, 2, 2 * hidden, LANES)
    pe = jnp.tile(cmp_pe.transpose(0, 1, 3, 2), (1, 1, 1, 2))
    return dict(w1=w1, w2=w2, pe=pe,
                b1=jnp.tile(cmp_b1, (1, 1, 2))[:, :, None, :], b2=jnp.tile(cmp_b2, (1, 1, 2))[:, :, None, :],
                g_kc=jnp.tile(g_kc, (1, 2))[:, None, :], seg=seg[:LANES, :LANES])


def kernel(x_prompt, x_sample, c_prompt, c_sample, cache_kv, state_win_kv, state_conv, page_table, w_ada, b_ada, g_norm_mix, g_norm_ffn, w_in, w_dw, b_dw, ln_conv_g, ln_conv_b, w_conv_out, g_q, g_kc, g_ks, g_kw, cmp_pe, cmp_w1, cmp_b1, cmp_w2, cmp_b2, w_attn_out, w_out, w_router, b_router, w_exp_gate, w_exp_up, w_exp_down):
    depth = w_in.shape[0]
    b, t, _ = x_prompt.shape
    db, dt, _ = x_sample.shape
    n_pages = page_table.shape[1]
    past_len = n_pages * PAGE_SIZE
    win_buf = state_win_kv.shape[2]
    tm_p, tm_s = 512, db * dt
    lq_s = Q_PER_KV * dt
    n_cmp_s = (past_len + dt) // CMP_BLOCK
    assert t % tm_p == 0 and t % KEY_CHUNK == 0 and win_buf == WINDOW and WINDOW % KEY_CHUNK == 0
    assert n_cmp_s * CMP_BLOCK == past_len and n_pages % ATTN_PAGES == 0

    c0, c1, c2, c3 = 2 * D_CONV, 2 * D_CONV + 512, 2 * D_CONV + 512 + 768, 2 * D_CONV + 512 + 768 + 3 * N_HEADS
    w_in_b = w_in.astype(BF16)
    w_dw_p = jnp.pad(w_dw, ((0, 0), (0, CONV_HALO - CONV_WIDTH), (0, 0)))
    pe256 = cmp_pe[:, :, :, None, :].repeat(N_KV, axis=3).transpose(0, 2, 1, 3, 4).reshape(depth, CMP_BLOCK, 256)
    pe_tile = jnp.tile(pe256, (1, tm_p // CMP_BLOCK, 1))
    layers = []
    for l in range(depth):
        layers.append(dict(
            g_norm_mix=g_norm_mix[l][None], g_norm_ffn=g_norm_ffn[l][None],
            w_u=w_in_b[l][:, :c0], w_q=w_in_b[l][:, c0:c1], w_kv=w_in_b[l][:, c1:c2],
            w_g=jnp.pad(w_in_b[l][:, c2:c3], ((0, 0), (0, LANES - 3 * N_HEADS))), w_m=w_in_b[l][:, c3:],
            g_q=jnp.tile(g_q[l], N_HEADS)[None], g_ks=jnp.tile(g_ks[l], N_KV)[None], g_kw=jnp.tile(g_kw[l], N_KV)[None],
            g_kc=g_kc[l][None], pe_tile=pe_tile[l],
            w_dw=w_dw_p[l], b_dw=b_dw[l][None], ln_conv_g=ln_conv_g[l][None], ln_conv_b=ln_conv_b[l][None],
            w_conv_out=w_conv_out[l].astype(BF16), w_attn_out=w_attn_out[l].astype(BF16), w_out=w_out[l].astype(BF16),
            cmp_w1=cmp_w1[l].astype(BF16), cmp_b1=cmp_b1[l][:, None, :], cmp_w2=cmp_w2[l].astype(BF16),
            cmp_b2=cmp_b2[l][:, None, :],
            w_exp_gate=w_exp_gate[l].astype(BF16), w_exp_up=w_exp_up[l].astype(BF16),
            w_exp_down=w_exp_down[l].astype(BF16)))
    wr = jnp.pad(w_router, ((0, 0), (0, LANES - N_EXPERTS)))
    wr_hi = wr.astype(BF16)
    wr_lo = (wr - wr_hi.astype(F32)).astype(BF16)
    shared = dict(w_router=jnp.stack([wr_hi, wr_lo]),
                  b_router=jnp.pad(b_router, (0, LANES - N_EXPERTS))[None])
    seg = (jnp.arange(512)[:, None] // HEAD_DIM == jnp.arange(512)[None, :] // HEAD_DIM).astype(BF16)
    cos_p, sin_p = _rope_tables(jnp.arange(t, dtype=jnp.int32))
    cos_s, sin_s = _rope_tables(past_len + jnp.arange(dt, dtype=jnp.int32))
    tabs_p = dict(cos=cos_p, sin=sin_p, seg=seg)
    tabs_s = dict(cos=jnp.tile(cos_s, (db, 1)), sin=jnp.tile(sin_s, (db, 1)), seg=seg)
    rsum = (jnp.arange(lq_s)[:, None] % dt == jnp.arange(lq_s)[None, :] % dt).astype(BF16)
    step_keys = ATTN_PAGES * PAGE_SIZE
    expand = (jnp.arange(step_keys // CMP_BLOCK)[:, None] == jnp.arange(step_keys)[None, :] // CMP_BLOCK).astype(BF16)

    rows = b + db
    rows_p = -(-rows // 8) * 8
    c_all = jnp.pad(jnp.concatenate([c_prompt, c_sample], axis=0), ((0, rows_p - rows), (0, 0)))
    mod_all = _ada_all_layers(c_all, w_ada.astype(BF16), b_ada[:, None, :])

    cache_t = cache_kv.transpose(0, 1, 3, 4, 5, 2)
    win_t = state_win_kv.transpose(0, 1, 3, 4, 5, 2)

    paged = _compress_paged(page_table, cache_t.reshape(depth, -1, 4, N_KV * HEAD_DIM, PAGE_SIZE),
                            _paged_compress_weights(cmp_pe, cmp_w1, cmp_b1, cmp_w2, cmp_b2, g_kc, seg))
    paged = paged.reshape(depth, 2, db, n_pages, N_KV, 2, HEAD_DIM).transpose(0, 1, 2, 4, 3, 5, 6)
    paged = paged.reshape(depth, 2, db, N_KV, n_cmp_s, HEAD_DIM)

    xp = x_prompt.reshape(b * t, D_MODEL)
    xs = x_sample.reshape(db * dt, D_MODEL)
    outs = {k: [] for k in ("kv_p", "kv_s", "win_p", "win_s", "conv_p", "conv_s")}
    zero_prefix = jnp.zeros((b, CONV_HALO, D_CONV), F32)

    for l in range(depth):
        lw = layers[l]
        mods_p = _split_mod(mod_all[l, :b])
        mods_s = _split_mod(mod_all[l, b:b + db])

        (ug, gm, qn_t, qr_t, rows_t, win_tp, g3_t, cmp_in, ks, kw, vst, vwt) = _inproj(
            xp, mods_p, lw, tabs_p, t, tm_p, True)
        y = _conv_branch(ug.reshape(b, t, D_CONV), zero_prefix, lw, 512)
        n_blk = t // CMP_BLOCK
        kvc = _compress(_flatten_blocks(cmp_in, b, n_blk), lw).reshape(2, b, N_KV, n_blk, HEAD_DIM)
        o_t = _attn_prompt(qn_t, qr_t, g3_t, kvc[0], kvc[1], ks, vst, kw, vwt, b, t)
        x1, h2, dw = _merge(xp, y.reshape(b * t, D_CONV), o_t, gm, mods_p, lw, shared, t, tm_p, True)
        xp = _moe(h2, dw, x1, mods_p, lw, t, tm_p)
        outs["kv_p"].append(rows_t.reshape(b, 4, N_KV, HEAD_DIM, t).transpose(0, 4, 1, 2, 3))
        outs["win_p"].append(win_tp.reshape(b, 2, N_KV, HEAD_DIM, t)[..., t - win_buf:].transpose(0, 4, 1, 2, 3))
        outs["conv_p"].append(ug.reshape(b, t, D_CONV)[:, t - (CONV_WIDTH - 1):])

        ug, gm, qn, qr, rows_new, win_new, g3 = _inproj(xs, mods_s, lw, tabs_s, dt, tm_s, False)
        ug3 = ug.reshape(db, dt, D_CONV)
        conv_ext = jnp.concatenate([state_conv[l], ug3], axis=1)
        prefix = jnp.pad(state_conv[l], ((0, 0), (CONV_HALO - (CONV_WIDTH - 1), 0), (0, 0)))
        y = _conv_branch(ug3, prefix, lw, dt)
        r5 = rows_new.reshape(db, dt, 4, N_KV, HEAD_DIM)
        w5 = win_new.reshape(db, dt, 2, N_KV, HEAD_DIM)
        per_group = lambda a: a.transpose(0, 2, 1, 3)
        q5 = lambda a: a.reshape(db, dt, N_KV, Q_PER_KV, HEAD_DIM)
        q_rows = q5(qr).transpose(0, 2, 3, 1, 4).reshape(db, N_KV, lq_s, HEAD_DIM)
        qn_cols = q5(qn).transpose(0, 2, 4, 3, 1).reshape(db, N_KV, HEAD_DIM, lq_s)
        g3_rows = g3[:, :3 * N_HEADS].reshape(db, dt, 3, N_KV, Q_PER_KV).transpose(0, 3, 4, 1, 2)
        g3_rows = jnp.pad(g3_rows.reshape(db, N_KV, lq_s, 3), ((0, 0), (0, 0), (0, 0), (0, 5)))
        o_s = _attn_sample(page_table, cache_t, l, q_rows, qn_cols, paged[l, 0], paged[l, 1],
                           per_group(r5[:, :, 2]), per_group(r5[:, :, 3]), per_group(w5[:, :, 0]),
                           per_group(w5[:, :, 1]), win_t, g3_rows, expand, rsum, past_len, dt)
        o = o_s.reshape(db, N_KV, Q_PER_KV, dt, HEAD_DIM).transpose(0, 3, 1, 2, 4).reshape(db * dt, 512).astype(BF16)
        x1, h2, dw = _merge(xs, y.reshape(db * dt, D_CONV), o, gm, mods_s, lw, shared, dt, tm_s, False)
        xs = _moe(h2, dw, x1, mods_s, lw, dt, tm_s)
        outs["kv_s"].append(r5)
        outs["win_s"].append(jnp.concatenate([state_win_kv[l], w5], axis=1)[:, dt:])
        outs["conv_s"].append(conv_ext[:, dt:])

    return (xp.reshape(b, t, D_MODEL), xs.reshape(db, dt, D_MODEL),
            jnp.stack(outs["kv_p"]), jnp.stack(outs["kv_s"]), jnp.stack(outs["win_p"]), jnp.stack(outs["win_s"]),
            jnp.stack(outs["conv_p"]), jnp.stack(outs["conv_s"]))
```

```python
import functools

import jax
import jax.numpy as jnp
from jax import lax
from jax.experimental import pallas as pl
from jax.experimental.pallas import tpu as pltpu

F32 = jnp.float32
BF16 = jnp.bfloat16

D_MODEL = 1024
D_CONV = 512
CONV_WIDTH = 31
CONV_HALO = 32
CONV_PAD = 16
N_HEADS = 8
HEAD_DIM = 64
N_KV = 2
Q_PER_KV = 4
CMP_BLOCK = 64
N_SELECT = 16
WINDOW = 512
N_EXPERTS = 16
N_GROUPS = 4
EXPERTS_PER_GROUP = 4
D_EXPERT = 256
PAGE_SIZE = 128
ROPE_THETA = 10000.0
ATTN_SCALE = HEAD_DIM ** -0.5
FORCED_SCORE = 1e4
EPS = 1e-6
NEG = -1e30
LANES = 128
KEY_CHUNK = 256
SEL_CHUNK = 512
VT_ROWS = 80
ATTN_PAGES = 16
MOE_EXPERTS_PER_STEP = 4
LOG2E = 1.4426950408889634
VMEM_LIMIT = 56 * 1024 * 1024


def _params(*sem):
    return pltpu.CompilerParams(dimension_semantics=sem, vmem_limit_bytes=VMEM_LIMIT)


def _dot(a, b):
    return jnp.dot(a, b, preferred_element_type=F32)


def _dot_t0(a, b):
    return lax.dot_general(a, b, (((0,), (0,)), ((), ())), preferred_element_type=F32)


def _dot_t1(a, b):
    return lax.dot_general(a, b, (((1,), (1,)), ((), ())), preferred_element_type=F32)


def _shift(d):
    assert d & (d - 1) == 0
    return d.bit_length() - 1


def _div(x, d):
    return lax.shift_right_logical(x, jnp.int32(_shift(d)))


def _mod(x, d):
    return x & jnp.int32(d - 1)


def _sigmoid(x):
    return 1.0 / (1.0 + jnp.exp(-x))


def _silu(x):
    return x * _sigmoid(x)


def _rms(x):
    return x * lax.rsqrt(jnp.mean(x * x, axis=-1, keepdims=True) + EPS)


def _split_dot(x, w):
    hi = x.astype(BF16)
    lo = (x - hi.astype(F32)).astype(BF16)
    return _dot(hi, w) + _dot(lo, w)


def _head_rms(x, seg):
    ss = _split_dot(x * x, seg)
    return x * lax.rsqrt(ss * (1.0 / HEAD_DIM) + EPS)


def _rope(x, cos, sin_signed):
    w = x.shape[-1]
    lane = lax.broadcasted_iota(jnp.int32, x.shape, 1)
    first_half = _mod(lane, HEAD_DIM) < (HEAD_DIM // 2)
    partner = jnp.where(first_half, pltpu.roll(x, w - HEAD_DIM // 2, 1), pltpu.roll(x, HEAD_DIM // 2, 1))
    return x * cos + partner * sin_signed


def _topk_mask(imp, k):
    nb = imp.shape[0]
    blk = lax.broadcasted_iota(jnp.int32, imp.shape, 0).astype(F32)
    sel = jnp.zeros(imp.shape, F32)
    for _ in range(k):
        mx = jnp.max(imp, axis=0, keepdims=True)
        first = jnp.min(jnp.where(imp == mx, blk, float(nb)), axis=0, keepdims=True)
        hit = blk == first
        sel = jnp.where(hit, 1.0, sel)
        imp = jnp.where(hit, -jnp.inf, imp)
    return sel


def _block_importance(imp, blk, cur):
    return jnp.where((blk == cur) | (blk == 0), FORCED_SCORE, jnp.where(blk > cur, -1.0, imp))


def _ada_kernel(c_ref, w_ref, b_ref, o_ref):
    o_ref[...] = _dot(_silu(c_ref[...]).astype(BF16), w_ref[...]) + b_ref[...]


def _ada_all_layers(c_all, w_ada, b_ada):
    depth, rows = w_ada.shape[0], c_all.shape[0]
    tn = 1536
    return pl.pallas_call(
        _ada_kernel,
        grid=(depth, 6 * D_MODEL // tn),
        in_specs=[pl.BlockSpec((rows, D_MODEL), lambda l, j: (0, 0)),
                  pl.BlockSpec((None, D_MODEL, tn), lambda l, j: (l, 0, j)),
                  pl.BlockSpec((None, 1, tn), lambda l, j: (l, 0, j))],
        out_specs=pl.BlockSpec((None, rows, tn), lambda l, j: (l, 0, j)),
        out_shape=jax.ShapeDtypeStruct((depth, rows, 6 * D_MODEL), F32),
        compiler_params=_params("parallel", "parallel"),
        name="ada",
    )(c_all, w_ada, b_ada)


def _mod_operand(mod, rows_per_batch, tm):
    if rows_per_batch % tm == 0:
        per = rows_per_batch // tm
        return mod[:, None, :], pl.BlockSpec((None, 1, D_MODEL), lambda i: (i // per, 0, 0))
    return jnp.repeat(mod, rows_per_batch, axis=0), pl.BlockSpec((tm, D_MODEL), lambda i: (i, 0))


def _const_spec(shape):
    nd = len(shape)
    return pl.BlockSpec(shape, lambda i: (0,) * nd)


def _inproj_kernel(x_ref, gn_ref, sc_ref, sh_ref, wu_ref, wq_ref, wkv_ref, wg_ref, wm_ref,
                   gq_ref, gks_ref, gkw_ref, cos_ref, sin_ref, seg_ref, pe_ref, ug_ref, gm_ref, *out, transposed):
    h = _rms(x_ref[...]) * gn_ref[...]
    h = h * (1.0 + sc_ref[...]) + sh_ref[...]
    hb = h.astype(BF16)

    u = _dot(hb, wu_ref[...])
    ug_ref[...] = u[:, :D_CONV] * _sigmoid(u[:, D_CONV:])
    gm_ref[...] = _sigmoid(_dot(hb, wm_ref[...])).astype(BF16)
    g3 = _sigmoid(_dot(hb, wg_ref[...]))

    cos1, sin1 = cos_ref[...], sin_ref[...]
    cos4 = jnp.concatenate([cos1] * 4, axis=1)
    sin4 = jnp.concatenate([sin1] * 4, axis=1)
    qn = _head_rms(_dot(hb, wq_ref[...]), seg_ref[...]) * gq_ref[...]
    qr = _rope(qn, cos4, sin4) * (ATTN_SCALE * LOG2E)
    qn = qn * (ATTN_SCALE * LOG2E)

    kv = _dot(hb, wkv_ref[...])
    seg1 = seg_ref[:LANES, :LANES]
    ks = _rope(_head_rms(kv[:, 2 * LANES:3 * LANES], seg1) * gks_ref[...], cos1, sin1)
    kw = _rope(_head_rms(kv[:, 4 * LANES:5 * LANES], seg1) * gkw_ref[...], cos1, sin1)
    vs = kv[:, 3 * LANES:4 * LANES]
    vw = kv[:, 5 * LANES:]

    if not transposed:
        qn_ref, qr_ref, rows_ref, win_ref, g3_ref = out
        qn_ref[...] = qn.astype(BF16)
        qr_ref[...] = qr.astype(BF16)
        rows_ref[:, :2 * LANES] = kv[:, :2 * LANES]
        rows_ref[:, 2 * LANES:3 * LANES] = ks
        rows_ref[:, 3 * LANES:] = vs
        win_ref[:, :LANES] = kw
        win_ref[:, LANES:] = vw
        g3_ref[...] = g3
        return

    qnt_ref, qrt_ref, rowst_ref, wint_ref, g3t_ref, cmp_ref, ks_ref, kw_ref, vst_ref, vwt_ref = out
    qnt_ref[...] = qn.T.astype(BF16)
    qrt_ref[...] = qr.T.astype(BF16)
    rowst_ref[:2 * LANES, :] = kv[:, :2 * LANES].T
    rowst_ref[2 * LANES:3 * LANES, :] = ks.T
    vst = vs.T
    rowst_ref[3 * LANES:, :] = vst
    wint_ref[:LANES, :] = kw.T
    vwt = vw.T
    wint_ref[LANES:, :] = vwt
    g3t_ref[...] = g3.T[:3 * N_HEADS, :]
    cmp_ref[...] = (kv[:, :2 * LANES] + pe_ref[...]).astype(BF16)
    ks_ref[...] = ks.astype(BF16)
    kw_ref[...] = kw.astype(BF16)
    ones = jnp.ones((VT_ROWS - HEAD_DIM, vst.shape[1]), BF16)
    for ref, vt in ((vst_ref, vst), (vwt_ref, vwt)):
        for g in range(N_KV):
            ref[g * VT_ROWS:g * VT_ROWS + HEAD_DIM, :] = vt[g * HEAD_DIM:(g + 1) * HEAD_DIM, :].astype(BF16)
            ref[g * VT_ROWS + HEAD_DIM:(g + 1) * VT_ROWS, :] = ones


def _inproj(x, mods, lw, tabs, rows_per_batch, tm, transposed):
    n = x.shape[0]
    sc, sc_spec = _mod_operand(mods["sc1"], rows_per_batch, tm)
    sh, sh_spec = _mod_operand(mods["sh1"], rows_per_batch, tm)
    cos, sin, tab_tiles = tabs["cos"], tabs["sin"], tabs["cos"].shape[0] // tm
    tab_spec = pl.BlockSpec((tm, LANES), lambda i: (i % tab_tiles, 0))
    row = lambda w: pl.BlockSpec((tm, w), lambda i: (i, 0))
    col = lambda h: pl.BlockSpec((h, tm), lambda i: (0, i))
    sds = jax.ShapeDtypeStruct
    out_shape = [sds((n, D_CONV), F32), sds((n, 2 * D_MODEL), BF16)]
    out_specs = [row(D_CONV), row(2 * D_MODEL)]
    if transposed:
        per = rows_per_batch // tm
        nb = n // rows_per_batch
        bcol = lambda h: pl.BlockSpec((None, h, tm), lambda i: (i // per, 0, i % per))
        out_shape += [sds((512, n), BF16), sds((512, n), BF16), sds((nb, 512, rows_per_batch), F32),
                      sds((nb, 256, rows_per_batch), F32), sds((3 * N_HEADS, n), F32), sds((n, 256), BF16),
                      sds((n, LANES), BF16), sds((n, LANES), BF16), sds((N_KV * VT_ROWS, n), BF16),
                      sds((N_KV * VT_ROWS, n), BF16)]
        out_specs += [col(512), col(512), bcol(512), bcol(256), col(3 * N_HEADS), row(256),
                      row(LANES), row(LANES), col(N_KV * VT_ROWS), col(N_KV * VT_ROWS)]
    else:
        out_shape += [sds((n, 512), BF16), sds((n, 512), BF16), sds((n, 512), F32), sds((n, 256), F32),
                      sds((n, LANES), F32)]
        out_specs += [row(512), row(512), row(512), row(256), row(LANES)]
    return pl.pallas_call(
        functools.partial(_inproj_kernel, transposed=transposed),
        grid=(n // tm,),
        in_specs=[row(D_MODEL), _const_spec((1, D_MODEL)), sc_spec, sh_spec,
                  _const_spec(lw["w_u"].shape), _const_spec(lw["w_q"].shape), _const_spec(lw["w_kv"].shape),
                  _const_spec(lw["w_g"].shape), _const_spec(lw["w_m"].shape),
                  _const_spec((1, 512)), _const_spec((1, LANES)), _const_spec((1, LANES)),
                  tab_spec, tab_spec, _const_spec((512, 512)), _const_spec((tm, 256))],
        out_specs=out_specs,
        out_shape=out_shape,
        compiler_params=_params("parallel"),
        name="inproj",
    )(x, lw["g_norm_mix"], sc, sh, lw["w_u"], lw["w_q"], lw["w_kv"], lw["w_g"], lw["w_m"],
      lw["g_q"], lw["g_ks"], lw["g_kw"], cos, sin, tabs["seg"], lw["pe_tile"][:tm])


def _conv_kernel(*refs, tc, n_tiles):
    if n_tiles > 1:
        cur_ref, halo_ref, pre_ref, w_ref, b_ref, g_ref, beta_ref, y_ref, win_ref = refs
    else:
        cur_ref, pre_ref, w_ref, b_ref, g_ref, beta_ref, y_ref, win_ref = refs
    if n_tiles > 1:
        first = pl.program_id(1) == 0

        @pl.when(first)
        def _():
            win_ref[:CONV_HALO, :] = pre_ref[...]

        @pl.when(jnp.logical_not(first))
        def _():
            win_ref[:CONV_HALO, :] = halo_ref[...]
    else:
        win_ref[:CONV_HALO, :] = pre_ref[...]
    win_ref[CONV_HALO:CONV_HALO + tc, :] = cur_ref[...]
    win_ref[CONV_HALO + tc:, :] = jnp.zeros((CONV_PAD, D_CONV), F32)

    rc = min(tc, 128)
    off = CONV_HALO - (CONV_WIDTH - 1)
    ext = rc + CONV_PAD
    for r0 in range(0, tc, rc):
        acc = None
        for s in range(8):
            z = None
            for a in range(-(-CONV_WIDTH // 8)):
                j = 8 * a + s
                if j < CONV_WIDTH:
                    term = w_ref[j:j + 1, :] * win_ref[r0 + 8 * a:r0 + 8 * a + ext, :]
                    z = term if z is None else z + term
            piece = z[off + s:off + s + rc]
            acc = piece if acc is None else acc + piece
        y = acc + b_ref[...]
        mu = jnp.mean(y, axis=-1, keepdims=True)
        yc = y - mu
        var = jnp.mean(yc * yc, axis=-1, keepdims=True)
        y = yc * lax.rsqrt(var + EPS) * g_ref[...] + beta_ref[...]
        y_ref[r0:r0 + rc, :] = _silu(y).astype(BF16)


def _conv_branch(ug, prefix, lw, tc):
    b, t, _ = ug.shape
    n_tiles = t // tc
    cur_spec = pl.BlockSpec((None, tc, D_CONV), lambda bi, i: (bi, i, 0))
    pre_spec = pl.BlockSpec((None, CONV_HALO, D_CONV), lambda bi, i: (bi, 0, 0))
    vec = lambda r: pl.BlockSpec((r, D_CONV), lambda bi, i: (0, 0))
    ops, specs = [ug], [cur_spec]
    if n_tiles > 1:
        per = tc // CONV_HALO
        ops.append(ug)
        specs.append(pl.BlockSpec((None, CONV_HALO, D_CONV), lambda bi, i: (bi, jnp.maximum(i * per - 1, 0), 0)))
    ops += [prefix, lw["w_dw"], lw["b_dw"], lw["ln_conv_g"], lw["ln_conv_b"]]
    specs += [pre_spec, vec(CONV_HALO), vec(1), vec(1), vec(1)]
    return pl.pallas_call(
        functools.partial(_conv_kernel, tc=tc, n_tiles=n_tiles),
        grid=(b, n_tiles),
        in_specs=specs,
        out_specs=pl.BlockSpec((None, tc, D_CONV), lambda bi, i: (bi, i, 0)),
        out_shape=jax.ShapeDtypeStruct((b, t, D_CONV), BF16),
        scratch_shapes=[pltpu.VMEM((CONV_HALO + tc + CONV_PAD, D_CONV), F32)],
        compiler_params=_params("parallel", "parallel"),
        name="conv",
    )(*ops)


def _compress_kernel(x_ref, w1_ref, b1_ref, w2_ref, b2_ref, gkc_ref, o_ref):
    hid = jax.nn.gelu(_dot(x_ref[...], w1_ref[...]) + b1_ref[...])
    o = _dot(hid.astype(BF16), w2_ref[...]) + b2_ref[...]
    is_key = pl.program_id(0) == 0
    o_ref[...] = jnp.where(is_key, _rms(o) * gkc_ref[...], o)


def _compress(flat, lw):
    _, m, kdim = flat.shape
    tm = min(m, 512)
    hidden = lw["cmp_w1"].shape[-1]
    return pl.pallas_call(
        _compress_kernel,
        grid=(2, m // tm),
        in_specs=[pl.BlockSpec((None, tm, kdim), lambda s, i: (s, i, 0)),
                  pl.BlockSpec((None, kdim, hidden), lambda s, i: (s, 0, 0)),
                  pl.BlockSpec((None, 1, hidden), lambda s, i: (s, 0, 0)),
                  pl.BlockSpec((None, hidden, HEAD_DIM), lambda s, i: (s, 0, 0)),
                  pl.BlockSpec((None, 1, HEAD_DIM), lambda s, i: (s, 0, 0)),
                  pl.BlockSpec((1, HEAD_DIM), lambda s, i: (0, 0))],
        out_specs=pl.BlockSpec((None, tm, HEAD_DIM), lambda s, i: (s, i, 0)),
        out_shape=jax.ShapeDtypeStruct((2, m, HEAD_DIM), F32),
        compiler_params=_params("parallel", "parallel"),
        name="compress",
    )(flat, lw["cmp_w1"], lw["cmp_b1"], lw["cmp_w2"], lw["cmp_b2"], lw["g_kc"])


def _flatten_blocks(cmp_in, b, n_blk):
    x = cmp_in.reshape(b, n_blk, CMP_BLOCK, 2, N_KV, HEAD_DIM)
    x = x.transpose(3, 0, 4, 1, 2, 5)
    return x.reshape(2, b * N_KV * n_blk, CMP_BLOCK * HEAD_DIM)


def _page_copies(pt_ref, cache_ref, buf_ref, sem_ref, layer, slot, seq, buf_slot, n_pages):
    rows = N_KV * HEAD_DIM
    return [pltpu.make_async_copy(cache_ref.at[layer, pt_ref[seq, p], slot],
                                  buf_ref.at[buf_slot, pl.ds(p * rows, rows), :], sem_ref.at[buf_slot])
            for p in range(n_pages)]


def _compress_paged_kernel(pt_ref, cache_ref, w_ref, pe_ref, b1_ref, w2_ref, b2_ref, gkc_ref, seg_ref, o_ref,
                           buf_ref, sem_ref, *, n_pages, n_seq):
    n_rows = n_pages * N_KV
    step = (pl.program_id(0) * 2 + pl.program_id(1)) * n_seq + pl.program_id(2)
    n_steps = pl.num_programs(0) * 2 * n_seq
    cur = step % 2

    def copies(s, buf_slot):
        seq = s % n_seq
        return _page_copies(pt_ref, cache_ref, buf_ref, sem_ref, s // (2 * n_seq), (s // n_seq) % 2, seq, buf_slot,
                            n_pages)

    @pl.when(step == 0)
    def _():
        for cp in copies(step, cur):
            cp.start()

    @pl.when(step + 1 < n_steps)
    def _():
        for cp in copies(step + 1, 1 - cur):
            cp.start()

    for cp in copies(step, cur):
        cp.wait()

    acc = jnp.zeros((n_rows, w_ref.shape[-1]), F32)
    for d in range(0, HEAD_DIM, 2):
        a = jnp.concatenate(
            [(buf_ref[cur, pl.ds(d + u, n_rows, stride=HEAD_DIM), :] + pe_ref[d + u:d + u + 1, :]).astype(BF16)
             for u in range(2)], axis=1)
        acc = acc + _dot(a, w_ref[d // 2])
    hid = jax.nn.gelu(acc + b1_ref[...])
    o = _dot(hid.astype(BF16), w2_ref[...]) + b2_ref[...]
    is_key = pl.program_id(1) == 0
    o_ref[...] = jnp.where(is_key, _head_rms(o, seg_ref[...]) * gkc_ref[...], o)


def _compress_paged(page_table, cache_pages, pw):
    depth = cache_pages.shape[0]
    db, n_pages = page_table.shape
    n_rows = n_pages * N_KV
    hid2 = pw["w1"].shape[-1]
    sel = lambda *s: pl.BlockSpec((None, None) + s, lambda l, sl, b, pt: (l, sl) + (0,) * len(s))
    grid_spec = pltpu.PrefetchScalarGridSpec(
        num_scalar_prefetch=1,
        grid=(depth, 2, db),
        in_specs=[pl.BlockSpec(memory_space=pl.ANY),
                  sel(HEAD_DIM // 2, 2 * PAGE_SIZE, hid2), sel(HEAD_DIM, PAGE_SIZE), sel(1, hid2), sel(hid2, LANES),
                  sel(1, LANES), pl.BlockSpec((None, 1, LANES), lambda l, sl, b, pt: (l, 0, 0)),
                  pl.BlockSpec((LANES, LANES), lambda l, sl, b, pt: (0, 0))],
        out_specs=pl.BlockSpec((None, None, None, n_rows, LANES), lambda l, sl, b, pt: (l, sl, b, 0, 0)),
        scratch_shapes=[pltpu.VMEM((2, n_rows * HEAD_DIM, PAGE_SIZE), F32), pltpu.SemaphoreType.DMA((2,))])
    return pl.pallas_call(
        functools.partial(_compress_paged_kernel, n_pages=n_pages, n_seq=db),
        grid_spec=grid_spec,
        out_shape=jax.ShapeDtypeStruct((depth, 2, db, n_rows, LANES), F32),
        compiler_params=_params("arbitrary", "arbitrary", "arbitrary"),
        name="compress_paged",
    )(page_table, cache_pages, pw["w1"], pw["pe"], pw["b1"], pw["w2"], pw["b2"], pw["g_kc"], pw["seg"])


def _attn_prompt_kernel(qn_ref, qr_ref, g3_ref, kc_ref, vc_ref, ks_ref, vst_ref, kw_ref, vwt_ref,
                        o_ref, bias_ref, ocmp_ref, ms_ref, accs_ref, mw_ref, accw_ref, *, tq):
    i = pl.program_id(1)
    t0 = i * tq
    lq = Q_PER_KV * tq
    lane = lax.broadcasted_iota(jnp.int32, (1, lq), 1)
    qpos = t0 + _mod(lane, tq)
    nb = kc_ref.shape[1]
    blk = lax.broadcasted_iota(jnp.int32, (nb, 1), 0)
    zpad = jnp.zeros((HEAD_DIM, lq), BF16)

    def heads(ref, g):
        return jnp.concatenate([ref[(Q_PER_KV * g + r) * HEAD_DIM:(Q_PER_KV * g + r + 1) * HEAD_DIM, :]
                                for r in range(Q_PER_KV)], axis=1)

    qpad = []
    for g in range(N_KV):
        q4 = heads(qr_ref, g)
        qpad.append(jnp.concatenate([q4, zpad] if g == 0 else [zpad, q4], axis=0))

        cm = blk < _div(qpos + 1, CMP_BLOCK)
        s = jnp.where(cm, _dot(kc_ref[g].astype(BF16), heads(qn_ref, g)), NEG)
        p = jnp.exp2(s - jnp.max(s, axis=0, keepdims=True)) * cm.astype(F32)
        p = p / jnp.maximum(jnp.sum(p, axis=0, keepdims=True), 1e-30)
        ocmp_ref[g] = _dot_t0(vc_ref[g].astype(BF16), p.astype(BF16))

        imp = p[:, :tq]
        for r in range(1, Q_PER_KV):
            imp = imp + p[:, r * tq:(r + 1) * tq]
        imp = _block_importance(imp, blk, _div(qpos[:, :tq], CMP_BLOCK))
        bias_ref[g] = (_topk_mask(imp, N_SELECT) - 1.0) * (-NEG)

    for m_ref, acc_ref in ((ms_ref, accs_ref), (mw_ref, accw_ref)):
        m_ref[...] = jnp.full(m_ref.shape, NEG, F32)
        acc_ref[...] = jnp.zeros(acc_ref.shape, F32)

    kidx = lax.broadcasted_iota(jnp.int32, (KEY_CHUNK, 1), 0)

    tpos = qpos[:, :tq]

    chains = [(g, slice(r * tq, (r + 1) * tq)) for g in range(N_KV) for r in range(Q_PER_KV)]

    def qk(k_ref, kc0, n_keys):
        kblk = k_ref[pl.ds(kc0, n_keys), :]
        return tuple(_dot(kblk, qpad[g][:, sl]) for g, sl in chains)

    def softmax_pv(scores, bias, vt_ref, kc0, n_keys, m_ref, acc_ref):
        m_all, acc_all = m_ref[...], acc_ref[...]
        m_out, alphas, probs = [], [], []
        for (g, sl), s in zip(chains, scores):
            s = s + bias[g]
            m = m_all[g, :, sl]
            m_new = jnp.maximum(m, jnp.max(s, axis=0, keepdims=True))
            m_out.append(m_new)
            alphas.append(jnp.exp2(m - m_new))
            probs.append(jnp.exp2(s - m_new).astype(BF16))
        vts = [vt_ref[g * VT_ROWS:(g + 1) * VT_ROWS, pl.ds(kc0, n_keys)] for g in range(N_KV)]
        acc_out = [alpha * acc_all[g, :, sl] + _dot(vts[g], p)
                   for (g, sl), alpha, p in zip(chains, alphas, probs)]
        gather = lambda parts: jnp.stack([jnp.concatenate(parts[g * Q_PER_KV:(g + 1) * Q_PER_KV], axis=1)
                                          for g in range(N_KV)])
        m_ref[...] = gather(m_out)
        acc_ref[...] = gather(acc_out)

    kidx2 = lax.broadcasted_iota(jnp.int32, (SEL_CHUNK, 1), 0)

    def sel_chunk(c, causal):
        k0 = pl.multiple_of(c * SEL_CHUNK, SEL_CHUNK)
        bias = []
        for g in range(N_KV):
            rows8 = bias_ref[g, pl.ds(pl.multiple_of(c * 8, 8), 8), :]
            b = jnp.concatenate([jnp.broadcast_to(rows8[j:j + 1, :], (CMP_BLOCK, tq)) for j in range(8)], axis=0)
            bias.append(jnp.where(k0 + kidx2 <= tpos, b, NEG) if causal else b)
        softmax_pv(qk(ks_ref, k0, SEL_CHUNK), bias, vst_ref, k0, SEL_CHUNK, ms_ref, accs_ref)

    def sel_body(c, carry):
        sel_chunk(c, False)
        return carry

    last = (t0 + tq - 1) // SEL_CHUNK
    lax.fori_loop(0, last, sel_body, 0)
    sel_chunk(last, True)

    n_win = WINDOW // KEY_CHUNK
    for c in range(n_win + 1):
        k0 = t0 - WINDOW + c * KEY_CHUNK
        start = pl.multiple_of(jnp.maximum(k0, 0), KEY_CHUNK)
        if c == n_win:
            bias_w = jnp.where(k0 + kidx <= tpos, 0.0, NEG)
        elif c == 0:
            bias_w = jnp.where(tpos - (k0 + kidx) < jnp.where(k0 >= 0, WINDOW, -(2 ** 30)), 0.0, NEG)
        else:
            bias_w = jnp.broadcast_to(jnp.where(k0 >= 0, 0.0, NEG), (1, 1))
        softmax_pv(qk(kw_ref, start, KEY_CHUNK), [bias_w] * N_KV, vwt_ref, start, KEY_CHUNK, mw_ref, accw_ref)

    for g in range(N_KV):
        gate = lambda br: jnp.concatenate(
            [g3_ref[br * N_HEADS + Q_PER_KV * g + r:br * N_HEADS + Q_PER_KV * g + r + 1, :]
             for r in range(Q_PER_KV)], axis=1)
        branch = lambda acc_ref: acc_ref[g, :HEAD_DIM, :] / acc_ref[g, HEAD_DIM:HEAD_DIM + 1, :]
        o = gate(0) * ocmp_ref[g] + gate(1) * branch(accs_ref) + gate(2) * branch(accw_ref)
        for r in range(Q_PER_KV):
            o_ref[(Q_PER_KV * g + r) * HEAD_DIM:(Q_PER_KV * g + r + 1) * HEAD_DIM, :] = (
                o[:, r * tq:(r + 1) * tq].astype(BF16))


def _attn_prompt(qn_t, qr_t, g3_t, kc, vc, ks, vst, kw, vwt, b, t):
    tq = KEY_CHUNK
    nq = t // tq
    lq = Q_PER_KV * tq
    n_cmp = kc.shape[2]
    qcol = lambda h: pl.BlockSpec((h, tq), lambda bi, i: (0, bi * nq + i))
    per_b = pl.BlockSpec((None, N_KV, n_cmp, HEAD_DIM), lambda bi, i: (bi, 0, 0, 0))
    keys = pl.BlockSpec((t, LANES), lambda bi, i: (bi, 0))
    vals = pl.BlockSpec((N_KV * VT_ROWS, t), lambda bi, i: (0, bi))
    st = lambda r: pltpu.VMEM((N_KV, r, lq), F32)
    return pl.pallas_call(
        functools.partial(_attn_prompt_kernel, tq=tq),
        grid=(b, nq),
        in_specs=[qcol(512), qcol(512), qcol(3 * N_HEADS), per_b, per_b, keys, vals, keys, vals],
        out_specs=qcol(512),
        out_shape=jax.ShapeDtypeStruct((512, b * t), BF16),
        scratch_shapes=[pltpu.VMEM((N_KV, n_cmp, tq), F32), st(HEAD_DIM),
                        st(1), st(VT_ROWS), st(1), st(VT_ROWS)],
        compiler_params=_params("parallel", "arbitrary"),
        name="attn_prompt",
    )(qn_t, qr_t, g3_t, kc, vc, ks, vst, kw, vwt)


def _attn_sample_kernel(pt_ref, *refs, n_pages, n_blk, pos0, tq):
    pages = refs[:ATTN_PAGES]
    (q_ref, qnt_ref, kc_ref, vc_ref, kn_ref, vn_ref, kwn_ref, vwn_ref, win_ref, g3_ref, expand_ref, rsum_ref,
     o_ref, sel_ref, m_ref, l_ref, acc_ref, ocmp_ref) = refs[ATTN_PAGES:]
    del pt_ref
    j = pl.program_id(1)
    lq = Q_PER_KV * tq
    nbp = sel_ref.shape[1]
    n_cmp = kc_ref.shape[1]
    step_keys = ATTN_PAGES * PAGE_SIZE

    @pl.when(j == 0)
    def _():
        lane = lax.broadcasted_iota(jnp.int32, (1, lq), 1)
        qpos = pos0 + _mod(lane, tq)
        blk = lax.broadcasted_iota(jnp.int32, (nbp, 1), 0)
        for g in range(N_KV):
            cm = blk[:n_cmp] < _div(qpos + 1, CMP_BLOCK)
            s = jnp.where(cm, _dot(kc_ref[g].astype(BF16), qnt_ref[g]), NEG)
            p = jnp.exp2(s - jnp.max(s, axis=0, keepdims=True)) * cm.astype(F32)
            p = p / jnp.maximum(jnp.sum(p, axis=0, keepdims=True), 1e-30)
            ocmp_ref[g] = _dot_t0(p.astype(BF16), vc_ref[g].astype(BF16))
            imp = _split_dot(p, rsum_ref[...])
            imp = jnp.concatenate([imp, jnp.zeros((nbp - n_cmp, lq), F32)], axis=0)
            imp = _block_importance(imp, blk, _div(qpos, CMP_BLOCK))
            imp = jnp.where(blk >= n_blk, -jnp.inf, imp)
            sel_ref[g] = _topk_mask(imp, N_SELECT)
        m_ref[...] = jnp.full(m_ref.shape, NEG, F32)
        l_ref[...] = jnp.zeros(l_ref.shape, F32)
        acc_ref[...] = jnp.zeros(acc_ref.shape, F32)

    def update(g, s, pv):
        m = m_ref[g]
        m_new = jnp.maximum(m, jnp.max(s, axis=1, keepdims=True))
        alpha = jnp.exp2(m - m_new)
        p = jnp.exp2(s - m_new)
        l_ref[g] = alpha * l_ref[g] + jnp.sum(p, axis=1, keepdims=True)
        acc_ref[g] = alpha * acc_ref[g] + pv(p)
        m_ref[g] = m_new

    step_blocks = step_keys // CMP_BLOCK
    expand = expand_ref[...]
    scores = []
    for g in range(N_KV):
        kt = jnp.concatenate([pg[0, g].astype(BF16) for pg in pages], axis=1)
        sel_rows = sel_ref[g, pl.ds(pl.multiple_of(j * step_blocks, step_blocks), step_blocks), :]
        picked = _dot_t0(sel_rows.astype(BF16), expand)
        scores.append(_dot(q_ref[g], kt) + (picked - 1.0) * (-NEG))
    m_all, l_all, acc_all = m_ref[...], l_ref[...], acc_ref[...]
    m_out, l_out, alphas, probs = [], [], [], []
    for g, s in enumerate(scores):
        m_new = jnp.maximum(m_all[g], jnp.max(s, axis=1, keepdims=True))
        alpha = jnp.exp2(m_all[g] - m_new)
        p = jnp.exp2(s - m_new)
        m_out.append(m_new)
        l_out.append(alpha * l_all[g] + jnp.sum(p, axis=1, keepdims=True))
        alphas.append(alpha)
        probs.append(p.astype(BF16))
    acc_out = []
    for g in range(N_KV):
        vt = jnp.concatenate([pg[1, g].astype(BF16) for pg in pages], axis=1)
        acc_out.append(alphas[g] * acc_all[g] + _dot_t1(probs[g], vt))
    m_ref[...] = jnp.stack(m_out)
    l_ref[...] = jnp.stack(l_out)
    acc_ref[...] = jnp.stack(acc_out)

    @pl.when(j == n_pages // ATTN_PAGES - 1)
    def _():
        rowi = _mod(lax.broadcasted_iota(jnp.int32, (lq, 1), 0), tq)
        keyi = lax.broadcasted_iota(jnp.int32, (1, tq), 1)
        causal = keyi <= rowi
        wlane = lax.broadcasted_iota(jnp.int32, (1, WINDOW), 1)
        kpos = pos0 - WINDOW + wlane
        in_win = ((pos0 + rowi) - kpos < WINDOW) & (kpos >= 0)
        cur_blk = pos0 // CMP_BLOCK
        first_row = (lax.broadcasted_iota(jnp.int32, (8, tq), 0) == cur_blk % 8).astype(F32)
        for g in range(N_KV):
            q = q_ref[g]
            qf = q.astype(F32)
            base = (cur_blk // 8) * 8
            picked = _dot_t0(sel_ref[g, base:base + 8, :], first_row)
            s = jnp.where(causal & (picked > 0.5), _dot_t1(qf, kn_ref[g]), NEG)
            update(g, s, lambda p: _dot(p, vn_ref[g]))
            o_sel = acc_ref[g] / l_ref[g]

            s_pre = jnp.where(in_win, _dot(q, win_ref[0, g].astype(BF16)), NEG)
            s_new = jnp.where(causal, _dot_t1(qf, kwn_ref[g]), NEG)
            m = jnp.maximum(jnp.max(s_pre, axis=1, keepdims=True), jnp.max(s_new, axis=1, keepdims=True))
            p_pre = jnp.exp2(s_pre - m) * in_win.astype(F32)
            p_new = jnp.exp2(s_new - m) * causal.astype(F32)
            l = jnp.sum(p_pre, axis=1, keepdims=True) + jnp.sum(p_new, axis=1, keepdims=True)
            o_win = (_dot_t1(p_pre.astype(BF16), win_ref[1, g].astype(BF16)) + _dot(p_new, vwn_ref[g]))
            o_win = o_win / jnp.maximum(l, 1e-30)

            g3 = g3_ref[g]
            o_ref[g] = g3[:, 0:1] * ocmp_ref[g] + g3[:, 1:2] * o_sel + g3[:, 2:3] * o_win


def _attn_sample(page_table, cache_t, layer, q, qn_t, kc, vc, kn, vn, kwn, vwn, win_t, g3, expand, rsum, pos0, tq):
    db, n_pages = page_table.shape
    lq = Q_PER_KV * tq
    n_cmp = kc.shape[2]
    n_blk = -(-(pos0 + tq) // CMP_BLOCK)
    nbp = -(-n_blk // 8) * 8
    steps = n_pages // ATTN_PAGES

    def page_spec(k):
        return pl.BlockSpec((None, None, 2, N_KV, HEAD_DIM, PAGE_SIZE),
                            lambda b, j, pt: (layer, pt[b, j * ATTN_PAGES + k], 1, 0, 0, 0))

    per_b = lambda *s: pl.BlockSpec((None,) + s, lambda b, j, pt: (b,) + (0,) * len(s))
    win_spec = pl.BlockSpec((None, None, 2, N_KV, HEAD_DIM, WINDOW), lambda b, j, pt: (layer, b, 0, 0, 0, 0))
    const = lambda *s: pl.BlockSpec(s, lambda b, j, pt: (0,) * len(s))
    grid_spec = pltpu.PrefetchScalarGridSpec(
        num_scalar_prefetch=1,
        grid=(db, steps),
        in_specs=[page_spec(k) for k in range(ATTN_PAGES)] + [
            per_b(N_KV, lq, HEAD_DIM), per_b(N_KV, HEAD_DIM, lq), per_b(N_KV, n_cmp, HEAD_DIM),
            per_b(N_KV, n_cmp, HEAD_DIM), per_b(N_KV, tq, HEAD_DIM), per_b(N_KV, tq, HEAD_DIM),
            per_b(N_KV, tq, HEAD_DIM), per_b(N_KV, tq, HEAD_DIM), win_spec,
            per_b(N_KV, lq, 8), const(ATTN_PAGES * PAGE_SIZE // CMP_BLOCK, ATTN_PAGES * PAGE_SIZE), const(lq, lq)],
        out_specs=per_b(N_KV, lq, HEAD_DIM),
        scratch_shapes=[pltpu.VMEM((N_KV, nbp, lq), F32), pltpu.VMEM((N_KV, lq, 1), F32),
                        pltpu.VMEM((N_KV, lq, 1), F32), pltpu.VMEM((N_KV, lq, HEAD_DIM), F32),
                        pltpu.VMEM((N_KV, lq, HEAD_DIM), F32)])
    return pl.pallas_call(
        functools.partial(_attn_sample_kernel, n_pages=n_pages, n_blk=n_blk, pos0=pos0, tq=tq),
        grid_spec=grid_spec,
        out_shape=jax.ShapeDtypeStruct((db, N_KV, lq, HEAD_DIM), F32),
        compiler_params=_params("parallel", "arbitrary"),
        name="attn_sample",
    )(page_table, *([cache_t] * ATTN_PAGES), q, qn_t, kc, vc, kn, vn, kwn, vwn, win_t, g3, expand, rsum)


def _merge_kernel(x_ref, y_ref, o_ref, gm_ref, gt_ref, sc_ref, sh_ref, gn_ref, wc_ref, wa_ref, wo_ref,
                  wr_ref, br_ref, x1_ref, h2_ref, dw_ref, *, o_transposed):
    tm = x_ref.shape[0]
    halves = [slice(h * (tm // 2), (h + 1) * (tm // 2)) for h in range(2)]
    rows_of = lambda ref, rows: ref[...] if ref.shape[0] == 1 else ref[rows, :]

    branch = []
    for rows in halves:
        attn = _dot_t0(o_ref[:, rows], wa_ref[...]) if o_transposed else _dot(o_ref[rows, :], wa_ref[...])
        branch.append((_dot(y_ref[rows, :], wc_ref[...]), attn))
    mixed = []
    for rows, (conv, attn) in zip(halves, branch):
        gm = gm_ref[rows, :]
        merged = gm[:, :D_MODEL].astype(F32) * conv + gm[:, D_MODEL:].astype(F32) * attn
        mixed.append(_dot(merged.astype(BF16), wo_ref[...]))
    x1s, h2s, affs = [], [], []
    for rows, mix in zip(halves, mixed):
        x1 = x_ref[rows, :] + rows_of(gt_ref, rows) * mix
        h2 = _rms(x1) * gn_ref[...]
        h2 = h2 * (1.0 + rows_of(sc_ref, rows)) + rows_of(sh_ref, rows)
        h_hi = h2.astype(BF16)
        h_lo = (h2 - h_hi.astype(F32)).astype(BF16)
        x1s.append(x1)
        h2s.append(h_hi)
        affs.append(_sigmoid(_dot(h_hi, wr_ref[0]) + _dot(h_lo, wr_ref[0]) + _dot(h_hi, wr_ref[1])))
    x1_ref[...] = jnp.concatenate(x1s, axis=0)
    h2_ref[...] = jnp.concatenate(h2s, axis=0)

    dws = []
    for aff in affs:
        score = aff + br_ref[...]
        e = lax.broadcasted_iota(jnp.int32, score.shape, 1)
        grp = _div(e, EXPERTS_PER_GROUP)
        e = e.astype(F32)
        big = float(LANES)

        def top2(j):
            vals = jnp.where(grp == j, score, -jnp.inf)
            m1 = jnp.max(vals, axis=-1, keepdims=True)
            i1 = jnp.min(jnp.where(vals == m1, e, big), axis=-1, keepdims=True)
            vals = jnp.where(e == i1, -jnp.inf, vals)
            m2 = jnp.max(vals, axis=-1, keepdims=True)
            i2 = jnp.min(jnp.where(vals == m2, e, big), axis=-1, keepdims=True)
            return m1 + m2, i1, i2

        best, ia, ib = top2(0)
        for j in range(1, N_GROUPS):
            gs, i1, i2 = top2(j)
            better = gs > best
            best = jnp.where(better, gs, best)
            ia = jnp.where(better, i1, ia)
            ib = jnp.where(better, i2, ib)
        hit_a = e == ia
        hit_b = e == ib
        aff_a = jnp.sum(jnp.where(hit_a, aff, 0.0), axis=-1, keepdims=True)
        aff_b = jnp.sum(jnp.where(hit_b, aff, 0.0), axis=-1, keepdims=True)
        tot = aff_a + aff_b
        dws.append(jnp.where(hit_a, aff_a / tot, 0.0) + jnp.where(hit_b, aff_b / tot, 0.0))
    dw_ref[...] = jnp.concatenate(dws, axis=0)


def _merge(x, y, o, gm, mods, lw, shared, rows_per_batch, tm, o_transposed):
    n = x.shape[0]
    gt, gt_spec = _mod_operand(mods["gt1"], rows_per_batch, tm)
    sc, sc_spec = _mod_operand(mods["sc2"], rows_per_batch, tm)
    sh, sh_spec = _mod_operand(mods["sh2"], rows_per_batch, tm)
    row = lambda w: pl.BlockSpec((tm, w), lambda i: (i, 0))
    o_spec = pl.BlockSpec((512, tm), lambda i: (0, i)) if o_transposed else row(512)
    return pl.pallas_call(
        functools.partial(_merge_kernel, o_transposed=o_transposed),
        grid=(n // tm,),
        in_specs=[row(D_MODEL), row(D_CONV), o_spec, row(2 * D_MODEL), gt_spec, sc_spec, sh_spec,
                  _const_spec((1, D_MODEL)), _const_spec((D_CONV, D_MODEL)), _const_spec((512, D_MODEL)),
                  _const_spec((D_MODEL, D_MODEL)), _const_spec((2, D_MODEL, LANES)), _const_spec((1, LANES))],
        out_specs=[row(D_MODEL), row(D_MODEL), row(LANES)],
        out_shape=[jax.ShapeDtypeStruct((n, D_MODEL), F32), jax.ShapeDtypeStruct((n, D_MODEL), BF16),
                   jax.ShapeDtypeStruct((n, LANES), F32)],
        compiler_params=_params("parallel"),
        name="merge",
    )(x, y, o, gm, gt, sc, sh, lw["g_norm_ffn"], lw["w_conv_out"], lw["w_attn_out"], lw["w_out"],
      shared["w_router"], shared["b_router"])


def _moe_kernel(h_ref, dw_ref, x_ref, gt_ref, wg_ref, wu_ref, wd_ref, o_ref, acc_ref):
    ex = pl.program_id(1)

    @pl.when(ex == 0)
    def _():
        acc_ref[...] = jnp.zeros(acc_ref.shape, F32)

    h = h_ref[...]
    dw = dw_ref[...]
    lane = lax.broadcasted_iota(jnp.int32, dw.shape, 1)
    part = None
    for j in range(MOE_EXPERTS_PER_STEP):
        w_col = jnp.sum(jnp.where(lane == ex * MOE_EXPERTS_PER_STEP + j, dw, 0.0), axis=-1, keepdims=True)
        hid = _silu(_dot(h, wg_ref[j])) * _dot(h, wu_ref[j]) * w_col
        out = _dot(hid.astype(BF16), wd_ref[j])
        part = out if part is None else part + out
    acc_ref[...] += part

    @pl.when(ex == N_EXPERTS // MOE_EXPERTS_PER_STEP - 1)
    def _():
        o_ref[...] = x_ref[...] + gt_ref[...] * acc_ref[...]


def _moe(h2, dw, x1, mods, lw, rows_per_batch, tm):
    n = h2.shape[0]
    gt, gt_spec = _mod_operand(mods["gt2"], rows_per_batch, tm)
    gt_spec = pl.BlockSpec(gt_spec.block_shape, lambda i, ex, f=gt_spec.index_map: f(i))
    row = lambda w: pl.BlockSpec((tm, w), lambda i, ex: (i, 0))
    return pl.pallas_call(
        _moe_kernel,
        grid=(n // tm, N_EXPERTS // MOE_EXPERTS_PER_STEP),
        in_specs=[row(D_MODEL), row(LANES), row(D_MODEL), gt_spec,
                  pl.BlockSpec((MOE_EXPERTS_PER_STEP, D_MODEL, D_EXPERT), lambda i, ex: (ex, 0, 0)),
                  pl.BlockSpec((MOE_EXPERTS_PER_STEP, D_MODEL, D_EXPERT), lambda i, ex: (ex, 0, 0)),
                  pl.BlockSpec((MOE_EXPERTS_PER_STEP, D_EXPERT, D_MODEL), lambda i, ex: (ex, 0, 0))],
        out_specs=row(D_MODEL),
        out_shape=jax.ShapeDtypeStruct((n, D_MODEL), F32),
        scratch_shapes=[pltpu.VMEM((tm, D_MODEL), F32)],
        compiler_params=_params("parallel", "arbitrary"),
        name="moe",
    )(h2, dw, x1, gt, lw["w_exp_gate"], lw["w_exp_up"], lw["w_exp_down"])


def _rope_tables(pos):
    half = HEAD_DIM // 2
    inv = ROPE_THETA ** (-jnp.arange(half, dtype=F32) / half)
    ang = pos.astype(F32)[:, None] * inv[None, :]
    cos, sin = jnp.cos(ang), jnp.sin(ang)
    cos = jnp.concatenate([cos, cos], axis=1)
    sin = jnp.concatenate([-sin, sin], axis=1)
    return jnp.concatenate([cos, cos], axis=1), jnp.concatenate([sin, sin], axis=1)


def _split_mod(mod):
    names = ("sh1", "sc1", "gt1", "sh2", "sc2", "gt2")
    return dict(zip(names, jnp.split(mod, 6, axis=-1)))


def _paged_compress_weights(cmp_pe, cmp_w1, cmp_b1, cmp_w2, cmp_b2, g_kc, seg):
    depth = cmp_w1.shape[0]
    hidden = cmp_w1.shape[-1]
    def block_diag2(w):
        z = jnp.zeros_like(w)
        return jnp.concatenate([jnp.concatenate([w, z], axis=-1), jnp.concatenate([z, w], axis=-1)], axis=-2)

    w1 = cmp_w1.astype(BF16).reshape(depth, 2, CMP_BLOCK, HEAD_DIM, hidden).transpose(0, 1, 3, 2, 4)
    w1 = block_diag2(w1).reshape(depth, 2, HEAD_DIM // 2, 2 * PAGE_SIZE, 2 * hidden)
    w2 = block_diag2(cmp_w2.astype(BF16))
    pe = jnp.tile(cmp_pe.transpose(0, 1, 3, 2), (1, 1, 1, 2))
    return dict(w1=w1, w2=w2, pe=pe,
                b1=jnp.tile(cmp_b1, (1, 1, 2))[:, :, None, :], b2=jnp.tile(cmp_b2, (1, 1, 2))[:, :, None, :],
                g_kc=jnp.tile(g_kc, (1, 2))[:, None, :], seg=seg[:LANES, :LANES])


def kernel(x_prompt, x_sample, c_prompt, c_sample, cache_kv, state_win_kv, state_conv, page_table, w_ada, b_ada, g_norm_mix, g_norm_ffn, w_in, w_dw, b_dw, ln_conv_g, ln_conv_b, w_conv_out, g_q, g_kc, g_ks, g_kw, cmp_pe, cmp_w1, cmp_b1, cmp_w2, cmp_b2, w_attn_out, w_out, w_router, b_router, w_exp_gate, w_exp_up, w_exp_down):
    depth = w_in.shape[0]
    b, t, _ = x_prompt.shape
    db, dt, _ = x_sample.shape
    n_pages = page_table.shape[1]
    past_len = n_pages * PAGE_SIZE
    win_buf = state_win_kv.shape[2]
    tm_p, tm_s = 512, db * dt
    lq_s = Q_PER_KV * dt
    n_cmp_s = (past_len + dt) // CMP_BLOCK
    assert t % tm_p == 0 and t % KEY_CHUNK == 0 and win_buf == WINDOW and WINDOW % KEY_CHUNK == 0
    assert n_cmp_s * CMP_BLOCK == past_len and n_pages % ATTN_PAGES == 0

    c0, c1, c2, c3 = 2 * D_CONV, 2 * D_CONV + 512, 2 * D_CONV + 512 + 768, 2 * D_CONV + 512 + 768 + 3 * N_HEADS
    w_in_b = w_in.astype(BF16)
    w_dw_p = jnp.pad(w_dw, ((0, 0), (0, CONV_HALO - CONV_WIDTH), (0, 0)))
    pe256 = cmp_pe[:, :, :, None, :].repeat(N_KV, axis=3).transpose(0, 2, 1, 3, 4).reshape(depth, CMP_BLOCK, 256)
    pe_tile = jnp.tile(pe256, (1, tm_p // CMP_BLOCK, 1))
    layers = []
    for l in range(depth):
        layers.append(dict(
            g_norm_mix=g_norm_mix[l][None], g_norm_ffn=g_norm_ffn[l][None],
            w_u=w_in_b[l][:, :c0], w_q=w_in_b[l][:, c0:c1], w_kv=w_in_b[l][:, c1:c2],
            w_g=jnp.pad(w_in_b[l][:, c2:c3], ((0, 0), (0, LANES - 3 * N_HEADS))), w_m=w_in_b[l][:, c3:],
            g_q=jnp.tile(g_q[l], N_HEADS)[None], g_ks=jnp.tile(g_ks[l], N_KV)[None], g_kw=jnp.tile(g_kw[l], N_KV)[None],
            g_kc=g_kc[l][None], pe_tile=pe_tile[l],
            w_dw=w_dw_p[l], b_dw=b_dw[l][None], ln_conv_g=ln_conv_g[l][None], ln_conv_b=ln_conv_b[l][None],
            w_conv_out=w_conv_out[l].astype(BF16), w_attn_out=w_attn_out[l].astype(BF16), w_out=w_out[l].astype(BF16),
            cmp_w1=cmp_w1[l].astype(BF16), cmp_b1=cmp_b1[l][:, None, :], cmp_w2=cmp_w2[l].astype(BF16),
            cmp_b2=cmp_b2[l][:, None, :],
            w_exp_gate=w_exp_gate[l].astype(BF16), w_exp_up=w_exp_up[l].astype(BF16),
            w_exp_down=w_exp_down[l].astype(BF16)))
    wr = jnp.pad(w_router, ((0, 0), (0, LANES - N_EXPERTS)))
    wr_hi = wr.astype(BF16)
    wr_lo = (wr - wr_hi.astype(F32)).astype(BF16)
    shared = dict(w_router=jnp.stack([wr_hi, wr_lo]),
                  b_router=jnp.pad(b_router, (0, LANES - N_EXPERTS))[None])
    seg = (jnp.arange(512)[:, None] // HEAD_DIM == jnp.arange(512)[None, :] // HEAD_DIM).astype(BF16)
    cos_p, sin_p = _rope_tables(jnp.arange(t, dtype=jnp.int32))
    cos_s, sin_s = _rope_tables(past_len + jnp.arange(dt, dtype=jnp.int32))
    tabs_p = dict(cos=cos_p, sin=sin_p, seg=seg)
    tabs_s = dict(cos=jnp.tile(cos_s, (db, 1)), sin=jnp.tile(sin_s, (db, 1)), seg=seg)
    rsum = (jnp.arange(lq_s)[:, None] % dt == jnp.arange(lq_s)[None, :] % dt).astype(BF16)
    step_keys = ATTN_PAGES * PAGE_SIZE
    expand = (jnp.arange(step_keys // CMP_BLOCK)[:, None] == jnp.arange(step_keys)[None, :] // CMP_BLOCK).astype(BF16)

    rows = b + db
    rows_p = -(-rows // 8) * 8
    c_all = jnp.pad(jnp.concatenate([c_prompt, c_sample], axis=0), ((0, rows_p - rows), (0, 0)))
    mod_all = _ada_all_layers(c_all, w_ada.astype(BF16), b_ada[:, None, :])

    cache_t = cache_kv.transpose(0, 1, 3, 4, 5, 2)
    win_t = state_win_kv.transpose(0, 1, 3, 4, 5, 2)

    paged = _compress_paged(page_table, cache_t.reshape(depth, -1, 4, N_KV * HEAD_DIM, PAGE_SIZE),
                            _paged_compress_weights(cmp_pe, cmp_w1, cmp_b1, cmp_w2, cmp_b2, g_kc, seg))
    paged = paged.reshape(depth, 2, db, n_pages, N_KV, 2, HEAD_DIM).transpose(0, 1, 2, 4, 3, 5, 6)
    paged = paged.reshape(depth, 2, db, N_KV, n_cmp_s, HEAD_DIM)

    xp = x_prompt.reshape(b * t, D_MODEL)
    xs = x_sample.reshape(db * dt, D_MODEL)
    outs = {k: [] for k in ("kv_p", "kv_s", "win_p", "win_s", "conv_p", "conv_s")}
    zero_prefix = jnp.zeros((b, CONV_HALO, D_CONV), F32)

    for l in range(depth):
        lw = layers[l]
        mods_p = _split_mod(mod_all[l, :b])
        mods_s = _split_mod(mod_all[l, b:b + db])

        (ug, gm, qn_t, qr_t, rows_t, win_tp, g3_t, cmp_in, ks, kw, vst, vwt) = _inproj(
            xp, mods_p, lw, tabs_p, t, tm_p, True)
        y = _conv_branch(ug.reshape(b, t, D_CONV), zero_prefix, lw, 512)
        n_blk = t // CMP_BLOCK
        kvc = _compress(_flatten_blocks(cmp_in, b, n_blk), lw).reshape(2, b, N_KV, n_blk, HEAD_DIM)
        o_t = _attn_prompt(qn_t, qr_t, g3_t, kvc[0], kvc[1], ks, vst, kw, vwt, b, t)
        x1, h2, dw = _merge(xp, y.reshape(b * t, D_CONV), o_t, gm, mods_p, lw, shared, t, tm_p, True)
        xp = _moe(h2, dw, x1, mods_p, lw, t, tm_p)
        outs["kv_p"].append(rows_t.reshape(b, 4, N_KV, HEAD_DIM, t).transpose(0, 4, 1, 2, 3))
        outs["win_p"].append(win_tp.reshape(b, 2, N_KV, HEAD_DIM, t)[..., t - win_buf:].transpose(0, 4, 1, 2, 3))
        outs["conv_p"].append(ug.reshape(b, t, D_CONV)[:, t - (CONV_WIDTH - 1):])

        ug, gm, qn, qr, rows_new, win_new, g3 = _inproj(xs, mods_s, lw, tabs_s, dt, tm_s, False)
        ug3 = ug.reshape(db, dt, D_CONV)
        conv_ext = jnp.concatenate([state_conv[l], ug3], axis=1)
        prefix = jnp.pad(state_conv[l], ((0, 0), (CONV_HALO - (CONV_WIDTH - 1), 0), (0, 0)))
        y = _conv_branch(ug3, prefix, lw, dt)
        r5 = rows_new.reshape(db, dt, 4, N_KV, HEAD_DIM)
        w5 = win_new.reshape(db, dt, 2, N_KV, HEAD_DIM)
        per_group = lambda a: a.transpose(0, 2, 1, 3)
        q5 = lambda a: a.reshape(db, dt, N_KV, Q_PER_KV, HEAD_DIM)
        q_rows = q5(qr).transpose(0, 2, 3, 1, 4).reshape(db, N_KV, lq_s, HEAD_DIM)
        qn_cols = q5(qn).transpose(0, 2, 4, 3, 1).reshape(db, N_KV, HEAD_DIM, lq_s)
        g3_rows = g3[:, :3 * N_HEADS].reshape(db, dt, 3, N_KV, Q_PER_KV).transpose(0, 3, 4, 1, 2)
        g3_rows = jnp.pad(g3_rows.reshape(db, N_KV, lq_s, 3), ((0, 0), (0, 0), (0, 0), (0, 5)))
        o_s = _attn_sample(page_table, cache_t, l, q_rows, qn_cols, paged[l, 0], paged[l, 1],
                           per_group(r5[:, :, 2]), per_group(r5[:, :, 3]), per_group(w5[:, :, 0]),
                           per_group(w5[:, :, 1]), win_t, g3_rows, expand, rsum, past_len, dt)
        o = o_s.reshape(db, N_KV, Q_PER_KV, dt, HEAD_DIM).transpose(0, 3, 1, 2, 4).reshape(db * dt, 512).astype(BF16)
        x1, h2, dw = _merge(xs, y.reshape(db * dt, D_CONV), o, gm, mods_s, lw, shared, dt, tm_s, False)
        xs = _moe(h2, dw, x1, mods_s, lw, dt, tm_s)
        outs["kv_s"].append(r5)
        outs["win_s"].append(jnp.concatenate([state_win_kv[l], w5], axis=1)[:, dt:])
        outs["conv_s"].append(conv_ext[:, dt:])

    return (xp.reshape(b, t, D_MODEL), xs.reshape(db, dt, D_MODEL),
            jnp.stack(outs["kv_p"]), jnp.stack(outs["kv_s"]), jnp.stack(outs["win_p"]), jnp.stack(outs["win_s"]),
            jnp.stack(outs["conv_p"]), jnp.stack(outs["conv_s"]))
```

```python
import functools

import jax
import jax.numpy as jnp
from jax import lax
from jax.experimental import pallas as pl
from jax.experimental.pallas import tpu as pltpu

F32 = jnp.float32
BF16 = jnp.bfloat16

D_MODEL = 1024
D_CONV = 512
CONV_WIDTH = 31
CONV_HALO = 32
CONV_PAD = 16
N_HEADS = 8
HEAD_DIM = 64
N_KV = 2
Q_PER_KV = 4
CMP_BLOCK = 64
N_SELECT = 16
WINDOW = 512
N_EXPERTS = 16
N_GROUPS = 4
EXPERTS_PER_GROUP = 4
D_EXPERT = 256
PAGE_SIZE = 128
ROPE_THETA = 10000.0
ATTN_SCALE = HEAD_DIM ** -0.5
FORCED_SCORE = 1e4
EPS = 1e-6
NEG = -1e30
LANES = 128
KEY_CHUNK = 256
SEL_CHUNK = 512
VT_ROWS = 80
ATTN_PAGES = 16
MOE_EXPERTS_PER_STEP = 4
LOG2E = 1.4426950408889634
VMEM_LIMIT = 56 * 1024 * 1024


def _params(*sem):
    return pltpu.CompilerParams(dimension_semantics=sem, vmem_limit_bytes=VMEM_LIMIT)


def _dot(a, b):
    return jnp.dot(a, b, preferred_element_type=F32)


def _dot_t0(a, b):
    return lax.dot_general(a, b, (((0,), (0,)), ((), ())), preferred_element_type=F32)


def _dot_t1(a, b):
    return lax.dot_general(a, b, (((1,), (1,)), ((), ())), preferred_element_type=F32)


def _shift(d):
    assert d & (d - 1) == 0
    return d.bit_length() - 1


def _div(x, d):
    return lax.shift_right_logical(x, jnp.int32(_shift(d)))


def _mod(x, d):
    return x & jnp.int32(d - 1)


def _sigmoid(x):
    return 1.0 / (1.0 + jnp.exp(-x))


def _silu(x):
    return x * _sigmoid(x)


def _rms(x):
    return x * lax.rsqrt(jnp.mean(x * x, axis=-1, keepdims=True) + EPS)


def _split_dot(x, w):
    hi = x.astype(BF16)
    lo = (x - hi.astype(F32)).astype(BF16)
    return _dot(hi, w) + _dot(lo, w)


def _head_rms(x, seg):
    ss = _split_dot(x * x, seg)
    return x * lax.rsqrt(ss * (1.0 / HEAD_DIM) + EPS)


def _rope(x, cos, sin_signed):
    w = x.shape[-1]
    lane = lax.broadcasted_iota(jnp.int32, x.shape, 1)
    first_half = _mod(lane, HEAD_DIM) < (HEAD_DIM // 2)
    partner = jnp.where(first_half, pltpu.roll(x, w - HEAD_DIM // 2, 1), pltpu.roll(x, HEAD_DIM // 2, 1))
    return x * cos + partner * sin_signed


def _topk_mask(imp, k):
    nb = imp.shape[0]
    blk = lax.broadcasted_iota(jnp.int32, imp.shape, 0).astype(F32)
    sel = jnp.zeros(imp.shape, F32)
    for _ in range(k):
        mx = jnp.max(imp, axis=0, keepdims=True)
        first = jnp.min(jnp.where(imp == mx, blk, float(nb)), axis=0, keepdims=True)
        hit = blk == first
        sel = jnp.where(hit, 1.0, sel)
        imp = jnp.where(hit, -jnp.inf, imp)
    return sel


def _block_importance(imp, blk, cur):
    return jnp.where((blk == cur) | (blk == 0), FORCED_SCORE, jnp.where(blk > cur, -1.0, imp))


def _ada_kernel(c_ref, w_ref, b_ref, o_ref):
    o_ref[...] = _dot(_silu(c_ref[...]).astype(BF16), w_ref[...]) + b_ref[...]


def _ada_all_layers(c_all, w_ada, b_ada):
    depth, rows = w_ada.shape[0], c_all.shape[0]
    tn = 1536
    return pl.pallas_call(
        _ada_kernel,
        grid=(depth, 6 * D_MODEL // tn),
        in_specs=[pl.BlockSpec((rows, D_MODEL), lambda l, j: (0, 0)),
                  pl.BlockSpec((None, D_MODEL, tn), lambda l, j: (l, 0, j)),
                  pl.BlockSpec((None, 1, tn), lambda l, j: (l, 0, j))],
        out_specs=pl.BlockSpec((None, rows, tn), lambda l, j: (l, 0, j)),
        out_shape=jax.ShapeDtypeStruct((depth, rows, 6 * D_MODEL), F32),
        compiler_params=_params("parallel", "parallel"),
        name="ada",
    )(c_all, w_ada, b_ada)


def _mod_operand(mod, rows_per_batch, tm):
    if rows_per_batch % tm == 0:
        per = rows_per_batch // tm
        return mod[:, None, :], pl.BlockSpec((None, 1, D_MODEL), lambda i: (i // per, 0, 0))
    return jnp.repeat(mod, rows_per_batch, axis=0), pl.BlockSpec((tm, D_MODEL), lambda i: (i, 0))


def _const_spec(shape):
    nd = len(shape)
    return pl.BlockSpec(shape, lambda i: (0,) * nd)


def _inproj_kernel(x_ref, gn_ref, sc_ref, sh_ref, wu_ref, wq_ref, wkv_ref, wg_ref, wm_ref,
                   gq_ref, gks_ref, gkw_ref, cos_ref, sin_ref, seg_ref, pe_ref, ug_ref, gm_ref, *out, transposed):
    h = _rms(x_ref[...]) * gn_ref[...]
    h = h * (1.0 + sc_ref[...]) + sh_ref[...]
    hb = h.astype(BF16)

    u = _dot(hb, wu_ref[...])
    ug_ref[...] = u[:, :D_CONV] * _sigmoid(u[:, D_CONV:])
    gm_ref[...] = _sigmoid(_dot(hb, wm_ref[...])).astype(BF16)
    g3 = _sigmoid(_dot(hb, wg_ref[...]))

    cos1, sin1 = cos_ref[...], sin_ref[...]
    cos4 = jnp.concatenate([cos1] * 4, axis=1)
    sin4 = jnp.concatenate([sin1] * 4, axis=1)
    qn = _head_rms(_dot(hb, wq_ref[...]), seg_ref[...]) * gq_ref[...]
    qr = _rope(qn, cos4, sin4) * (ATTN_SCALE * LOG2E)
    qn = qn * (ATTN_SCALE * LOG2E)

    kv = _dot(hb, wkv_ref[...])
    seg1 = seg_ref[:LANES, :LANES]
    ks = _rope(_head_rms(kv[:, 2 * LANES:3 * LANES], seg1) * gks_ref[...], cos1, sin1)
    kw = _rope(_head_rms(kv[:, 4 * LANES:5 * LANES], seg1) * gkw_ref[...], cos1, sin1)
    vs = kv[:, 3 * LANES:4 * LANES]
    vw = kv[:, 5 * LANES:]

    if not transposed:
        qn_ref, qr_ref, rows_ref, win_ref, g3_ref = out
        qn_ref[...] = qn.astype(BF16)
        qr_ref[...] = qr.astype(BF16)
        rows_ref[:, :2 * LANES] = kv[:, :2 * LANES]
        rows_ref[:, 2 * LANES:3 * LANES] = ks
        rows_ref[:, 3 * LANES:] = vs
        win_ref[:, :LANES] = kw
        win_ref[:, LANES:] = vw
        g3_ref[...] = g3
        return

    qnt_ref, qrt_ref, rowst_ref, wint_ref, g3t_ref, cmp_ref, ks_ref, kw_ref, vst_ref, vwt_ref = out
    qnt_ref[...] = qn.T.astype(BF16)
    qrt_ref[...] = qr.T.astype(BF16)
    rowst_ref[:2 * LANES, :] = kv[:, :2 * LANES].T
    rowst_ref[2 * LANES:3 * LANES, :] = ks.T
    vst = vs.T
    rowst_ref[3 * LANES:, :] = vst
    wint_ref[:LANES, :] = kw.T
    vwt = vw.T
    wint_ref[LANES:, :] = vwt
    g3t_ref[...] = g3.T[:3 * N_HEADS, :]
    cmp_ref[...] = (kv[:, :2 * LANES] + pe_ref[...]).astype(BF16)
    lane = lax.broadcasted_iota(jnp.int32, ks.shape, 1)
    blk_in_chunk = _mod(_div(lax.broadcasted_iota(jnp.int32, ks.shape, 0), CMP_BLOCK), SEL_CHUNK // CMP_BLOCK)
    onehot = jnp.where(lane - HEAD_DIM == blk_in_chunk, 1.0, 0.0)
    ks_ref[:, :LANES] = jnp.where(lane < HEAD_DIM, ks, onehot).astype(BF16)
    ks_ref[:, LANES:] = jnp.where(lane < HEAD_DIM, pltpu.roll(ks, HEAD_DIM, 1), onehot).astype(BF16)
    kw_ref[...] = kw.astype(BF16)
    ones = jnp.ones((VT_ROWS - HEAD_DIM, vst.shape[1]), BF16)
    for ref, vt in ((vst_ref, vst), (vwt_ref, vwt)):
        for g in range(N_KV):
            ref[g * VT_ROWS:g * VT_ROWS + HEAD_DIM, :] = vt[g * HEAD_DIM:(g + 1) * HEAD_DIM, :].astype(BF16)
            ref[g * VT_ROWS + HEAD_DIM:(g + 1) * VT_ROWS, :] = ones


def _inproj(x, mods, lw, tabs, rows_per_batch, tm, transposed):
    n = x.shape[0]
    sc, sc_spec = _mod_operand(mods["sc1"], rows_per_batch, tm)
    sh, sh_spec = _mod_operand(mods["sh1"], rows_per_batch, tm)
    cos, sin, tab_tiles = tabs["cos"], tabs["sin"], tabs["cos"].shape[0] // tm
    tab_spec = pl.BlockSpec((tm, LANES), lambda i: (i % tab_tiles, 0))
    row = lambda w: pl.BlockSpec((tm, w), lambda i: (i, 0))
    col = lambda h: pl.BlockSpec((h, tm), lambda i: (0, i))
    sds = jax.ShapeDtypeStruct
    out_shape = [sds((n, D_CONV), F32), sds((n, 2 * D_MODEL), BF16)]
    out_specs = [row(D_CONV), row(2 * D_MODEL)]
    if transposed:
        per = rows_per_batch // tm
        nb = n // rows_per_batch
        bcol = lambda h: pl.BlockSpec((None, h, tm), lambda i: (i // per, 0, i % per))
        out_shape += [sds((512, n), BF16), sds((512, n), BF16), sds((nb, 512, rows_per_batch), F32),
                      sds((nb, 256, rows_per_batch), F32), sds((3 * N_HEADS, n), F32), sds((n, 256), BF16),
                      sds((n, N_KV * LANES), BF16), sds((n, LANES), BF16), sds((N_KV * VT_ROWS, n), BF16),
                      sds((N_KV * VT_ROWS, n), BF16)]
        out_specs += [col(512), col(512), bcol(512), bcol(256), col(3 * N_HEADS), row(256),
                      row(N_KV * LANES), row(LANES), col(N_KV * VT_ROWS), col(N_KV * VT_ROWS)]
    else:
        out_shape += [sds((n, 512), BF16), sds((n, 512), BF16), sds((n, 512), F32), sds((n, 256), F32),
                      sds((n, LANES), F32)]
        out_specs += [row(512), row(512), row(512), row(256), row(LANES)]
    return pl.pallas_call(
        functools.partial(_inproj_kernel, transposed=transposed),
        grid=(n // tm,),
        in_specs=[row(D_MODEL), _const_spec((1, D_MODEL)), sc_spec, sh_spec,
                  _const_spec(lw["w_u"].shape), _const_spec(lw["w_q"].shape), _const_spec(lw["w_kv"].shape),
                  _const_spec(lw["w_g"].shape), _const_spec(lw["w_m"].shape),
                  _const_spec((1, 512)), _const_spec((1, LANES)), _const_spec((1, LANES)),
                  tab_spec, tab_spec, _const_spec((512, 512)), _const_spec((tm, 256))],
        out_specs=out_specs,
        out_shape=out_shape,
        compiler_params=_params("parallel"),
        name="inproj",
    )(x, lw["g_norm_mix"], sc, sh, lw["w_u"], lw["w_q"], lw["w_kv"], lw["w_g"], lw["w_m"],
      lw["g_q"], lw["g_ks"], lw["g_kw"], cos, sin, tabs["seg"], lw["pe_tile"][:tm])


def _conv_kernel(*refs, tc, n_tiles):
    if n_tiles > 1:
        cur_ref, halo_ref, pre_ref, w_ref, b_ref, g_ref, beta_ref, y_ref, win_ref = refs
    else:
        cur_ref, pre_ref, w_ref, b_ref, g_ref, beta_ref, y_ref, win_ref = refs
    if n_tiles > 1:
        first = pl.program_id(1) == 0

        @pl.when(first)
        def _():
            win_ref[:CONV_HALO, :] = pre_ref[...]

        @pl.when(jnp.logical_not(first))
        def _():
            win_ref[:CONV_HALO, :] = halo_ref[...]
    else:
        win_ref[:CONV_HALO, :] = pre_ref[...]
    win_ref[CONV_HALO:CONV_HALO + tc, :] = cur_ref[...]
    win_ref[CONV_HALO + tc:, :] = jnp.zeros((CONV_PAD, D_CONV), F32)

    rc = min(tc, 128)
    off = CONV_HALO - (CONV_WIDTH - 1)
    ext = rc + CONV_PAD
    for r0 in range(0, tc, rc):
        acc = None
        for s in range(8):
            z = None
            for a in range(-(-CONV_WIDTH // 8)):
                j = 8 * a + s
                if j < CONV_WIDTH:
                    term = w_ref[j:j + 1, :] * win_ref[r0 + 8 * a:r0 + 8 * a + ext, :]
                    z = term if z is None else z + term
            piece = z[off + s:off + s + rc]
            acc = piece if acc is None else acc + piece
        y = acc + b_ref[...]
        mu = jnp.mean(y, axis=-1, keepdims=True)
        yc = y - mu
        var = jnp.mean(yc * yc, axis=-1, keepdims=True)
        y = yc * lax.rsqrt(var + EPS) * g_ref[...] + beta_ref[...]
        y_ref[r0:r0 + rc, :] = _silu(y).astype(BF16)


def _conv_branch(ug, prefix, lw, tc):
    b, t, _ = ug.shape
    n_tiles = t // tc
    cur_spec = pl.BlockSpec((None, tc, D_CONV), lambda bi, i: (bi, i, 0))
    pre_spec = pl.BlockSpec((None, CONV_HALO, D_CONV), lambda bi, i: (bi, 0, 0))
    vec = lambda r: pl.BlockSpec((r, D_CONV), lambda bi, i: (0, 0))
    ops, specs = [ug], [cur_spec]
    if n_tiles > 1:
        per = tc // CONV_HALO
        ops.append(ug)
        specs.append(pl.BlockSpec((None, CONV_HALO, D_CONV), lambda bi, i: (bi, jnp.maximum(i * per - 1, 0), 0)))
    ops += [prefix, lw["w_dw"], lw["b_dw"], lw["ln_conv_g"], lw["ln_conv_b"]]
    specs += [pre_spec, vec(CONV_HALO), vec(1), vec(1), vec(1)]
    return pl.pallas_call(
        functools.partial(_conv_kernel, tc=tc, n_tiles=n_tiles),
        grid=(b, n_tiles),
        in_specs=specs,
        out_specs=pl.BlockSpec((None, tc, D_CONV), lambda bi, i: (bi, i, 0)),
        out_shape=jax.ShapeDtypeStruct((b, t, D_CONV), BF16),
        scratch_shapes=[pltpu.VMEM((CONV_HALO + tc + CONV_PAD, D_CONV), F32)],
        compiler_params=_params("parallel", "parallel"),
        name="conv",
    )(*ops)


def _compress_kernel(x_ref, w1_ref, b1_ref, w2_ref, b2_ref, gkc_ref, o_ref):
    hid = jax.nn.gelu(_dot(x_ref[...], w1_ref[...]) + b1_ref[...])
    o = _dot(hid.astype(BF16), w2_ref[...]) + b2_ref[...]
    is_key = pl.program_id(0) == 0
    o_ref[...] = jnp.where(is_key, _rms(o) * gkc_ref[...], o)


def _compress(flat, lw):
    _, m, kdim = flat.shape
    tm = min(m, 512)
    hidden = lw["cmp_w1"].shape[-1]
    return pl.pallas_call(
        _compress_kernel,
        grid=(2, m // tm),
        in_specs=[pl.BlockSpec((None, tm, kdim), lambda s, i: (s, i, 0)),
                  pl.BlockSpec((None, kdim, hidden), lambda s, i: (s, 0, 0)),
                  pl.BlockSpec((None, 1, hidden), lambda s, i: (s, 0, 0)),
                  pl.BlockSpec((None, hidden, HEAD_DIM), lambda s, i: (s, 0, 0)),
                  pl.BlockSpec((None, 1, HEAD_DIM), lambda s, i: (s, 0, 0)),
                  pl.BlockSpec((1, HEAD_DIM), lambda s, i: (0, 0))],
        out_specs=pl.BlockSpec((None, tm, HEAD_DIM), lambda s, i: (s, i, 0)),
        out_shape=jax.ShapeDtypeStruct((2, m, HEAD_DIM), F32),
        compiler_params=_params("parallel", "parallel"),
        name="compress",
    )(flat, lw["cmp_w1"], lw["cmp_b1"], lw["cmp_w2"], lw["cmp_b2"], lw["g_kc"])


def _flatten_blocks(cmp_in, b, n_blk):
    x = cmp_in.reshape(b, n_blk, CMP_BLOCK, 2, N_KV, HEAD_DIM)
    x = x.transpose(3, 0, 4, 1, 2, 5)
    return x.reshape(2, b * N_KV * n_blk, CMP_BLOCK * HEAD_DIM)


def _page_copies(pt_ref, cache_ref, buf_ref, sem_ref, layer, slot, seq, buf_slot, n_pages):
    rows = N_KV * HEAD_DIM
    return [pltpu.make_async_copy(cache_ref.at[layer, pt_ref[seq, p], slot],
                                  buf_ref.at[buf_slot, pl.ds(p * rows, rows), :], sem_ref.at[buf_slot])
            for p in range(n_pages)]


def _compress_paged_kernel(pt_ref, cache_ref, w_ref, pe_ref, b1_ref, w2_ref, b2_ref, gkc_ref, seg_ref, o_ref,
                           buf_ref, sem_ref, *, n_pages, n_seq):
    n_rows = n_pages * N_KV
    step = (pl.program_id(0) * 2 + pl.program_id(1)) * n_seq + pl.program_id(2)
    n_steps = pl.num_programs(0) * 2 * n_seq
    cur = step % 2

    def copies(s, buf_slot):
        seq = s % n_seq
        return _page_copies(pt_ref, cache_ref, buf_ref, sem_ref, s // (2 * n_seq), (s // n_seq) % 2, seq, buf_slot,
                            n_pages)

    @pl.when(step == 0)
    def _():
        for cp in copies(step, cur):
            cp.start()

    @pl.when(step + 1 < n_steps)
    def _():
        for cp in copies(step + 1, 1 - cur):
            cp.start()

    for cp in copies(step, cur):
        cp.wait()

    acc = jnp.zeros((n_rows, w_ref.shape[-1]), F32)
    for d in range(0, HEAD_DIM, 2):
        a = jnp.concatenate(
            [(buf_ref[cur, pl.ds(d + u, n_rows, stride=HEAD_DIM), :] + pe_ref[d + u:d + u + 1, :]).astype(BF16)
             for u in range(2)], axis=1)
        acc = acc + _dot(a, w_ref[d // 2])
    hid = jax.nn.gelu(acc + b1_ref[...])
    o = _dot(hid.astype(BF16), w2_ref[...]) + b2_ref[...]
    is_key = pl.program_id(1) == 0
    o_ref[...] = jnp.where(is_key, _head_rms(o, seg_ref[...]) * gkc_ref[...], o)


def _compress_paged(page_table, cache_pages, pw):
    depth = cache_pages.shape[0]
    db, n_pages = page_table.shape
    n_rows = n_pages * N_KV
    hid2 = pw["w1"].shape[-1]
    sel = lambda *s: pl.BlockSpec((None, None) + s, lambda l, sl, b, pt: (l, sl) + (0,) * len(s))
    grid_spec = pltpu.PrefetchScalarGridSpec(
        num_scalar_prefetch=1,
        grid=(depth, 2, db),
        in_specs=[pl.BlockSpec(memory_space=pl.ANY),
                  sel(HEAD_DIM // 2, 2 * PAGE_SIZE, hid2), sel(HEAD_DIM, PAGE_SIZE), sel(1, hid2), sel(hid2, LANES),
                  sel(1, LANES), pl.BlockSpec((None, 1, LANES), lambda l, sl, b, pt: (l, 0, 0)),
                  pl.BlockSpec((LANES, LANES), lambda l, sl, b, pt: (0, 0))],
        out_specs=pl.BlockSpec((None, None, None, n_rows, LANES), lambda l, sl, b, pt: (l, sl, b, 0, 0)),
        scratch_shapes=[pltpu.VMEM((2, n_rows * HEAD_DIM, PAGE_SIZE), F32), pltpu.SemaphoreType.DMA((2,))])
    return pl.pallas_call(
        functools.partial(_compress_paged_kernel, n_pages=n_pages, n_seq=db),
        grid_spec=grid_spec,
        out_shape=jax.ShapeDtypeStruct((depth, 2, db, n_rows, LANES), F32),
        compiler_params=_params("arbitrary", "arbitrary", "arbitrary"),
        name="compress_paged",
    )(page_table, cache_pages, pw["w1"], pw["pe"], pw["b1"], pw["w2"], pw["b2"], pw["g_kc"], pw["seg"])


def _attn_prompt_kernel(qn_ref, qr_ref, g3_ref, kc_ref, vc_ref, ks_ref, vst_ref, kw_ref, vwt_ref,
                        o_ref, bias_ref, ocmp_ref, ms_ref, accs_ref, mw_ref, accw_ref, *, tq):
    i = pl.program_id(1)
    t0 = i * tq
    lq = Q_PER_KV * tq
    lane = lax.broadcasted_iota(jnp.int32, (1, lq), 1)
    qpos = t0 + _mod(lane, tq)
    nb = kc_ref.shape[1]
    blk = lax.broadcasted_iota(jnp.int32, (nb, 1), 0)
    zpad = jnp.zeros((HEAD_DIM, lq), BF16)

    def heads(ref, g):
        return jnp.concatenate([ref[(Q_PER_KV * g + r) * HEAD_DIM:(Q_PER_KV * g + r + 1) * HEAD_DIM, :]
                                for r in range(Q_PER_KV)], axis=1)

    qpad = []
    for g in range(N_KV):
        q4 = heads(qr_ref, g)
        qpad.append(jnp.concatenate([q4, zpad] if g == 0 else [zpad, q4], axis=0))

        cm = blk < _div(qpos + 1, CMP_BLOCK)
        s = jnp.where(cm, _dot(kc_ref[g].astype(BF16), heads(qn_ref, g)), NEG)
        p = jnp.exp2(s - jnp.max(s, axis=0, keepdims=True)) * cm.astype(F32)
        p = p / jnp.maximum(jnp.sum(p, axis=0, keepdims=True), 1e-30)
        ocmp_ref[g] = _dot_t0(vc_ref[g].astype(BF16), p.astype(BF16))

        imp = p[:, :tq]
        for r in range(1, Q_PER_KV):
            imp = imp + p[:, r * tq:(r + 1) * tq]
        imp = _block_importance(imp, blk, _div(qpos[:, :tq], CMP_BLOCK))
        bias_ref[g] = (_topk_mask(imp, N_SELECT) - 1.0) * (-NEG)

    for m_ref, acc_ref in ((ms_ref, accs_ref), (mw_ref, accw_ref)):
        m_ref[...] = jnp.full(m_ref.shape, NEG, F32)
        acc_ref[...] = jnp.zeros(acc_ref.shape, F32)

    kidx = lax.broadcasted_iota(jnp.int32, (KEY_CHUNK, 1), 0)

    tpos = qpos[:, :tq]

    chains = [(g, slice(r * tq, (r + 1) * tq)) for g in range(N_KV) for r in range(Q_PER_KV)]

    def qk(k_ref, kc0, n_keys):
        kblk = k_ref[pl.ds(kc0, n_keys), :]
        return tuple(_dot(kblk, qpad[g][:, sl]) for g, sl in chains)

    def softmax_pv(scores, bias, vt_ref, kc0, n_keys, m_ref, acc_ref):
        m_all, acc_all = m_ref[...], acc_ref[...]
        m_out, alphas, probs = [], [], []
        for (g, sl), s in zip(chains, scores):
            if bias is not None:
                s = s + bias[g]
            m = m_all[g, :, sl]
            m_new = jnp.maximum(m, jnp.max(s, axis=0, keepdims=True))
            m_out.append(m_new)
            alphas.append(jnp.exp2(m - m_new))
            probs.append(jnp.exp2(s - m_new).astype(BF16))
        vts = [vt_ref[g * VT_ROWS:(g + 1) * VT_ROWS, pl.ds(kc0, n_keys)] for g in range(N_KV)]
        acc_out = [alpha * acc_all[g, :, sl] + _dot(vts[g], p)
                   for (g, sl), alpha, p in zip(chains, alphas, probs)]
        gather = lambda parts: jnp.stack([jnp.concatenate(parts[g * Q_PER_KV:(g + 1) * Q_PER_KV], axis=1)
                                          for g in range(N_KV)])
        m_ref[...] = gather(m_out)
        acc_ref[...] = gather(acc_out)

    kidx2 = lax.broadcasted_iota(jnp.int32, (SEL_CHUNK, 1), 0)

    def sel_chunk(c, causal):
        k0 = pl.multiple_of(c * SEL_CHUNK, SEL_CHUNK)
        kblk = ks_ref[pl.ds(k0, SEL_CHUNK), :]
        pad = jnp.zeros((HEAD_DIM - 16, tq), BF16)
        scores = []
        for g in range(N_KV):
            rows8 = bias_ref[g, pl.ds(pl.multiple_of(c * 8, 8), 8), :]
            rows16 = jnp.concatenate([rows8, jnp.zeros((8, tq), F32)], axis=0).astype(BF16)
            for r in range(Q_PER_KV):
                head = Q_PER_KV * g + r
                q_op = jnp.concatenate([qr_ref[head * HEAD_DIM:(head + 1) * HEAD_DIM, :], rows16, pad], axis=0)
                scores.append(_dot(kblk[:, g * LANES:(g + 1) * LANES], q_op))
        bias = [jnp.where(k0 + kidx2 <= tpos, 0.0, NEG)] * N_KV if causal else None
        softmax_pv(scores, bias, vst_ref, k0, SEL_CHUNK, ms_ref, accs_ref)

    def sel_body(c, carry):
        sel_chunk(c, False)
        return carry

    last = (t0 + tq - 1) // SEL_CHUNK
    lax.fori_loop(0, last, sel_body, 0)
    sel_chunk(last, True)

    n_win = WINDOW // KEY_CHUNK
    for c in range(n_win + 1):
        k0 = t0 - WINDOW + c * KEY_CHUNK
        start = pl.multiple_of(jnp.maximum(k0, 0), KEY_CHUNK)
        if c == n_win:
            bias_w = jnp.where(k0 + kidx <= tpos, 0.0, NEG)
        elif c == 0:
            bias_w = jnp.where(tpos - (k0 + kidx) < jnp.where(k0 >= 0, WINDOW, -(2 ** 30)), 0.0, NEG)
        else:
            bias_w = jnp.broadcast_to(jnp.where(k0 >= 0, 0.0, NEG), (1, 1))
        softmax_pv(qk(kw_ref, start, KEY_CHUNK), [bias_w] * N_KV, vwt_ref, start, KEY_CHUNK, mw_ref, accw_ref)

    for g in range(N_KV):
        gate = lambda br: jnp.concatenate(
            [g3_ref[br * N_HEADS + Q_PER_KV * g + r:br * N_HEADS + Q_PER_KV * g + r + 1, :]
             for r in range(Q_PER_KV)], axis=1)
        branch = lambda acc_ref: acc_ref[g, :HEAD_DIM, :] / acc_ref[g, HEAD_DIM:HEAD_DIM + 1, :]
        o = gate(0) * ocmp_ref[g] + gate(1) * branch(accs_ref) + gate(2) * branch(accw_ref)
        for r in range(Q_PER_KV):
            o_ref[(Q_PER_KV * g + r) * HEAD_DIM:(Q_PER_KV * g + r + 1) * HEAD_DIM, :] = (
                o[:, r * tq:(r + 1) * tq].astype(BF16))


def _attn_prompt(qn_t, qr_t, g3_t, kc, vc, ks, vst, kw, vwt, b, t):
    tq = KEY_CHUNK
    nq = t // tq
    lq = Q_PER_KV * tq
    n_cmp = kc.shape[2]
    qcol = lambda h: pl.BlockSpec((h, tq), lambda bi, i: (0, bi * nq + i))
    per_b = pl.BlockSpec((None, N_KV, n_cmp, HEAD_DIM), lambda bi, i: (bi, 0, 0, 0))
    keys = pl.BlockSpec((t, LANES), lambda bi, i: (bi, 0))
    sel_keys = pl.BlockSpec((t, N_KV * LANES), lambda bi, i: (bi, 0))
    vals = pl.BlockSpec((N_KV * VT_ROWS, t), lambda bi, i: (0, bi))
    st = lambda r: pltpu.VMEM((N_KV, r, lq), F32)
    return pl.pallas_call(
        functools.partial(_attn_prompt_kernel, tq=tq),
        grid=(b, nq),
        in_specs=[qcol(512), qcol(512), qcol(3 * N_HEADS), per_b, per_b, sel_keys, vals, keys, vals],
        out_specs=qcol(512),
        out_shape=jax.ShapeDtypeStruct((512, b * t), BF16),
        scratch_shapes=[pltpu.VMEM((N_KV, n_cmp, tq), F32), st(HEAD_DIM),
                        st(1), st(VT_ROWS), st(1), st(VT_ROWS)],
        compiler_params=_params("parallel", "arbitrary"),
        name="attn_prompt",
    )(qn_t, qr_t, g3_t, kc, vc, ks, vst, kw, vwt)


def _attn_sample_kernel(pt_ref, *refs, n_pages, n_blk, pos0, tq):
    pages = refs[:ATTN_PAGES]
    (q_ref, qnt_ref, kc_ref, vc_ref, kn_ref, vn_ref, kwn_ref, vwn_ref, win_ref, g3_ref, expand_ref, rsum_ref,
     o_ref, sel_ref, m_ref, l_ref, acc_ref, ocmp_ref) = refs[ATTN_PAGES:]
    del pt_ref
    j = pl.program_id(1)
    lq = Q_PER_KV * tq
    nbp = sel_ref.shape[1]
    n_cmp = kc_ref.shape[1]
    step_keys = ATTN_PAGES * PAGE_SIZE

    @pl.when(j == 0)
    def _():
        lane = lax.broadcasted_iota(jnp.int32, (1, lq), 1)
        qpos = pos0 + _mod(lane, tq)
        blk = lax.broadcasted_iota(jnp.int32, (nbp, 1), 0)
        for g in range(N_KV):
            cm = blk[:n_cmp] < _div(qpos + 1, CMP_BLOCK)
            s = jnp.where(cm, _dot(kc_ref[g].astype(BF16), qnt_ref[g]), NEG)
            p = jnp.exp2(s - jnp.max(s, axis=0, keepdims=True)) * cm.astype(F32)
            p = p / jnp.maximum(jnp.sum(p, axis=0, keepdims=True), 1e-30)
            ocmp_ref[g] = _dot_t0(p.astype(BF16), vc_ref[g].astype(BF16))
            imp = _split_dot(p, rsum_ref[...])
            imp = jnp.concatenate([imp, jnp.zeros((nbp - n_cmp, lq), F32)], axis=0)
            imp = _block_importance(imp, blk, _div(qpos, CMP_BLOCK))
            imp = jnp.where(blk >= n_blk, -jnp.inf, imp)
            sel_ref[g] = _topk_mask(imp, N_SELECT)
        m_ref[...] = jnp.full(m_ref.shape, NEG, F32)
        l_ref[...] = jnp.zeros(l_ref.shape, F32)
        acc_ref[...] = jnp.zeros(acc_ref.shape, F32)

    def update(g, s, pv):
        m = m_ref[g]
        m_new = jnp.maximum(m, jnp.max(s, axis=1, keepdims=True))
        alpha = jnp.exp2(m - m_new)
        p = jnp.exp2(s - m_new)
        l_ref[g] = alpha * l_ref[g] + jnp.sum(p, axis=1, keepdims=True)
        acc_ref[g] = alpha * acc_ref[g] + pv(p)
        m_ref[g] = m_new

    step_blocks = step_keys // CMP_BLOCK
    expand = expand_ref[...]
    scores = []
    for g in range(N_KV):
        kt = jnp.concatenate([pg[0, g].astype(BF16) for pg in pages], axis=1)
        sel_rows = sel_ref[g, pl.ds(pl.multiple_of(j * step_blocks, step_blocks), step_blocks), :]
        picked = _dot_t0(sel_rows.astype(BF16), expand)
        scores.append(_dot(q_ref[g], kt) + (picked - 1.0) * (-NEG))
    m_all, l_all, acc_all = m_ref[...], l_ref[...], acc_ref[...]
    m_out, l_out, alphas, probs = [], [], [], []
    for g, s in enumerate(scores):
        m_new = jnp.maximum(m_all[g], jnp.max(s, axis=1, keepdims=True))
        alpha = jnp.exp2(m_all[g] - m_new)
        p = jnp.exp2(s - m_new)
        m_out.append(m_new)
        l_out.append(alpha * l_all[g] + jnp.sum(p, axis=1, keepdims=True))
        alphas.append(alpha)
        probs.append(p.astype(BF16))
    acc_out = []
    for g in range(N_KV):
        vt = jnp.concatenate([pg[1, g].astype(BF16) for pg in pages], axis=1)
        acc_out.append(alphas[g] * acc_all[g] + _dot_t1(probs[g], vt))
    m_ref[...] = jnp.stack(m_out)
    l_ref[...] = jnp.stack(l_out)
    acc_ref[...] = jnp.stack(acc_out)

    @pl.when(j == n_pages // ATTN_PAGES - 1)
    def _():
        rowi = _mod(lax.broadcasted_iota(jnp.int32, (lq, 1), 0), tq)
        keyi = lax.broadcasted_iota(jnp.int32, (1, tq), 1)
        causal = keyi <= rowi
        wlane = lax.broadcasted_iota(jnp.int32, (1, WINDOW), 1)
        kpos = pos0 - WINDOW + wlane
        in_win = ((pos0 + rowi) - kpos < WINDOW) & (kpos >= 0)
        cur_blk = pos0 // CMP_BLOCK
        first_row = (lax.broadcasted_iota(jnp.int32, (8, tq), 0) == cur_blk % 8).astype(F32)
        for g in range(N_KV):
            q = q_ref[g]
            qf = q.astype(F32)
            base = (cur_blk // 8) * 8
            picked = _dot_t0(sel_ref[g, base:base + 8, :], first_row)
            s = jnp.where(causal & (picked > 0.5), _dot_t1(qf, kn_ref[g]), NEG)
            update(g, s, lambda p: _dot(p, vn_ref[g]))
            o_sel = acc_ref[g] / l_ref[g]

            s_pre = jnp.where(in_win, _dot(q, win_ref[0, g].astype(BF16)), NEG)
            s_new = jnp.where(causal, _dot_t1(qf, kwn_ref[g]), NEG)
            m = jnp.maximum(jnp.max(s_pre, axis=1, keepdims=True), jnp.max(s_new, axis=1, keepdims=True))
            p_pre = jnp.exp2(s_pre - m) * in_win.astype(F32)
            p_new = jnp.exp2(s_new - m) * causal.astype(F32)
            l = jnp.sum(p_pre, axis=1, keepdims=True) + jnp.sum(p_new, axis=1, keepdims=True)
            o_win = (_dot_t1(p_pre.astype(BF16), win_ref[1, g].astype(BF16)) + _dot(p_new, vwn_ref[g]))
            o_win = o_win / jnp.maximum(l, 1e-30)

            g3 = g3_ref[g]
            o_ref[g] = g3[:, 0:1] * ocmp_ref[g] + g3[:, 1:2] * o_sel + g3[:, 2:3] * o_win


def _attn_sample(page_table, cache_t, layer, q, qn_t, kc, vc, kn, vn, kwn, vwn, win_t, g3, expand, rsum, pos0, tq):
    db, n_pages = page_table.shape
    lq = Q_PER_KV * tq
    n_cmp = kc.shape[2]
    n_blk = -(-(pos0 + tq) // CMP_BLOCK)
    nbp = -(-n_blk // 8) * 8
    steps = n_pages // ATTN_PAGES

    def page_spec(k):
        return pl.BlockSpec((None, None, 2, N_KV, HEAD_DIM, PAGE_SIZE),
                            lambda b, j, pt: (layer, pt[b, j * ATTN_PAGES + k], 1, 0, 0, 0))

    per_b = lambda *s: pl.BlockSpec((None,) + s, lambda b, j, pt: (b,) + (0,) * len(s))
    win_spec = pl.BlockSpec((None, None, 2, N_KV, HEAD_DIM, WINDOW), lambda b, j, pt: (layer, b, 0, 0, 0, 0))
    const = lambda *s: pl.BlockSpec(s, lambda b, j, pt: (0,) * len(s))
    grid_spec = pltpu.PrefetchScalarGridSpec(
        num_scalar_prefetch=1,
        grid=(db, steps),
        in_specs=[page_spec(k) for k in range(ATTN_PAGES)] + [
            per_b(N_KV, lq, HEAD_DIM), per_b(N_KV, HEAD_DIM, lq), per_b(N_KV, n_cmp, HEAD_DIM),
            per_b(N_KV, n_cmp, HEAD_DIM), per_b(N_KV, tq, HEAD_DIM), per_b(N_KV, tq, HEAD_DIM),
            per_b(N_KV, tq, HEAD_DIM), per_b(N_KV, tq, HEAD_DIM), win_spec,
            per_b(N_KV, lq, 8), const(ATTN_PAGES * PAGE_SIZE // CMP_BLOCK, ATTN_PAGES * PAGE_SIZE), const(lq, lq)],
        out_specs=per_b(N_KV, lq, HEAD_DIM),
        scratch_shapes=[pltpu.VMEM((N_KV, nbp, lq), F32), pltpu.VMEM((N_KV, lq, 1), F32),
                        pltpu.VMEM((N_KV, lq, 1), F32), pltpu.VMEM((N_KV, lq, HEAD_DIM), F32),
                        pltpu.VMEM((N_KV, lq, HEAD_DIM), F32)])
    return pl.pallas_call(
        functools.partial(_attn_sample_kernel, n_pages=n_pages, n_blk=n_blk, pos0=pos0, tq=tq),
        grid_spec=grid_spec,
        out_shape=jax.ShapeDtypeStruct((db, N_KV, lq, HEAD_DIM), F32),
        compiler_params=_params("parallel", "arbitrary"),
        name="attn_sample",
    )(page_table, *([cache_t] * ATTN_PAGES), q, qn_t, kc, vc, kn, vn, kwn, vwn, win_t, g3, expand, rsum)


def _merge_kernel(x_ref, y_ref, o_ref, gm_ref, gt_ref, sc_ref, sh_ref, gn_ref, wc_ref, wa_ref, wo_ref,
                  wr_ref, br_ref, x1_ref, h2_ref, dw_ref, *, o_transposed):
    tm = x_ref.shape[0]
    halves = [slice(h * (tm // 2), (h + 1) * (tm // 2)) for h in range(2)]
    rows_of = lambda ref, rows: ref[...] if ref.shape[0] == 1 else ref[rows, :]

    branch = []
    for rows in halves:
        attn = _dot_t0(o_ref[:, rows], wa_ref[...]) if o_transposed else _dot(o_ref[rows, :], wa_ref[...])
        branch.append((_dot(y_ref[rows, :], wc_ref[...]), attn))
    mixed = []
    for rows, (conv, attn) in zip(halves, branch):
        gm = gm_ref[rows, :]
        merged = gm[:, :D_MODEL].astype(F32) * conv + gm[:, D_MODEL:].astype(F32) * attn
        mixed.append(_dot(merged.astype(BF16), wo_ref[...]))
    x1s, h2s, affs = [], [], []
    for rows, mix in zip(halves, mixed):
        x1 = x_ref[rows, :] + rows_of(gt_ref, rows) * mix
        h2 = _rms(x1) * gn_ref[...]
        h2 = h2 * (1.0 + rows_of(sc_ref, rows)) + rows_of(sh_ref, rows)
        h_hi = h2.astype(BF16)
        h_lo = (h2 - h_hi.astype(F32)).astype(BF16)
        x1s.append(x1)
        h2s.append(h_hi)
        affs.append(_sigmoid(_dot(h_hi, wr_ref[0]) + _dot(h_lo, wr_ref[0]) + _dot(h_hi, wr_ref[1])))
    x1_ref[...] = jnp.concatenate(x1s, axis=0)
    h2_ref[...] = jnp.concatenate(h2s, axis=0)

    dws = []
    for aff in affs:
        score = aff + br_ref[...]
        e = lax.broadcasted_iota(jnp.int32, score.shape, 1)
        grp = _div(e, EXPERTS_PER_GROUP)
        e = e.astype(F32)
        big = float(LANES)

        def top2(j):
            vals = jnp.where(grp == j, score, -jnp.inf)
            m1 = jnp.max(vals, axis=-1, keepdims=True)
            i1 = jnp.min(jnp.where(vals == m1, e, big), axis=-1, keepdims=True)
            vals = jnp.where(e == i1, -jnp.inf, vals)
            m2 = jnp.max(vals, axis=-1, keepdims=True)
            i2 = jnp.min(jnp.where(vals == m2, e, big), axis=-1, keepdims=True)
            return m1 + m2, i1, i2

        best, ia, ib = top2(0)
        for j in range(1, N_GROUPS):
            gs, i1, i2 = top2(j)
            better = gs > best
            best = jnp.where(better, gs, best)
            ia = jnp.where(better, i1, ia)
            ib = jnp.where(better, i2, ib)
        hit_a = e == ia
        hit_b = e == ib
        aff_a = jnp.sum(jnp.where(hit_a, aff, 0.0), axis=-1, keepdims=True)
        aff_b = jnp.sum(jnp.where(hit_b, aff, 0.0), axis=-1, keepdims=True)
        tot = aff_a + aff_b
        dws.append(jnp.where(hit_a, aff_a / tot, 0.0) + jnp.where(hit_b, aff_b / tot, 0.0))
    dw_ref[...] = jnp.concatenate(dws, axis=0)


def _merge(x, y, o, gm, mods, lw, shared, rows_per_batch, tm, o_transposed):
    n = x.shape[0]
    gt, gt_spec = _mod_operand(mods["gt1"], rows_per_batch, tm)
    sc, sc_spec = _mod_operand(mods["sc2"], rows_per_batch, tm)
    sh, sh_spec = _mod_operand(mods["sh2"], rows_per_batch, tm)
    row = lambda w: pl.BlockSpec((tm, w), lambda i: (i, 0))
    o_spec = pl.BlockSpec((512, tm), lambda i: (0, i)) if o_transposed else row(512)
    return pl.pallas_call(
        functools.partial(_merge_kernel, o_transposed=o_transposed),
        grid=(n // tm,),
        in_specs=[row(D_MODEL), row(D_CONV), o_spec, row(2 * D_MODEL), gt_spec, sc_spec, sh_spec,
                  _const_spec((1, D_MODEL)), _const_spec((D_CONV, D_MODEL)), _const_spec((512, D_MODEL)),
                  _const_spec((D_MODEL, D_MODEL)), _const_spec((2, D_MODEL, LANES)), _const_spec((1, LANES))],
        out_specs=[row(D_MODEL), row(D_MODEL), row(LANES)],
        out_shape=[jax.ShapeDtypeStruct((n, D_MODEL), F32), jax.ShapeDtypeStruct((n, D_MODEL), BF16),
                   jax.ShapeDtypeStruct((n, LANES), F32)],
        compiler_params=_params("parallel"),
        name="merge",
    )(x, y, o, gm, gt, sc, sh, lw["g_norm_ffn"], lw["w_conv_out"], lw["w_attn_out"], lw["w_out"],
      shared["w_router"], shared["b_router"])


def _moe_kernel(h_ref, dw_ref, x_ref, gt_ref, wg_ref, wu_ref, wd_ref, o_ref, acc_ref):
    ex = pl.program_id(1)

    @pl.when(ex == 0)
    def _():
        acc_ref[...] = jnp.zeros(acc_ref.shape, F32)

    h = h_ref[...]
    dw = dw_ref[...]
    lane = lax.broadcasted_iota(jnp.int32, dw.shape, 1)
    part = None
    for j in range(MOE_EXPERTS_PER_STEP):
        w_col = jnp.sum(jnp.where(lane == ex * MOE_EXPERTS_PER_STEP + j, dw, 0.0), axis=-1, keepdims=True)
        hid = _silu(_dot(h, wg_ref[j])) * _dot(h, wu_ref[j]) * w_col
        out = _dot(hid.astype(BF16), wd_ref[j])
        part = out if part is None else part + out
    acc_ref[...] += part

    @pl.when(ex == N_EXPERTS // MOE_EXPERTS_PER_STEP - 1)
    def _():
        o_ref[...] = x_ref[...] + gt_ref[...] * acc_ref[...]


def _moe(h2, dw, x1, mods, lw, rows_per_batch, tm):
    n = h2.shape[0]
    gt, gt_spec = _mod_operand(mods["gt2"], rows_per_batch, tm)
    gt_spec = pl.BlockSpec(gt_spec.block_shape, lambda i, ex, f=gt_spec.index_map: f(i))
    row = lambda w: pl.BlockSpec((tm, w), lambda i, ex: (i, 0))
    return pl.pallas_call(
        _moe_kernel,
        grid=(n // tm, N_EXPERTS // MOE_EXPERTS_PER_STEP),
        in_specs=[row(D_MODEL), row(LANES), row(D_MODEL), gt_spec,
                  pl.BlockSpec((MOE_EXPERTS_PER_STEP, D_MODEL, D_EXPERT), lambda i, ex: (ex, 0, 0)),
                  pl.BlockSpec((MOE_EXPERTS_PER_STEP, D_MODEL, D_EXPERT), lambda i, ex: (ex, 0, 0)),
                  pl.BlockSpec((MOE_EXPERTS_PER_STEP, D_EXPERT, D_MODEL), lambda i, ex: (ex, 0, 0))],
        out_specs=row(D_MODEL),
        out_shape=jax.ShapeDtypeStruct((n, D_MODEL), F32),
        scratch_shapes=[pltpu.VMEM((tm, D_MODEL), F32)],
        compiler_params=_params("parallel", "arbitrary"),
        name="moe",
    )(h2, dw, x1, gt, lw["w_exp_gate"], lw["w_exp_up"], lw["w_exp_down"])


def _rope_tables(pos):
    half = HEAD_DIM // 2
    inv = ROPE_THETA ** (-jnp.arange(half, dtype=F32) / half)
    ang = pos.astype(F32)[:, None] * inv[None, :]
    cos, sin = jnp.cos(ang), jnp.sin(ang)
    cos = jnp.concatenate([cos, cos], axis=1)
    sin = jnp.concatenate([-sin, sin], axis=1)
    return jnp.concatenate([cos, cos], axis=1), jnp.concatenate([sin, sin], axis=1)


def _split_mod(mod):
    names = ("sh1", "sc1", "gt1", "sh2", "sc2", "gt2")
    return dict(zip(names, jnp.split(mod, 6, axis=-1)))


def _paged_compress_weights(cmp_pe, cmp_w1, cmp_b1, cmp_w2, cmp_b2, g_kc, seg):
    depth = cmp_w1.shape[0]
    hidden = cmp_w1.shape[-1]
    def block_diag2(w):
        z = jnp.zeros_like(w)
        return jnp.concatenate([jnp.concatenate([w, z], axis=-1), jnp.concatenate([z, w], axis=-1)], axis=-2)

    w1 = cmp_w1.astype(BF16).reshape(depth, 2, CMP_BLOCK, HEAD_DIM, hidden).transpose(0, 1, 3, 2, 4)
    w1 = block_diag2(w1).reshape(depth, 2, HEAD_DIM // 2, 2 * PAGE_SIZE, 2 * hidden)
    w2 = block_diag2(cmp_w2.astype(BF16))
    pe = jnp.tile(cmp_pe.transpose(0, 1, 3, 2), (1, 1, 1, 2))
    return dict(w1=w1, w2=w2, pe=pe,
                b1=jnp.tile(cmp_b1, (1, 1, 2))[:, :, None, :], b2=jnp.tile(cmp_b2, (1, 1, 2))[:, :, None, :],
                g_kc=jnp.tile(g_kc, (1, 2))[:, None, :], seg=seg[:LANES, :LANES])


def kernel(x_prompt, x_sample, c_prompt, c_sample, cache_kv, state_win_kv, state_conv, page_table, w_ada, b_ada, g_norm_mix, g_norm_ffn, w_in, w_dw, b_dw, ln_conv_g, ln_conv_b, w_conv_out, g_q, g_kc, g_ks, g_kw, cmp_pe, cmp_w1, cmp_b1, cmp_w2, cmp_b2, w_attn_out, w_out, w_router, b_router, w_exp_gate, w_exp_up, w_exp_down):
    depth = w_in.shape[0]
    b, t, _ = x_prompt.shape
    db, dt, _ = x_sample.shape
    n_pages = page_table.shape[1]
    past_len = n_pages * PAGE_SIZE
    win_buf = state_win_kv.shape[2]
    tm_p, tm_s = 512, db * dt
    lq_s = Q_PER_KV * dt
    n_cmp_s = (past_len + dt) // CMP_BLOCK
    assert t % tm_p == 0 and t % KEY_CHUNK == 0 and win_buf == WINDOW and WINDOW % KEY_CHUNK == 0
    assert n_cmp_s * CMP_BLOCK == past_len and n_pages % ATTN_PAGES == 0

    c0, c1, c2, c3 = 2 * D_CONV, 2 * D_CONV + 512, 2 * D_CONV + 512 + 768, 2 * D_CONV + 512 + 768 + 3 * N_HEADS
    w_in_b = w_in.astype(BF16)
    w_dw_p = jnp.pad(w_dw, ((0, 0), (0, CONV_HALO - CONV_WIDTH), (0, 0)))
    pe256 = cmp_pe[:, :, :, None, :].repeat(N_KV, axis=3).transpose(0, 2, 1, 3, 4).reshape(depth, CMP_BLOCK, 256)
    pe_tile = jnp.tile(pe256, (1, tm_p // CMP_BLOCK, 1))
    layers = []
    for l in range(depth):
        layers.append(dict(
            g_norm_mix=g_norm_mix[l][None], g_norm_ffn=g_norm_ffn[l][None],
            w_u=w_in_b[l][:, :c0], w_q=w_in_b[l][:, c0:c1], w_kv=w_in_b[l][:, c1:c2],
            w_g=jnp.pad(w_in_b[l][:, c2:c3], ((0, 0), (0, LANES - 3 * N_HEADS))), w_m=w_in_b[l][:, c3:],
            g_q=jnp.tile(g_q[l], N_HEADS)[None], g_ks=jnp.tile(g_ks[l], N_KV)[None], g_kw=jnp.tile(g_kw[l], N_KV)[None],
            g_kc=g_kc[l][None], pe_tile=pe_tile[l],
            w_dw=w_dw_p[l], b_dw=b_dw[l][None], ln_conv_g=ln_conv_g[l][None], ln_conv_b=ln_conv_b[l][None],
            w_conv_out=w_conv_out[l].astype(BF16), w_attn_out=w_attn_out[l].astype(BF16), w_out=w_out[l].astype(BF16),
            cmp_w1=cmp_w1[l].astype(BF16), cmp_b1=cmp_b1[l][:, None, :], cmp_w2=cmp_w2[l].astype(BF16),
            cmp_b2=cmp_b2[l][:, None, :],
            w_exp_gate=w_exp_gate[l].astype(BF16), w_exp_up=w_exp_up[l].astype(BF16),
            w_exp_down=w_exp_down[l].astype(BF16)))
    wr = jnp.pad(w_router, ((0, 0), (0, LANES - N_EXPERTS)))
    wr_hi = wr.astype(BF16)
    wr_lo = (wr - wr_hi.astype(F32)).astype(BF16)
    shared = dict(w_router=jnp.stack([wr_hi, wr_lo]),
                  b_router=jnp.pad(b_router, (0, LANES - N_EXPERTS))[None])
    seg = (jnp.arange(512)[:, None] // HEAD_DIM == jnp.arange(512)[None, :] // HEAD_DIM).astype(BF16)
    cos_p, sin_p = _rope_tables(jnp.arange(t, dtype=jnp.int32))
    cos_s, sin_s = _rope_tables(past_len + jnp.arange(dt, dtype=jnp.int32))
    tabs_p = dict(cos=cos_p, sin=sin_p, seg=seg)
    tabs_s = dict(cos=jnp.tile(cos_s, (db, 1)), sin=jnp.tile(sin_s, (db, 1)), seg=seg)
    rsum = (jnp.arange(lq_s)[:, None] % dt == jnp.arange(lq_s)[None, :] % dt).astype(BF16)
    step_keys = ATTN_PAGES * PAGE_SIZE
    expand = (jnp.arange(step_keys // CMP_BLOCK)[:, None] == jnp.arange(step_keys)[None, :] // CMP_BLOCK).astype(BF16)

    rows = b + db
    rows_p = -(-rows // 8) * 8
    c_all = jnp.pad(jnp.concatenate([c_prompt, c_sample], axis=0), ((0, rows_p - rows), (0, 0)))
    mod_all = _ada_all_layers(c_all, w_ada.astype(BF16), b_ada[:, None, :])

    cache_t = cache_kv.transpose(0, 1, 3, 4, 5, 2)
    win_t = state_win_kv.transpose(0, 1, 3, 4, 5, 2)

    paged = _compress_paged(page_table, cache_t.reshape(depth, -1, 4, N_KV * HEAD_DIM, PAGE_SIZE),
                            _paged_compress_weights(cmp_pe, cmp_w1, cmp_b1, cmp_w2, cmp_b2, g_kc, seg))
    paged = paged.reshape(depth, 2, db, n_pages, N_KV, 2, HEAD_DIM).transpose(0, 1, 2, 4, 3, 5, 6)
    paged = paged.reshape(depth, 2, db, N_KV, n_cmp_s, HEAD_DIM)

    xp = x_prompt.reshape(b * t, D_MODEL)
    xs = x_sample.reshape(db * dt, D_MODEL)
    outs = {k: [] for k in ("kv_p", "kv_s", "win_p", "win_s", "conv_p", "conv_s")}
    zero_prefix = jnp.zeros((b, CONV_HALO, D_CONV), F32)

    for l in range(depth):
        lw = layers[l]
        mods_p = _split_mod(mod_all[l, :b])
        mods_s = _split_mod(mod_all[l, b:b + db])

        (ug, gm, qn_t, qr_t, rows_t, win_tp, g3_t, cmp_in, ks, kw, vst, vwt) = _inproj(
            xp, mods_p, lw, tabs_p, t, tm_p, True)
        y = _conv_branch(ug.reshape(b, t, D_CONV), zero_prefix, lw, 512)
        n_blk = t // CMP_BLOCK
        kvc = _compress(_flatten_blocks(cmp_in, b, n_blk), lw).reshape(2, b, N_KV, n_blk, HEAD_DIM)
        o_t = _attn_prompt(qn_t, qr_t, g3_t, kvc[0], kvc[1], ks, vst, kw, vwt, b, t)
        x1, h2, dw = _merge(xp, y.reshape(b * t, D_CONV), o_t, gm, mods_p, lw, shared, t, tm_p, True)
        xp = _moe(h2, dw, x1, mods_p, lw, t, tm_p)
        outs["kv_p"].append(rows_t.reshape(b, 4, N_KV, HEAD_DIM, t).transpose(0, 4, 1, 2, 3))
        outs["win_p"].append(win_tp.reshape(b, 2, N_KV, HEAD_DIM, t)[..., t - win_buf:].transpose(0, 4, 1, 2, 3))
        outs["conv_p"].append(ug.reshape(b, t, D_CONV)[:, t - (CONV_WIDTH - 1):])

        ug, gm, qn, qr, rows_new, win_new, g3 = _inproj(xs, mods_s, lw, tabs_s, dt, tm_s, False)
        ug3 = ug.reshape(db, dt, D_CONV)
        conv_ext = jnp.concatenate([state_conv[l], ug3], axis=1)
        prefix = jnp.pad(state_conv[l], ((0, 0), (CONV_HALO - (CONV_WIDTH - 1), 0), (0, 0)))
        y = _conv_branch(ug3, prefix, lw, dt)
        r5 = rows_new.reshape(db, dt, 4, N_KV, HEAD_DIM)
        w5 = win_new.reshape(db, dt, 2, N_KV, HEAD_DIM)
        per_group = lambda a: a.transpose(0, 2, 1, 3)
        q5 = lambda a: a.reshape(db, dt, N_KV, Q_PER_KV, HEAD_DIM)
        q_rows = q5(qr).transpose(0, 2, 3, 1, 4).reshape(db, N_KV, lq_s, HEAD_DIM)
        qn_cols = q5(qn).transpose(0, 2, 4, 3, 1).reshape(db, N_KV, HEAD_DIM, lq_s)
        g3_rows = g3[:, :3 * N_HEADS].reshape(db, dt, 3, N_KV, Q_PER_KV).transpose(0, 3, 4, 1, 2)
        g3_rows = jnp.pad(g3_rows.reshape(db, N_KV, lq_s, 3), ((0, 0), (0, 0), (0, 0), (0, 5)))
        o_s = _attn_sample(page_table, cache_t, l, q_rows, qn_cols, paged[l, 0], paged[l, 1],
                           per_group(r5[:, :, 2]), per_group(r5[:, :, 3]), per_group(w5[:, :, 0]),
                           per_group(w5[:, :, 1]), win_t, g3_rows, expand, rsum, past_len, dt)
        o = o_s.reshape(db, N_KV, Q_PER_KV, dt, HEAD_DIM).transpose(0, 3, 1, 2, 4).reshape(db * dt, 512).astype(BF16)
        x1, h2, dw = _merge(xs, y.reshape(db * dt, D_CONV), o, gm, mods_s, lw, shared, dt, tm_s, False)
        xs = _moe(h2, dw, x1, mods_s, lw, dt, tm_s)
        outs["kv_s"].append(r5)
        outs["win_s"].append(jnp.concatenate([state_win_kv[l], w5], axis=1)[:, dt:])
        outs["conv_s"].append(conv_ext[:, dt:])

    return (xp.reshape(b, t, D_MODEL), xs.reshape(db, dt, D_MODEL),
            jnp.stack(outs["kv_p"]), jnp.stack(outs["kv_s"]), jnp.stack(outs["win_p"]), jnp.stack(outs["win_s"]),
            jnp.stack(outs["conv_p"]), jnp.stack(outs["conv_s"]))
```
